```python
import math
import jax, jax.numpy as jnp
from jax import lax
import numpy as np

D_MODEL = 1024
BATCH = 16
SEQ = 4096
DEPTH = 1

CHUNK = 64
Q_BLOCK = 128
M_HEADS = 4
M_QK_DIM = 64
M_V_DIM = 128
CONV_WIDTH = 4
A_HEADS = 4
A_HEAD_DIM = 64
A_V_DIM = 2 * A_HEAD_DIM
ROPE_THETA = 10000.0
M_WIDTH = M_HEADS * M_V_DIM
A_WIDTH = A_HEADS * A_V_DIM
MIX_WIDTH = M_WIDTH + A_WIDTH
SPLIT_SIZES = (M_HEADS * M_QK_DIM, M_HEADS * M_QK_DIM, M_WIDTH, M_WIDTH, M_HEADS, M_HEADS,
               A_HEADS * 2 * A_HEAD_DIM, A_HEADS * 2 * A_HEAD_DIM, A_WIDTH)
IN_WIDTH = sum(SPLIT_SIZES)
N_EXPERTS = 32
TOP_K = 4
D_FF = D_MODEL
SWIGLU_ALPHA = 1.702
SWIGLU_LIMIT = 7.0
EXPERT_BLOCK = 512
EPS = 1e-6

kernel_name = "hybrid_mlstm_diffattn_moe_block"


def rms_norm(t, g):
    tf = t.astype(jnp.float32)
    y = tf * lax.rsqrt(jnp.mean(tf * tf, axis=-1, keepdims=True) + EPS)
    return (y * g.astype(jnp.float32)).astype(t.dtype)


def causal_depthwise_conv(u, w, b):
    K, C = w.shape
    y = lax.conv_general_dilated(u, w[:, None, :].astype(u.dtype), window_strides=(1,),
                                 padding=[(K - 1, 0)], dimension_numbers=('NWC', 'WIO', 'NWC'),
                                 feature_group_count=C)
    return y + b.astype(u.dtype)


def rope(t, cos, sin):
    half = t.shape[-1] // 2
    t1, t2 = t[..., :half], t[..., half:]
    return jnp.concatenate([t1 * cos - t2 * sin, t2 * cos + t1 * sin], axis=-1).astype(t.dtype)


def to_chunks(t):
    B, S, H, d = t.shape
    return t.reshape(B, S // CHUNK, CHUNK, H, d).transpose(0, 3, 1, 2, 4)


def mlstm_chunkwise(q, k, v, i_pre, logf):
    L = q.shape[3]
    q, k, v = (t.astype(jnp.float32) for t in (q, k, v))
    k = k * (k.shape[-1] ** -0.5)
    b = jnp.cumsum(logf, axis=-1)
    b_last = b[..., -1]
    a = b_last[..., None] - b + i_pre
    m_loc = jnp.max(a, axis=-1)
    w_s = jnp.exp(a - m_loc[..., None])
    C_loc = jnp.einsum('bhnsv,bhnsk->bhnvk', v * w_s[..., None], k)
    n_loc = jnp.einsum('bhns,bhnsk->bhnk', w_s, k)

    def step(carry, inp):
        C, n, m = carry
        Cl, nl, ml, bl = inp
        m_new = jnp.maximum(bl + m, ml)
        decay = jnp.exp(bl + m - m_new)
        fresh = jnp.exp(ml - m_new)
        C_new = decay[..., None, None] * C + fresh[..., None, None] * Cl
        n_new = decay[..., None] * n + fresh[..., None] * nl
        return (C_new, n_new, m_new), (C, n, m)

    Bsz, H, _, _, dk = k.shape
    dv = v.shape[-1]
    init = (jnp.zeros((Bsz, H, dv, dk), jnp.float32), jnp.zeros((Bsz, H, dk), jnp.float32),
            jnp.zeros((Bsz, H), jnp.float32))
    xs = tuple(jnp.moveaxis(t, 2, 0) for t in (C_loc, n_loc, m_loc, b_last))
    _, states = lax.scan(step, init, xs)
    C_prev, n_prev, m_prev = (jnp.moveaxis(t, 0, 2) for t in states)

    causal = jnp.tril(jnp.ones((L, L), dtype=bool))
    d_log = jnp.where(causal, b[..., :, None] - b[..., None, :] + i_pre[..., None, :], -jnp.inf)
    m_inter = b + m_prev[..., None]
    m_t = jnp.maximum(m_inter, jnp.max(d_log, axis=-1))
    d_w = jnp.exp(d_log - m_t[..., None])
    inter_w = jnp.exp(m_inter - m_t)
    s = jnp.einsum('bhntk,bhnsk->bhnts', q, k) * d_w
    num = jnp.einsum('bhnts,bhnsv->bhntv', s, v) + inter_w[..., None] * jnp.einsum('bhnvk,bhntk->bhntv', C_prev, q)
    den = jnp.sum(s, axis=-1) + inter_w * jnp.einsum('bhnk,bhntk->bhnt', n_prev, q)
    return num / jnp.maximum(jnp.abs(den), jnp.exp(-m_t))[..., None]


def diff_attention(q, k, v, lam):
    B, H, _, S, dh = q.shape
    dv = v.shape[-1]
    key_chunk = jnp.arange(S) // CHUNK
    scale = dh ** -0.5

    def block(i):
        start = i * Q_BLOCK
        qb = lax.dynamic_slice_in_dim(q, start, Q_BLOCK, axis=3)
        s = jnp.einsum('bhmqd,bhmkd->bhmqk', qb, k).astype(jnp.float32) * scale
        q_chunk = (start + jnp.arange(Q_BLOCK)) // CHUNK
        mask = key_chunk[None, :] <= q_chunk[:, None]
        p = jax.nn.softmax(jnp.where(mask, s, -jnp.inf), axis=-1)
        a = p[:, :, 0] - lam * p[:, :, 1]
        return jnp.einsum('bhqk,bhkv->bhqv', a.astype(v.dtype), v)

    o = lax.map(block, jnp.arange(S // Q_BLOCK))
    return o.transpose(1, 0, 3, 2, 4).reshape(B, S, H, dv)


def moe_ffn(h, w_router, b_router, w_gu, b_gu, w_down, b_down):
    B, S, D = h.shape
    N = B * S
    hf = h.reshape(N, D)
    logits = jnp.dot(hf, w_router).astype(jnp.float32) + b_router.astype(jnp.float32)
    top_val, top_idx = lax.top_k(logits, TOP_K)
    gates = jax.nn.softmax(top_val, axis=-1)
    n_slots = N * TOP_K
    e_flat = top_idx.reshape(-1)
    tok_flat = jnp.arange(n_slots, dtype=jnp.int32) // TOP_K
    g_flat = gates.reshape(-1)
    order = jnp.argsort(e_flat)
    e_sorted = e_flat[order]
    sizes = jnp.bincount(e_flat, length=N_EXPERTS)
    group_start = jnp.cumsum(sizes) - sizes
    padded = ((sizes + EXPERT_BLOCK - 1) // EXPERT_BLOCK) * EXPERT_BLOCK
    padded_end = jnp.cumsum(padded)
    padded_start = padded_end - padded
    dest = padded_start[e_sorted] + (jnp.arange(n_slots) - group_start[e_sorted])
    n_rows = ((n_slots + EXPERT_BLOCK - 1) // EXPERT_BLOCK) * EXPERT_BLOCK + N_EXPERTS * EXPERT_BLOCK
    row_tok = jnp.zeros((n_rows,), jnp.int32).at[dest].set(tok_flat[order])
    row_gate = jnp.zeros((n_rows,), jnp.float32).at[dest].set(g_flat[order])
    n_blocks = n_rows // EXPERT_BLOCK
    block_expert = jnp.minimum(jnp.searchsorted(padded_end, jnp.arange(n_blocks) * EXPERT_BLOCK, side='right'),
                               N_EXPERTS - 1)

    def body(y, inp):
        bi, e = inp
        idx = lax.dynamic_slice_in_dim(row_tok, bi * EXPERT_BLOCK, EXPERT_BLOCK)
        g = lax.dynamic_slice_in_dim(row_gate, bi * EXPERT_BLOCK, EXPERT_BLOCK)
        xb = hf[idx]
        gu = jnp.dot(xb, w_gu[e]) + b_gu[e]
        gate = jnp.minimum(gu[:, 0::2], SWIGLU_LIMIT)
        up = jnp.clip(gu[:, 1::2], -SWIGLU_LIMIT, SWIGLU_LIMIT)
        act = (up + 1.0) * (gate * jax.nn.sigmoid(SWIGLU_ALPHA * gate))
        out = jnp.dot(act, w_down[e]) + b_down[e]
        y = y.at[idx].add(out * g[:, None].astype(out.dtype))
        return y, None

    y, _ = lax.scan(body, jnp.zeros_like(hf), (jnp.arange(n_blocks), block_expert))
    return y.reshape(B, S, D)


def setup_inputs(seed: int = 0) -> dict:
    key = jax.random.key(seed)
    ks = jax.random.split(key, 32)
    f32 = jnp.float32
    L, D = DEPTH, D_MODEL
    nrm = lambda k, shape, s: jax.random.normal(k, shape, f32) * s
    return {
        "x": nrm(ks[0], (BATCH, SEQ, D), 1.0),
        "c": nrm(ks[1], (BATCH, D), 1.0),
        "w_ada": nrm(ks[2], (L, D, 6 * D), 0.5 * D ** -0.5),
        "b_ada": nrm(ks[3], (L, 6 * D), 0.02),
        "norm1_g": 1.0 + nrm(ks[4], (L, D), 0.02),
        "w_in": nrm(ks[5], (L, D, IN_WIDTH), D ** -0.5),
        "conv_w": nrm(ks[6], (L, CONV_WIDTH, 2 * M_HEADS * M_QK_DIM), CONV_WIDTH ** -0.5),
        "conv_b": nrm(ks[7], (L, 2 * M_HEADS * M_QK_DIM), 0.02),
        "b_igate": nrm(ks[8], (L, M_HEADS), 0.1),
        "b_fgate": jnp.linspace(3.0, 6.0, M_HEADS, dtype=f32)[None, :] + nrm(ks[9], (L, M_HEADS), 0.1),
        "mlstm_norm_g": 1.0 + nrm(ks[10], (L, M_WIDTH), 0.02),
        "q_norm_g": 1.0 + nrm(ks[11], (L, A_HEAD_DIM), 0.02),
        "k_norm_g": 1.0 + nrm(ks[12], (L, A_HEAD_DIM), 0.02),
        "lambda_q1": nrm(ks[13], (L, A_HEAD_DIM), 0.1),
        "lambda_k1": nrm(ks[14], (L, A_HEAD_DIM), 0.1),
        "lambda_q2": nrm(ks[15], (L, A_HEAD_DIM), 0.1),
        "lambda_k2": nrm(ks[16], (L, A_HEAD_DIM), 0.1),
        "diff_norm_g": 1.0 + nrm(ks[17], (L, A_WIDTH), 0.02),
        "w_out": nrm(ks[18], (L, MIX_WIDTH, D), MIX_WIDTH ** -0.5),
        "norm2_g": 1.0 + nrm(ks[19], (L, D), 0.02),
        "w_router": nrm(ks[20], (L, D, N_EXPERTS), D ** -0.5),
        "b_router": nrm(ks[21], (L, N_EXPERTS), 0.01),
        "w_gu": nrm(ks[22], (L, N_EXPERTS, D, 2 * D_FF), D ** -0.5),
        "b_gu": nrm(ks[23], (L, N_EXPERTS, 2 * D_FF), 0.01),
        "w_down": nrm(ks[24], (L, N_EXPERTS, D_FF, D), D_FF ** -0.5),
        "b_down": nrm(ks[25], (L, N_EXPERTS, D), 0.01),
    }


def reference(x, c, w_ada, b_ada, norm1_g, w_in, conv_w, conv_b, b_igate, b_fgate, mlstm_norm_g,
              q_norm_g, k_norm_g, lambda_q1, lambda_k1, lambda_q2, lambda_k2, diff_norm_g, w_out,
              norm2_g, w_router, b_router, w_gu, b_gu, w_down, b_down):
    B, S, D = x.shape
    split_idx = np.cumsum(SPLIT_SIZES)[:-1].tolist()
    pos = jnp.arange(S, dtype=jnp.float32)
    inv_freq = ROPE_THETA ** (-jnp.arange(0, A_HEAD_DIM, 2, dtype=jnp.float32) / A_HEAD_DIM)
    ang = pos[:, None] * inv_freq[None, :]
    cos, sin = jnp.cos(ang).astype(x.dtype), jnp.sin(ang).astype(x.dtype)
    cond = jax.nn.silu(c)
    for l in range(DEPTH):
        lambda_init = 0.8 - 0.6 * math.exp(-0.3 * l)
        mod = jnp.dot(cond, w_ada[l]) + b_ada[l]
        shift1, scale1, gate1, shift2, scale2, gate2 = (m[:, None, :] for m in jnp.split(mod, 6, axis=-1))

        h = rms_norm(x, norm1_g[l]) * (1.0 + scale1) + shift1
        p = jnp.dot(h, w_in[l])
        mq, mk, mv, mo, mi, mf, aq, ak, av = jnp.split(p, split_idx, axis=-1)

        qk = jax.nn.silu(causal_depthwise_conv(jnp.concatenate([mq, mk], axis=-1), conv_w[l], conv_b[l]))
        mq, mk = jnp.split(qk, 2, axis=-1)
        mq = mq.reshape(B, S, M_HEADS, M_QK_DIM)
        mk = mk.reshape(B, S, M_HEADS, M_QK_DIM)
        mv = mv.reshape(B, S, M_HEADS, M_V_DIM)
        i_pre = (mi.astype(jnp.float32) + b_igate[l].astype(jnp.float32)).reshape(B, S // CHUNK, CHUNK, M_HEADS).transpose(0, 3, 1, 2)
        logf = jax.nn.log_sigmoid(mf.astype(jnp.float32) + b_fgate[l].astype(jnp.float32)).reshape(B, S // CHUNK, CHUNK, M_HEADS).transpose(0, 3, 1, 2)
        hm = mlstm_chunkwise(to_chunks(mq), to_chunks(mk), to_chunks(mv), i_pre, logf)
        hm = hm.transpose(0, 2, 3, 1, 4).reshape(B, S, M_HEADS, M_V_DIM).astype(x.dtype)
        hm = rms_norm(hm, mlstm_norm_g[l].reshape(M_HEADS, M_V_DIM))
        hm = (hm * jax.nn.sigmoid(mo).reshape(B, S, M_HEADS, M_V_DIM)).reshape(B, S, M_WIDTH)

        aq = rms_norm(aq.reshape(B, S, A_HEADS, 2, A_HEAD_DIM), q_norm_g[l]).transpose(0, 2, 3, 1, 4)
        ak = rms_norm(ak.reshape(B, S, A_HEADS, 2, A_HEAD_DIM), k_norm_g[l]).transpose(0, 2, 3, 1, 4)
        aq, ak = rope(aq, cos, sin), rope(ak, cos, sin)
        av = av.reshape(B, S, A_HEADS, A_V_DIM).transpose(0, 2, 1, 3)
        lam = (jnp.exp(jnp.sum(lambda_q1[l].astype(jnp.float32) * lambda_k1[l].astype(jnp.float32)))
               - jnp.exp(jnp.sum(lambda_q2[l].astype(jnp.float32) * lambda_k2[l].astype(jnp.float32)))
               + lambda_init)
        ha = diff_attention(aq, ak, av, lam)
        ha = (rms_norm(ha, diff_norm_g[l].reshape(A_HEADS, A_V_DIM)) * (1.0 - lambda_init)).reshape(B, S, A_WIDTH)

        mix = jnp.dot(jnp.concatenate([hm, ha], axis=-1), w_out[l])
        x = x + gate1 * mix

        h2 = rms_norm(x, norm2_g[l]) * (1.0 + scale2) + shift2
        x = x + gate2 * moe_ffn(h2, w_router[l], b_router[l], w_gu[l], b_gu[l], w_down[l], b_down[l])
    return x
```

```python
import functools
import math

import jax
import jax.numpy as jnp
from jax import lax
from jax.experimental import pallas as pl
from jax.experimental.pallas import tpu as pltpu

F32 = jnp.float32
BF16 = jnp.bfloat16
HIGHEST = lax.Precision.HIGHEST

LANES = 128
SUBLANES = 8
VMEM_LIMIT = 48 * 1024 * 1024

CHUNK = 64
M_HEADS = 4
M_QK_DIM = 64
M_V_DIM = 128
CONV_WIDTH = 4
A_HEADS = 4
A_HEAD_DIM = 64
A_V_DIM = 128
ROPE_THETA = 10000.0
N_EXPERTS = 32
TOP_K = 4
SWIGLU_ALPHA = 1.702
SWIGLU_LIMIT = 7.0
EPS = 1e-6
LAMBDA_INIT = 0.8 - 0.6 * math.exp(-0.3 * 0)

TM = 512
GCH = 8
TQ = 256
RB = 512
TC = 512
GW = 128


def _dot(a, b):
    return jnp.dot(a, b, preferred_element_type=F32)


def _cparams(sem):
    return pltpu.CompilerParams(dimension_semantics=sem, vmem_limit_bytes=VMEM_LIMIT)


def _ada_kernel(c_ref, w_ref, b_ref, o_ref):
    c = c_ref[...]
    cond = c * jax.nn.sigmoid(c)
    o_ref[...] = jnp.dot(cond, w_ref[...], preferred_element_type=F32, precision=HIGHEST) + b_ref[...]


def _ada(c, w, b):
    B, D = c.shape
    n = w.shape[1]
    tn = 1024
    return pl.pallas_call(
        _ada_kernel,
        grid=(n // tn,),
        in_specs=[pl.BlockSpec((B, D), lambda j: (0, 0)),
                  pl.BlockSpec((D, tn), lambda j: (0, j)),
                  pl.BlockSpec((1, tn), lambda j: (0, j))],
        out_specs=pl.BlockSpec((B, tn), lambda j: (0, j)),
        out_shape=jax.ShapeDtypeStruct((B, n), F32),
        compiler_params=_cparams(("arbitrary",)),
        name="ada",
    )(c, w, b.reshape(1, n))


def _inproj_kernel(x_ref, mod_ref, g1_ref, w_ref, wgh_ref, wgl_ref, cos_ref, sin_ref, qg_ref, kg_ref,
                   bd_ref, p_ref, gates_ref):
    x = x_ref[0]
    ms = jnp.mean(x * x, axis=-1, keepdims=True)
    shift = mod_ref[0, 0:1, :]
    scale = mod_ref[0, 1:2, :]
    h = (x * lax.rsqrt(ms + EPS) * g1_ref[...]) * (1.0 + scale) + shift
    hb = h.astype(BF16)
    hl = (h - hb.astype(F32)).astype(BF16)
    gates_ref[0] = _dot(hb, wgh_ref[...]) + _dot(hl, wgh_ref[...]) + _dot(hb, wgl_ref[...])

    tm = x.shape[0]
    lane = lax.broadcasted_iota(jnp.int32, (tm, 512), 1)
    first_half = (lane & 63) < 32
    cos = jnp.concatenate([cos_ref[...]] * 4, axis=1)
    sin = jnp.concatenate([sin_ref[...]] * 4, axis=1)
    for sec in range(6):
        acc = _dot(hb, w_ref[:, sec * 512:(sec + 1) * 512])
        if sec in (3, 4):
            g = qg_ref[...] if sec == 3 else kg_ref[...]
            ssq = _dot((acc * acc).astype(BF16), bd_ref[...])
            y = acc * lax.rsqrt(ssq * (1.0 / A_HEAD_DIM) + EPS) * g
            swapped = jnp.where(first_half, pltpu.roll(y, 512 - 32, 1), pltpu.roll(y, 32, 1))
            acc = y * cos + swapped * sin
            if sec == 3:
                acc = acc * (A_HEAD_DIM ** -0.5)
        p_ref[0, :, sec * 512:(sec + 1) * 512] = acc.astype(BF16)


def _inproj(x, mod3, g1, w_main, wg_hi, wg_lo, cos_t, sin_t, qg, kg, bd):
    B, S, D = x.shape
    nw = w_main.shape[1]
    return pl.pallas_call(
        _inproj_kernel,
        grid=(B, S // TM),
        in_specs=[pl.BlockSpec((1, TM, D), lambda b, s: (b, s, 0)),
                  pl.BlockSpec((1, 6, D), lambda b, s: (b, 0, 0)),
                  pl.BlockSpec((1, D), lambda b, s: (0, 0)),
                  pl.BlockSpec((D, nw), lambda b, s: (0, 0)),
                  pl.BlockSpec((D, GW), lambda b, s: (0, 0)),
                  pl.BlockSpec((D, GW), lambda b, s: (0, 0)),
                  pl.BlockSpec((TM, LANES), lambda b, s: (s, 0)),
                  pl.BlockSpec((TM, LANES), lambda b, s: (s, 0)),
                  pl.BlockSpec((1, 512), lambda b, s: (0, 0)),
                  pl.BlockSpec((1, 512), lambda b, s: (0, 0)),
                  pl.BlockSpec((512, 512), lambda b, s: (0, 0))],
        out_specs=[pl.BlockSpec((1, TM, nw), lambda b, s: (b, s, 0)),
                   pl.BlockSpec((1, TM, GW), lambda b, s: (b, s, 0))],
        out_shape=[jax.ShapeDtypeStruct((B, S, nw), BF16),
                   jax.ShapeDtypeStruct((B, S, GW), F32)],
        compiler_params=_cparams(("arbitrary", "arbitrary")),
        name="inproj",
    )(x, mod3, g1, w_main, wg_hi, wg_lo, cos_t, sin_t, qg, kg, bd)


def _log_sigmoid(z):
    return jnp.minimum(z, 0.0) - jnp.log1p(jnp.exp(-jnp.abs(z)))


def _mlstm_kernel(qk_ref, v_ref, o_ref, gc_ref, gr_ref, bc_ref, br_ref, cw_ref, cb_ref, ng_ref,
                  out_ref, ubuf, q_sc, k_sc, gcs, grs, cst, msc):
    g = pl.program_id(1)
    T = GCH * CHUNK
    HW = M_HEADS * M_QK_DIM

    @pl.when(g == 0)
    def _():
        cst[...] = jnp.zeros_like(cst)
        msc[...] = jnp.zeros_like(msc)
        ubuf[0:SUBLANES, :] = jnp.zeros((SUBLANES, 2 * HW), F32)

    @pl.when(g > 0)
    def _():
        ubuf[0:SUBLANES, :] = ubuf[T:T + SUBLANES, :]

    ubuf[SUBLANES:SUBLANES + T, :] = qk_ref[0].astype(F32)
    y = cb_ref[...]
    for j in range(CONV_WIDTH):
        off = SUBLANES - (CONV_WIDTH - 1) + j
        y = y + cw_ref[j:j + 1, :] * ubuf[off:off + T, :]
    qk = y * jax.nn.sigmoid(y)
    q_sc[...] = qk[:, :HW].astype(BF16)
    k_sc[...] = (qk[:, HW:] * (M_QK_DIM ** -0.5)).astype(BF16)

    gc = gc_ref[0] + bc_ref[...]
    lane = lax.broadcasted_iota(jnp.int32, gc.shape, 1)
    gcs[...] = jnp.where(lane < M_HEADS, gc, _log_sigmoid(gc))
    gr = gr_ref[0] + br_ref[...]
    row = lax.broadcasted_iota(jnp.int32, gr.shape, 1)
    grs[...] = jnp.where(row < M_HEADS, gr, _log_sigmoid(gr))

    ti = lax.broadcasted_iota(jnp.int32, (CHUNK, CHUNK), 0)
    si = lax.broadcasted_iota(jnp.int32, (CHUNK, CHUNK), 1)
    causal = si <= ti
    tri = causal.astype(F32)
    tri_t = (ti <= si).astype(F32)
    lane256 = lax.broadcasted_iota(jnp.int32, (CHUNK, HW), 1)
    lane128 = lax.broadcasted_iota(jnp.int32, (CHUNK, LANES), 1)
    ones_blk = jnp.where(lane128 == 0, 1.0, 0.0).astype(BF16)

    def chunk(c, carry):
        r0 = pl.multiple_of(c * CHUNK, CHUNK)
        qc = q_sc[pl.ds(r0, CHUNK), :]
        kc = k_sc[pl.ds(r0, CHUNK), :]
        vc = v_ref[0, pl.ds(r0, CHUNK), :]
        gcc = gcs[pl.ds(r0, CHUNK), :]
        grr = grs[c]
        b_c = jnp.dot(tri, gcc, preferred_element_type=F32, precision=HIGHEST)
        b_r = jnp.dot(grr, tri_t, preferred_element_type=F32, precision=HIGHEST)
        qstack = jnp.concatenate(
            [jnp.where((lane256 // M_QK_DIM) == h, qc, jnp.zeros_like(qc)) for h in range(M_HEADS)], axis=0)
        kstack = jnp.concatenate(
            [jnp.where((lane256 // M_QK_DIM) == h, kc, jnp.zeros_like(kc)) for h in range(M_HEADS)], axis=0)
        s_all = lax.dot_general(qstack, kc, (((1,), (1,)), ((), ())), preferred_element_type=F32)
        i_all = _dot(qstack, cst[...].astype(BF16))
        vws, decays, freshs = [], [], []
        for h in range(M_HEADS):
            rs = slice(h * CHUNK, (h + 1) * CHUNK)
            bcol = b_c[:, M_HEADS + h:M_HEADS + h + 1]
            icol = gcc[:, h:h + 1]
            brow = b_r[M_HEADS + h:M_HEADS + h + 1, :]
            irow = grr[h:h + 1, :]
            b_last = brow[:, CHUNK - 1:CHUNK]
            m_prev = msc[h:h + 1, 0:1]
            dlog = jnp.where(causal, bcol - brow + irow, -jnp.inf)
            m_intra = jnp.max(dlog, axis=-1, keepdims=True)
            m_inter = bcol + m_prev
            m_t = jnp.maximum(m_inter, m_intra)
            d_w = jnp.exp(dlog - m_t)
            inter_w = jnp.exp(m_inter - m_t)
            p =(s_all[rs, :] * d_w).astype(BF16)
            vaug = jnp.concatenate([vc[:, h * M_V_DIM:(h + 1) * M_V_DIM], ones_blk], axis=1)
            r = _dot(p, vaug) + inter_w * i_all[rs, :]
            num = r[:, :M_V_DIM]
            den = r[:, M_V_DIM:M_V_DIM + 1]
            hv = num / jnp.maximum(jnp.abs(den), jnp.exp(-m_t))
            hn = hv * lax.rsqrt(jnp.mean(hv * hv, axis=-1, keepdims=True) + EPS) * ng_ref[:, h * M_V_DIM:(h + 1) * M_V_DIM]
            og = o_ref[0, pl.ds(r0, CHUNK), h * M_V_DIM:(h + 1) * M_V_DIM].astype(F32)
            out_ref[0, pl.ds(r0, CHUNK), h * M_V_DIM:(h + 1) * M_V_DIM] = (hn * jax.nn.sigmoid(og)).astype(BF16)
            a_col = b_last - bcol + icol
            m_loc = jnp.max(a_col, axis=0, keepdims=True)
            w_col = jnp.exp(a_col - m_loc)
            vws.append((vaug.astype(F32) * w_col).astype(BF16))
            m_new = jnp.maximum(b_last + m_prev, m_loc)
            decays.append(jnp.exp(b_last + m_prev - m_new))
            freshs.append(jnp.exp(m_loc - m_new))
            msc[h:h + 1, :] = jnp.broadcast_to(m_new, (1, LANES))
        c_loc = lax.dot_general(kstack, jnp.concatenate(vws, axis=0), (((0,), (0,)), ((), ())),
                                preferred_element_type=F32)
        for h in range(M_HEADS):
            ks = slice(h * M_QK_DIM, (h + 1) * M_QK_DIM)
            cst[ks, :] = decays[h] * cst[ks, :] + freshs[h] * c_loc[ks, :]
        return carry

    lax.fori_loop(0, GCH, chunk, 0)


def _mlstm(p_all, gates_c, gates_r, bias_c, bias_r, conv_w, conv_b, ng):
    B, S, _ = p_all.shape
    T = GCH * CHUNK
    return pl.pallas_call(
        _mlstm_kernel,
        grid=(B, S // T),
        in_specs=[pl.BlockSpec((1, T, 512), lambda b, g: (b, g, 0)),
                  pl.BlockSpec((1, T, 512), lambda b, g: (b, g, 1)),
                  pl.BlockSpec((1, T, 512), lambda b, g: (b, g, 2)),
                  pl.BlockSpec((1, T, GW), lambda b, g: (b, g, 0)),
                  pl.BlockSpec((1, GCH, SUBLANES, CHUNK), lambda b, g: (b, g, 0, 0)),
                  pl.BlockSpec((1, GW), lambda b, g: (0, 0)),
                  pl.BlockSpec((SUBLANES, 1), lambda b, g: (0, 0)),
                  pl.BlockSpec((CONV_WIDTH, 512), lambda b, g: (0, 0)),
                  pl.BlockSpec((1, 512), lambda b, g: (0, 0)),
                  pl.BlockSpec((1, 512), lambda b, g: (0, 0))],
        out_specs=pl.BlockSpec((1, T, 512), lambda b, g: (b, g, 0)),
        out_shape=jax.ShapeDtypeStruct((B, S, 512), BF16),
        scratch_shapes=[pltpu.VMEM((T + SUBLANES, 512), F32),
                        pltpu.VMEM((T, 256), BF16),
                        pltpu.VMEM((T, 256), BF16),
                        pltpu.VMEM((T, GW), F32),
                        pltpu.VMEM((GCH, SUBLANES, CHUNK), F32),
                        pltpu.VMEM((M_HEADS * M_QK_DIM, 2 * M_V_DIM), F32),
                        pltpu.VMEM((SUBLANES, LANES), F32)],
        compiler_params=_cparams(("arbitrary", "arbitrary")),
        name="mlstm",
    )(p_all, p_all, p_all, gates_c, gates_r, bias_c, bias_r, conv_w, conv_b, ng)


def _attn_kernel(lq1_ref, lk1_ref, lq2_ref, lk2_ref, q_ref, k_ref, v_ref, ng_ref, o_ref,
                 qs_sc, m_sc, l_sc, acc_sc):
    i = pl.program_id(2)
    q = q_ref[0]
    lane = lax.broadcasted_iota(jnp.int32, q.shape, 1)
    qs_sc[0:TQ, :] = jnp.where(lane < A_HEAD_DIM, q, jnp.zeros_like(q))
    qs_sc[TQ:2 * TQ, :] = jnp.where(lane >= A_HEAD_DIM, q, jnp.zeros_like(q))
    m_sc[...] = jnp.full(m_sc.shape, -jnp.inf, F32)
    l_sc[...] = jnp.zeros_like(l_sc)
    acc_sc[...] = jnp.zeros_like(acc_sc)

    ri = lax.broadcasted_iota(jnp.int32, (2 * TQ, TQ), 0)
    ci = lax.broadcasted_iota(jnp.int32, (2 * TQ, TQ), 1)
    visible = (ci // CHUNK) <= ((ri & (TQ - 1)) // CHUNK)

    def kv_step(j, masked):
        r0 = pl.multiple_of(j * TQ, TQ)
        k = k_ref[0, pl.ds(r0, TQ), :]
        v = v_ref[0, pl.ds(r0, TQ), :]
        s = lax.dot_general(qs_sc[...], k, (((1,), (1,)), ((), ())), preferred_element_type=F32)
        if masked:
            s = jnp.where(visible, s, -jnp.inf)
        m_old = m_sc[...]
        m_new = jnp.maximum(m_old, jnp.max(s, axis=-1, keepdims=True))
        alpha = jnp.exp(m_old - m_new)
        p = jnp.exp(s - jnp.concatenate([m_new] * (TQ // LANES), axis=1))
        l_sc[...] = alpha * l_sc[...] + jnp.sum(p, axis=-1, keepdims=True)
        acc_sc[...] = alpha * acc_sc[...] + _dot(p.astype(BF16), v)
        m_sc[...] = m_new

    def body(j, carry):
        kv_step(j, False)
        return carry

    lax.fori_loop(0, i, body, 0)
    kv_step(i, True)

    o = acc_sc[...] / l_sc[...]
    lam = (jnp.exp(jnp.sum(lq1_ref[...] * lk1_ref[...], axis=-1, keepdims=True))
           - jnp.exp(jnp.sum(lq2_ref[...] * lk2_ref[...], axis=-1, keepdims=True)) + LAMBDA_INIT)
    a = o[0:TQ, :] - lam * o[TQ:2 * TQ, :]
    y = a * lax.rsqrt(jnp.mean(a * a, axis=-1, keepdims=True) + EPS) * ng_ref[...]
    o_ref[0] = (y * (1.0 - LAMBDA_INIT)).astype(BF16)


def _attn(p_all, lq1, lk1, lq2, lk2, ng):
    B, S, _ = p_all.shape
    nsec = 512 // LANES
    lam_spec = pl.BlockSpec((1, A_HEAD_DIM), lambda b, h, i: (0, 0))
    return pl.pallas_call(
        _attn_kernel,
        grid=(B, A_HEADS, S // TQ),
        in_specs=[lam_spec, lam_spec, lam_spec, lam_spec,
                  pl.BlockSpec((1, TQ, LANES), lambda b, h, i: (b, i, 3 * nsec + h)),
                  pl.BlockSpec((1, S, LANES), lambda b, h, i: (b, 0, 4 * nsec + h)),
                  pl.BlockSpec((1, S, LANES), lambda b, h, i: (b, 0, 5 * nsec + h)),
                  pl.BlockSpec((1, LANES), lambda b, h, i: (0, h))],
        out_specs=pl.BlockSpec((1, TQ, LANES), lambda b, h, i: (b, i, h)),
        out_shape=jax.ShapeDtypeStruct((B, S, 512), BF16),
        scratch_shapes=[pltpu.VMEM((2 * TQ, LANES), BF16),
                        pltpu.VMEM((2 * TQ, LANES), F32),
                        pltpu.VMEM((2 * TQ, LANES), F32),
                        pltpu.VMEM((2 * TQ, LANES), F32)],
        compiler_params=_cparams(("arbitrary", "arbitrary", "arbitrary")),
        name="attn",
    )(lq1, lk1, lq2, lk2, p_all, p_all, p_all, ng)


def _outproj_kernel(hm_ref, ha_ref, x_ref, mod_ref, g2_ref, wo_ref, wrh_ref, wrl_ref, br_ref,
                    x1_ref, h2_ref, idx_ref, gate_ref):
    hcat = jnp.concatenate([hm_ref[0], ha_ref[0]], axis=1)
    mix = _dot(hcat, wo_ref[...])
    gate1 = mod_ref[0, 2:3, :]
    shift2 = mod_ref[0, 3:4, :]
    scale2 = mod_ref[0, 4:5, :]
    x1 = x_ref[0] + gate1 * mix
    x1_ref[0] = x1
    ms = jnp.mean(x1 * x1, axis=-1, keepdims=True)
    h2 = (x1 * lax.rsqrt(ms + EPS) * g2_ref[...]) * (1.0 + scale2) + shift2
    tm = h2.shape[0]
    for s in range(SUBLANES):
        h2_ref[pl.ds(s, tm, stride=SUBLANES), :] = h2[:, s * LANES:(s + 1) * LANES]
    hb = h2.astype(BF16)
    hl = (h2 - hb.astype(F32)).astype(BF16)
    logits = _dot(hb, wrh_ref[...]) + _dot(hl, wrh_ref[...]) + _dot(hb, wrl_ref[...]) + br_ref[...]
    lane = lax.broadcasted_iota(jnp.int32, logits.shape, 1).astype(F32)
    vals, idxs = [], []
    work = logits
    for _ in range(TOP_K):
        mx = jnp.max(work, axis=-1, keepdims=True)
        ix = jnp.min(jnp.where(work == mx, lane, float(LANES)), axis=-1, keepdims=True)
        vals.append(mx)
        idxs.append(ix)
        work = jnp.where(lane == ix, -jnp.inf, work)
    es = [jnp.exp(v - vals[0]) for v in vals]
    tot = es[0] + es[1] + es[2] + es[3]
    gsel = jnp.zeros_like(logits)
    isel = jnp.zeros_like(logits)
    for k in range(TOP_K):
        gsel = jnp.where(lane == float(k), es[k] / tot, gsel)
        isel = jnp.where(lane == float(k), idxs[k], isel)
    gate_ref[0] = gsel[:, :TOP_K]
    idx_ref[0] = isel[:, :TOP_K].astype(jnp.int32)


def _outproj(hm, ha, x, mod3, g2, wo, wr_hi, wr_lo, br):
    B, S, D = x.shape
    nt = S // TM
    return pl.pallas_call(
        _outproj_kernel,
        grid=(B, nt),
        in_specs=[pl.BlockSpec((1, TM, 512), lambda b, s: (b, s, 0)),
                  pl.BlockSpec((1, TM, 512), lambda b, s: (b, s, 0)),
                  pl.BlockSpec((1, TM, D), lambda b, s: (b, s, 0)),
                  pl.BlockSpec((1, 6, D), lambda b, s: (b, 0, 0)),
                  pl.BlockSpec((1, D), lambda b, s: (0, 0)),
                  pl.BlockSpec((D, D), lambda b, s: (0, 0)),
                  pl.BlockSpec((D, LANES), lambda b, s: (0, 0)),
                  pl.BlockSpec((D, LANES), lambda b, s: (0, 0)),
                  pl.BlockSpec((1, LANES), lambda b, s: (0, 0))],
        out_specs=[pl.BlockSpec((1, TM, D), lambda b, s: (b, s, 0)),
                   pl.BlockSpec((TM * SUBLANES, LANES), lambda b, s: (b * nt + s, 0)),
                   pl.BlockSpec((1, TM, TOP_K), lambda b, s: (b, s, 0)),
                   pl.BlockSpec((1, TM, TOP_K), lambda b, s: (b, s, 0))],
        out_shape=[jax.ShapeDtypeStruct((B, S, D), F32),
                   jax.ShapeDtypeStruct((B * S * SUBLANES, LANES), F32),
                   jax.ShapeDtypeStruct((B, S, TOP_K), jnp.int32),
                   jax.ShapeDtypeStruct((B, S, TOP_K), F32)],
        compiler_params=_cparams(("arbitrary", "arbitrary")),
        name="outproj",
    )(hm, ha, x, mod3, g2, wo, wr_hi, wr_lo, br)


def _row_copy(src_hbm, dst_vmem, src_row, dst_row, sem):
    def tile_start(row):
        start = row * SUBLANES
        return start if isinstance(start, int) else pl.multiple_of(start, SUBLANES)

    return pltpu.make_async_copy(
        src_hbm.at[pl.ds(tile_start(src_row), SUBLANES), :],
        dst_vmem.at[pl.ds(tile_start(dst_row), SUBLANES), :],
        sem)


def _gather_kernel(nused_ref, tok_ref, src_hbm, out_ref, sem, *, rows):
    i = pl.program_id(0)

    @pl.when(i < nused_ref[0])
    def _():
        def issue(r, carry):
            _row_copy(src_hbm, out_ref, tok_ref[0, 0, r], r, sem).start()
            return carry

        lax.fori_loop(0, rows, issue, 0)

        def drain(r, carry):
            _row_copy(src_hbm, out_ref, 0, r, sem).wait()
            return carry

        lax.fori_loop(0, rows, drain, 0)

    @pl.when(i >= nused_ref[0])
    def _():
        out_ref[...] = jnp.zeros_like(out_ref)


def _gather_rows(nused, row_tok, src, n_blocks, rows):
    return pl.pallas_call(
        functools.partial(_gather_kernel, rows=rows),
        grid_spec=pltpu.PrefetchScalarGridSpec(
            num_scalar_prefetch=1,
            grid=(n_blocks,),
            in_specs=[pl.BlockSpec((1, 1, rows), lambda i, n: (i, 0, 0), memory_space=pltpu.SMEM),
                      pl.BlockSpec(memory_space=pl.ANY)],
            out_specs=pl.BlockSpec((rows * SUBLANES, LANES), lambda i, n: (i, 0)),
            scratch_shapes=[pltpu.SemaphoreType.DMA(())]),
        out_shape=jax.ShapeDtypeStruct((n_blocks * rows * SUBLANES, LANES), F32),
        compiler_params=_cparams(("arbitrary",)),
        name="gather",
    )(nused, row_tok.reshape(n_blocks, 1, rows), src)


def _expert_kernel(be_ref, nused_ref, xs_ref, wg_ref, wu_ref, bg_ref, bu_ref, wd_ref, bd_ref, ys_ref):
    i = pl.program_id(0)

    @pl.when(i < nused_ref[0])
    def _():
        x = jnp.concatenate([xs_ref[pl.ds(s, RB, stride=SUBLANES), :] for s in range(SUBLANES)], axis=1).astype(BF16)
        gate = jnp.minimum(_dot(x, wg_ref[0]) + bg_ref[0], SWIGLU_LIMIT)
        up = jnp.clip(_dot(x, wu_ref[0]) + bu_ref[0], -SWIGLU_LIMIT, SWIGLU_LIMIT)
        act = (up + 1.0) * (gate * jax.nn.sigmoid(SWIGLU_ALPHA * gate))
        out = _dot(act.astype(BF16), wd_ref[0]) + bd_ref[0]
        for s in range(SUBLANES):
            ys_ref[pl.ds(s, RB, stride=SUBLANES), :] = out[:, s * LANES:(s + 1) * LANES]

    @pl.when(i >= nused_ref[0])
    def _():
        ys_ref[...] = jnp.zeros_like(ys_ref)


def _experts(block_expert, nused, xs, wg, wu, bg, bu, wd, bd, n_blocks):
    D = wg.shape[1]
    F = wg.shape[2]
    wmap = lambda i, be, n: (be[i], 0, 0)
    return pl.pallas_call(
        _expert_kernel,
        grid_spec=pltpu.PrefetchScalarGridSpec(
            num_scalar_prefetch=2,
            grid=(n_blocks,),
            in_specs=[pl.BlockSpec((RB * SUBLANES, LANES), lambda i, be, n: (i, 0)),
                      pl.BlockSpec((1, D, F), wmap),
                      pl.BlockSpec((1, D, F), wmap),
                      pl.BlockSpec((1, 1, F), wmap),
                      pl.BlockSpec((1, 1, F), wmap),
                      pl.BlockSpec((1, F, D), wmap),
                      pl.BlockSpec((1, 1, D), wmap)],
            out_specs=pl.BlockSpec((RB * SUBLANES, LANES), lambda i, be, n: (i, 0))),
        out_shape=jax.ShapeDtypeStruct((n_blocks * RB * SUBLANES, LANES), F32),
        compiler_params=_cparams(("arbitrary",)),
        name="experts",
    )(block_expert, nused, xs, wg, wu, bg, bu, wd, bd)


def _combine_kernel(pos_ref, ys_hbm, x1_ref, mod_ref, gate_ref, o_ref, buf, sem):
    n = TOP_K * TC

    def issue(j, carry):
        _row_copy(ys_hbm, buf, pos_ref[0, 0, j], j, sem).start()
        return carry

    lax.fori_loop(0, n, issue, 0)

    def drain(j, carry):
        _row_copy(ys_hbm, buf, 0, j, sem).wait()
        return carry

    lax.fori_loop(0, n, drain, 0)

    gates = gate_ref[0]
    y = None
    for k in range(TOP_K):
        rows = jnp.concatenate(
            [buf[pl.ds(k * TC * SUBLANES + s, TC, stride=SUBLANES), :] for s in range(SUBLANES)], axis=1)
        term = gates[:, k:k + 1] * rows
        y = term if y is None else y + term
    gate2 = mod_ref[0, 5:6, :]
    o_ref[0] = x1_ref[0] + gate2 * y


def _combine(pos_tiles, ys, x1, mod3, gates):
    B, S, D = x1.shape
    nt = S // TC
    return pl.pallas_call(
        _combine_kernel,
        grid=(B, nt),
        in_specs=[pl.BlockSpec((1, 1, TOP_K * TC), lambda b, s: (b * nt + s, 0, 0), memory_space=pltpu.SMEM),
                  pl.BlockSpec(memory_space=pl.ANY),
                  pl.BlockSpec((1, TC, D), lambda b, s: (b, s, 0)),
                  pl.BlockSpec((1, 6, D), lambda b, s: (b, 0, 0)),
                  pl.BlockSpec((1, TC, TOP_K), lambda b, s: (b, s, 0))],
        out_specs=pl.BlockSpec((1, TC, D), lambda b, s: (b, s, 0)),
        out_shape=jax.ShapeDtypeStruct((B, S, D), F32),
        scratch_shapes=[pltpu.VMEM((TOP_K * TC * SUBLANES, LANES), F32),
                        pltpu.SemaphoreType.DMA(())],
        compiler_params=_cparams(("arbitrary", "arbitrary")),
        name="combine",
    )(pos_tiles, ys, x1, mod3, gates)


def _routing_tables(top_idx, n_tokens):
    n_slots = n_tokens * TOP_K
    e_flat = top_idx.reshape(-1)
    order = jnp.argsort(e_flat, stable=True).astype(jnp.int32)
    e_sorted = e_flat[order]
    sizes = jnp.bincount(e_flat, length=N_EXPERTS).astype(jnp.int32)
    group_start = jnp.cumsum(sizes) - sizes
    padded = ((sizes + RB - 1) // RB) * RB
    padded_end = jnp.cumsum(padded)
    padded_start = padded_end - padded
    dest = (padded_start[e_sorted] + (jnp.arange(n_slots, dtype=jnp.int32) - group_start[e_sorted])).astype(jnp.int32)
    n_rows = ((n_slots + RB - 1) // RB) * RB + N_EXPERTS * RB
    n_blocks = n_rows // RB
    row_tok = jnp.zeros((n_rows,), jnp.int32).at[dest].set(order // TOP_K)
    pos = jnp.zeros((n_slots,), jnp.int32).at[order].set(dest)
    n_used = (padded_end[-1] // RB).astype(jnp.int32)
    blk = jnp.arange(n_blocks, dtype=jnp.int32)
    block_expert = jnp.minimum(jnp.searchsorted(padded_end, blk * RB, side='right'), N_EXPERTS - 1).astype(jnp.int32)
    last_e = block_expert[jnp.maximum(n_used - 1, 0)]
    block_expert = jnp.where(blk < n_used, block_expert, last_e)
    return row_tok, pos, block_expert, n_used.reshape(1), n_blocks


def kernel(x, c, w_ada, b_ada, norm1_g, w_in, conv_w, conv_b, b_igate, b_fgate, mlstm_norm_g, q_norm_g, k_norm_g,
           lambda_q1, lambda_k1, lambda_q2, lambda_k2, diff_norm_g, w_out, norm2_g, w_router, b_router, w_gu, b_gu,
           w_down, b_down):
    B, S, D = x.shape
    N = B * S
    l = 0
    mq, mk, mv, mo = 256, 256, 512, 512
    o_mq, o_mk, o_mv, o_mo = 0, 256, 512, 1024
    o_mi, o_mf = 1536, 1540
    o_aq, o_ak, o_av = 1544, 2056, 2568

    wi = w_in[l]
    w_main = jnp.concatenate([wi[:, o_mq:o_mq + 512], wi[:, o_mv:o_mv + 512], wi[:, o_mo:o_mo + 512],
                              wi[:, o_aq:o_aq + 512], wi[:, o_ak:o_ak + 512], wi[:, o_av:o_av + 512]],
                             axis=1).astype(BF16)
    wg = jnp.pad(wi[:, o_mi:o_mi + 2 * M_HEADS], ((0, 0), (0, GW - 2 * M_HEADS)))
    wg_hi = wg.astype(BF16)
    wg_lo = (wg - wg_hi.astype(F32)).astype(BF16)
    pos_ids = jnp.arange(S, dtype=F32)
    inv_freq = ROPE_THETA ** (-jnp.arange(0, A_HEAD_DIM, 2, dtype=F32) / A_HEAD_DIM)
    ang = pos_ids[:, None] * inv_freq[None, :]
    cos_h, sin_h = jnp.cos(ang), jnp.sin(ang)
    cos_t = jnp.concatenate([cos_h, cos_h, cos_h, cos_h], axis=1)
    sin_t = jnp.concatenate([-sin_h, sin_h, -sin_h, sin_h], axis=1)
    qg = jnp.tile(q_norm_g[l], 512 // A_HEAD_DIM).reshape(1, 512)
    kg = jnp.tile(k_norm_g[l], 512 // A_HEAD_DIM).reshape(1, 512)
    gid = jnp.arange(512) // A_HEAD_DIM
    bd = (gid[:, None] == gid[None, :]).astype(BF16)
    bias_c = jnp.pad(jnp.concatenate([b_igate[l], b_fgate[l]]), (0, GW - 2 * M_HEADS)).reshape(1, GW)
    bias_r = jnp.concatenate([b_igate[l], b_fgate[l]]).reshape(2 * M_HEADS, 1)
    wr = jnp.pad(w_router[l], ((0, 0), (0, LANES - N_EXPERTS)))
    wr_hi = wr.astype(BF16)
    wr_lo = (wr - wr_hi.astype(F32)).astype(BF16)
    br = jnp.pad(b_router[l], (0, LANES - N_EXPERTS), constant_values=-jnp.inf).reshape(1, LANES)
    wgate = w_gu[l][:, :, 0::2].astype(BF16)
    wup = w_gu[l][:, :, 1::2].astype(BF16)
    bgate = b_gu[l][:, None, 0::2]
    bup = b_gu[l][:, None, 1::2]
    wdn = w_down[l].astype(BF16)
    bdn = b_down[l][:, None, :]

    mod3 = _ada(c, w_ada[l], b_ada[l]).reshape(B, 6, D)
    p_all, gates_c = _inproj(x, mod3, norm1_g[l].reshape(1, D), w_main, wg_hi, wg_lo, cos_t, sin_t, qg, kg, bd)
    gates_r = gates_c[:, :, :2 * M_HEADS].reshape(B, S // CHUNK, CHUNK, 2 * M_HEADS).transpose(0, 1, 3, 2)
    hm = _mlstm(p_all, gates_c, gates_r, bias_c, bias_r, conv_w[l], conv_b[l].reshape(1, 512),
                mlstm_norm_g[l].reshape(1, 512))
    ha = _attn(p_all, lambda_q1[l].reshape(1, -1), lambda_k1[l].reshape(1, -1), lambda_q2[l].reshape(1, -1),
               lambda_k2[l].reshape(1, -1), diff_norm_g[l].reshape(1, 512))

    x1, h2_rows, top_idx, gates = _outproj(hm, ha, x, mod3, norm2_g[l].reshape(1, D), w_out[l].astype(BF16),
                                           wr_hi, wr_lo, br)

    row_tok, pos, block_expert, n_used, n_blocks = _routing_tables(top_idx, N)
    xs = _gather_rows(n_used, row_tok, h2_rows, n_blocks, RB)
    ys = _experts(block_expert, n_used, xs, wgate, wup, bgate, bup, wdn, bdn, n_blocks)
    pos_tiles = pos.reshape(N // TC, TC, TOP_K).transpose(0, 2, 1).reshape(N // TC, 1, TOP_K * TC)
    return _combine(pos_tiles, ys, x1, mod3, gates)
```

```python
import functools
import math

import jax
import jax.numpy as jnp
from jax import lax
from jax.experimental import pallas as pl
from jax.experimental.pallas import tpu as pltpu

F32 = jnp.float32
BF16 = jnp.bfloat16
HIGHEST = lax.Precision.HIGHEST

LANES = 128
SUBLANES = 8
VMEM_LIMIT = 48 * 1024 * 1024

CHUNK = 64
M_HEADS = 4
M_QK_DIM = 64
M_V_DIM = 128
CONV_WIDTH = 4
A_HEADS = 4
A_HEAD_DIM = 64
A_V_DIM = 128
ROPE_THETA = 10000.0
N_EXPERTS = 32
TOP_K = 4
SWIGLU_ALPHA = 1.702
SWIGLU_LIMIT = 7.0
EPS = 1e-6
LAMBDA_INIT = 0.8 - 0.6 * math.exp(-0.3 * 0)

TM = 512
GCH = 8
TQ = 512
RB = 512
TC = 512
GW = 128


def _dot(a, b):
    return jnp.dot(a, b, preferred_element_type=F32)


def _cparams(sem):
    return pltpu.CompilerParams(dimension_semantics=sem, vmem_limit_bytes=VMEM_LIMIT)


def _ada_kernel(c_ref, w_ref, b_ref, o_ref):
    c = c_ref[...]
    cond = c * jax.nn.sigmoid(c)
    o_ref[...] = jnp.dot(cond, w_ref[...], preferred_element_type=F32, precision=HIGHEST) + b_ref[...]


def _ada(c, w, b):
    B, D = c.shape
    n = w.shape[1]
    tn = 1024
    return pl.pallas_call(
        _ada_kernel,
        grid=(n // tn,),
        in_specs=[pl.BlockSpec((B, D), lambda j: (0, 0)),
                  pl.BlockSpec((D, tn), lambda j: (0, j)),
                  pl.BlockSpec((1, tn), lambda j: (0, j))],
        out_specs=pl.BlockSpec((B, tn), lambda j: (0, j)),
        out_shape=jax.ShapeDtypeStruct((B, n), F32),
        compiler_params=_cparams(("arbitrary",)),
        name="ada",
    )(c, w, b.reshape(1, n))


def _inproj_kernel(x_ref, mod_ref, g1_ref, w_ref, wgh_ref, wgl_ref, cos_ref, sin_ref, qg_ref, kg_ref,
                   bd_ref, p_ref, gates_ref):
    x = x_ref[0]
    ms = jnp.mean(x * x, axis=-1, keepdims=True)
    shift = mod_ref[0, 0:1, :]
    scale = mod_ref[0, 1:2, :]
    h = (x * lax.rsqrt(ms + EPS) * g1_ref[...]) * (1.0 + scale) + shift
    hb = h.astype(BF16)
    hl = (h - hb.astype(F32)).astype(BF16)
    gates_ref[0] = _dot(hb, wgh_ref[...]) + _dot(hl, wgh_ref[...]) + _dot(hb, wgl_ref[...])

    tm = x.shape[0]
    lane = lax.broadcasted_iota(jnp.int32, (tm, 512), 1)
    first_half = (lane & 63) < 32
    cos = jnp.concatenate([cos_ref[...]] * 4, axis=1)
    sin = jnp.concatenate([sin_ref[...]] * 4, axis=1)
    for sec in range(6):
        acc = _dot(hb, w_ref[:, sec * 512:(sec + 1) * 512])
        if sec in (3, 4):
            g = qg_ref[...] if sec == 3 else kg_ref[...]
            ssq = _dot((acc * acc).astype(BF16), bd_ref[...])
            y = acc * lax.rsqrt(ssq * (1.0 / A_HEAD_DIM) + EPS) * g
            swapped = jnp.where(first_half, pltpu.roll(y, 512 - 32, 1), pltpu.roll(y, 32, 1))
            acc = y * cos + swapped * sin
            if sec == 3:
                acc = acc * (A_HEAD_DIM ** -0.5)
        p_ref[0, :, sec * 512:(sec + 1) * 512] = acc.astype(BF16)


def _inproj(x, mod3, g1, w_main, wg_hi, wg_lo, cos_t, sin_t, qg, kg, bd):
    B, S, D = x.shape
    nw = w_main.shape[1]
    return pl.pallas_call(
        _inproj_kernel,
        grid=(B, S // TM),
        in_specs=[pl.BlockSpec((1, TM, D), lambda b, s: (b, s, 0)),
                  pl.BlockSpec((1, 6, D), lambda b, s: (b, 0, 0)),
                  pl.BlockSpec((1, D), lambda b, s: (0, 0)),
                  pl.BlockSpec((D, nw), lambda b, s: (0, 0)),
                  pl.BlockSpec((D, GW), lambda b, s: (0, 0)),
                  pl.BlockSpec((D, GW), lambda b, s: (0, 0)),
                  pl.BlockSpec((TM, LANES), lambda b, s: (s, 0)),
                  pl.BlockSpec((TM, LANES), lambda b, s: (s, 0)),
                  pl.BlockSpec((1, 512), lambda b, s: (0, 0)),
                  pl.BlockSpec((1, 512), lambda b, s: (0, 0)),
                  pl.BlockSpec((512, 512), lambda b, s: (0, 0))],
        out_specs=[pl.BlockSpec((1, TM, nw), lambda b, s: (b, s, 0)),
                   pl.BlockSpec((1, TM, GW), lambda b, s: (b, s, 0))],
        out_shape=[jax.ShapeDtypeStruct((B, S, nw), BF16),
                   jax.ShapeDtypeStruct((B, S, GW), F32)],
        compiler_params=_cparams(("arbitrary", "arbitrary")),
        name="inproj",
    )(x, mod3, g1, w_main, wg_hi, wg_lo, cos_t, sin_t, qg, kg, bd)


def _log_sigmoid(z):
    return jnp.minimum(z, 0.0) - jnp.log1p(jnp.exp(-jnp.abs(z)))


def _mlstm_kernel(qk_ref, v_ref, o_ref, gc_ref, gr_ref, bc_ref, br_ref, cw_ref, cb_ref, ng_ref,
                  out_ref, ubuf, q_sc, k_sc, gcs, grs, cst, msc):
    g = pl.program_id(1)
    T = GCH * CHUNK
    HW = M_HEADS * M_QK_DIM

    @pl.when(g == 0)
    def _():
        cst[...] = jnp.zeros_like(cst)
        msc[...] = jnp.zeros_like(msc)
        ubuf[0:SUBLANES, :] = jnp.zeros((SUBLANES, 2 * HW), F32)

    @pl.when(g > 0)
    def _():
        ubuf[0:SUBLANES, :] = ubuf[T:T + SUBLANES, :]

    ubuf[SUBLANES:SUBLANES + T, :] = qk_ref[0].astype(F32)
    y = cb_ref[...]
    for j in range(CONV_WIDTH):
        off = SUBLANES - (CONV_WIDTH - 1) + j
        y = y + cw_ref[j:j + 1, :] * ubuf[off:off + T, :]
    qk = y * jax.nn.sigmoid(y)
    q_sc[...] = qk[:, :HW].astype(BF16)
    k_sc[...] = (qk[:, HW:] * (M_QK_DIM ** -0.5)).astype(BF16)

    gc = gc_ref[0] + bc_ref[...]
    lane = lax.broadcasted_iota(jnp.int32, gc.shape, 1)
    gcs[...] = jnp.where(lane < M_HEADS, gc, _log_sigmoid(gc))
    gr = gr_ref[0] + br_ref[...]
    row = lax.broadcasted_iota(jnp.int32, gr.shape, 1)
    grs[...] = jnp.where(row < M_HEADS, gr, _log_sigmoid(gr))

    ti = lax.broadcasted_iota(jnp.int32, (CHUNK, CHUNK), 0)
    si = lax.broadcasted_iota(jnp.int32, (CHUNK, CHUNK), 1)
    causal = si <= ti
    tri = causal.astype(F32)
    tri_t = (ti <= si).astype(F32)
    lane256 = lax.broadcasted_iota(jnp.int32, (CHUNK, HW), 1)
    lane128 = lax.broadcasted_iota(jnp.int32, (CHUNK, LANES), 1)
    ones_blk = jnp.where(lane128 == 0, 1.0, 0.0).astype(BF16)

    def chunk(c, carry):
        r0 = pl.multiple_of(c * CHUNK, CHUNK)
        qc = q_sc[pl.ds(r0, CHUNK), :]
        kc = k_sc[pl.ds(r0, CHUNK), :]
        vc = v_ref[0, pl.ds(r0, CHUNK), :]
        gcc = gcs[pl.ds(r0, CHUNK), :]
        grr = grs[c]
        b_c = jnp.dot(tri, gcc, preferred_element_type=F32, precision=HIGHEST)
        b_r = jnp.dot(grr, tri_t, preferred_element_type=F32, precision=HIGHEST)
        qstack = jnp.concatenate(
            [jnp.where((lane256 // M_QK_DIM) == h, qc, jnp.zeros_like(qc)) for h in range(M_HEADS)], axis=0)
        kstack = jnp.concatenate(
            [jnp.where((lane256 // M_QK_DIM) == h, kc, jnp.zeros_like(kc)) for h in range(M_HEADS)], axis=0)
        s_all = lax.dot_general(qstack, kc, (((1,), (1,)), ((), ())), preferred_element_type=F32)
        i_all = _dot(qstack, cst[...].astype(BF16))
        vws, decays, freshs = [], [], []
        for h in range(M_HEADS):
            rs = slice(h * CHUNK, (h + 1) * CHUNK)
            bcol = b_c[:, M_HEADS + h:M_HEADS + h + 1]
            icol = gcc[:, h:h + 1]
            brow = b_r[M_HEADS + h:M_HEADS + h + 1, :]
            irow = grr[h:h + 1, :]
            b_last = brow[:, CHUNK - 1:CHUNK]
            m_prev = msc[h:h + 1, 0:1]
            dlog = jnp.where(causal, bcol - brow + irow, -jnp.inf)
            m_intra = jnp.max(dlog, axis=-1, keepdims=True)
            m_inter = bcol + m_prev
            m_t = jnp.maximum(m_inter, m_intra)
            d_w = jnp.exp(dlog - m_t)
            inter_w = jnp.exp(m_inter - m_t)
            p =(s_all[rs, :] * d_w).astype(BF16)
            vaug = jnp.concatenate([vc[:, h * M_V_DIM:(h + 1) * M_V_DIM], ones_blk], axis=1)
            r = _dot(p, vaug) + inter_w * i_all[rs, :]
            num = r[:, :M_V_DIM]
            den = r[:, M_V_DIM:M_V_DIM + 1]
            hv = num / jnp.maximum(jnp.abs(den), jnp.exp(-m_t))
            hn = hv * lax.rsqrt(jnp.mean(hv * hv, axis=-1, keepdims=True) + EPS) * ng_ref[:, h * M_V_DIM:(h + 1) * M_V_DIM]
            og = o_ref[0, pl.ds(r0, CHUNK), h * M_V_DIM:(h + 1) * M_V_DIM].astype(F32)
            out_ref[0, pl.ds(r0, CHUNK), h * M_V_DIM:(h + 1) * M_V_DIM] = (hn * jax.nn.sigmoid(og)).astype(BF16)
            a_col = b_last - bcol + icol
            m_loc = jnp.max(a_col, axis=0, keepdims=True)
            w_col = jnp.exp(a_col - m_loc)
            vws.append((vaug.astype(F32) * w_col).astype(BF16))
            m_new = jnp.maximum(b_last + m_prev, m_loc)
            decays.append(jnp.exp(b_last + m_prev - m_new))
            freshs.append(jnp.exp(m_loc - m_new))
            msc[h:h + 1, :] = jnp.broadcast_to(m_new, (1, LANES))
        c_loc = lax.dot_general(kstack, jnp.concatenate(vws, axis=0), (((0,), (0,)), ((), ())),
                                preferred_element_type=F32)
        for h in range(M_HEADS):
            ks = slice(h * M_QK_DIM, (h + 1) * M_QK_DIM)
            cst[ks, :] = decays[h] * cst[ks, :] + freshs[h] * c_loc[ks, :]
        return carry

    lax.fori_loop(0, GCH, chunk, 0)


def _mlstm(p_all, gates_c, gates_r, bias_c, bias_r, conv_w, conv_b, ng):
    B, S, _ = p_all.shape
    T = GCH * CHUNK
    return pl.pallas_call(
        _mlstm_kernel,
        grid=(B, S // T),
        in_specs=[pl.BlockSpec((1, T, 512), lambda b, g: (b, g, 0)),
                  pl.BlockSpec((1, T, 512), lambda b, g: (b, g, 1)),
                  pl.BlockSpec((1, T, 512), lambda b, g: (b, g, 2)),
                  pl.BlockSpec((1, T, GW), lambda b, g: (b, g, 0)),
                  pl.BlockSpec((1, GCH, SUBLANES, CHUNK), lambda b, g: (b, g, 0, 0)),
                  pl.BlockSpec((1, GW), lambda b, g: (0, 0)),
                  pl.BlockSpec((SUBLANES, 1), lambda b, g: (0, 0)),
                  pl.BlockSpec((CONV_WIDTH, 512), lambda b, g: (0, 0)),
                  pl.BlockSpec((1, 512), lambda b, g: (0, 0)),
                  pl.BlockSpec((1, 512), lambda b, g: (0, 0))],
        out_specs=pl.BlockSpec((1, T, 512), lambda b, g: (b, g, 0)),
        out_shape=jax.ShapeDtypeStruct((B, S, 512), BF16),
        scratch_shapes=[pltpu.VMEM((T + SUBLANES, 512), F32),
                        pltpu.VMEM((T, 256), BF16),
                        pltpu.VMEM((T, 256), BF16),
                        pltpu.VMEM((T, GW), F32),
                        pltpu.VMEM((GCH, SUBLANES, CHUNK), F32),
                        pltpu.VMEM((M_HEADS * M_QK_DIM, 2 * M_V_DIM), F32),
                        pltpu.VMEM((SUBLANES, LANES), F32)],
        compiler_params=_cparams(("arbitrary", "arbitrary")),
        name="mlstm",
    )(p_all, p_all, p_all, gates_c, gates_r, bias_c, bias_r, conv_w, conv_b, ng)


def _attn_kernel(lq1_ref, lk1_ref, lq2_ref, lk2_ref, q_ref, k_ref, v_ref, ng_ref, o_ref,
                 qs_sc, m_sc, acc_sc):
    i = pl.program_id(2)
    q = q_ref[0]
    lane = lax.broadcasted_iota(jnp.int32, q.shape, 1)
    qs_sc[0:TQ, :] = jnp.where(lane < A_HEAD_DIM, q, jnp.zeros_like(q))
    qs_sc[TQ:2 * TQ, :] = jnp.where(lane >= A_HEAD_DIM, q, jnp.zeros_like(q))
    m_sc[...] = jnp.full(m_sc.shape, -jnp.inf, F32)
    acc_sc[...] = jnp.zeros_like(acc_sc)
    ones_blk = jnp.where(lax.broadcasted_iota(jnp.int32, (TQ, LANES), 1) == 0, 1.0, 0.0).astype(BF16)

    def scores(j):
        k = k_ref[0, pl.ds(pl.multiple_of(j * TQ, TQ), TQ), :]
        return lax.dot_general(qs_sc[...], k, (((1,), (1,)), ((), ())), preferred_element_type=F32)

    def diag_scores():
        ri = lax.broadcasted_iota(jnp.int32, (2 * TQ, TQ), 0)
        ci = lax.broadcasted_iota(jnp.int32, (2 * TQ, TQ), 1)
        visible = (ci // CHUNK) <= ((ri & (TQ - 1)) // CHUNK)
        return jnp.where(visible, scores(i), -jnp.inf)

    def accumulate(s, j):
        v = v_ref[0, pl.ds(pl.multiple_of(j * TQ, TQ), TQ), :]
        m_old = m_sc[...]
        m_new = jnp.maximum(m_old, jnp.max(s, axis=-1, keepdims=True))
        alpha = jnp.exp(m_old - m_new)
        p = jnp.exp(s - jnp.concatenate([m_new] * (TQ // LANES), axis=1))
        pv = _dot(p.astype(BF16), jnp.concatenate([v, ones_blk], axis=1))
        acc_sc[...] = jnp.concatenate([alpha, alpha], axis=1) * acc_sc[...] + pv
        m_sc[...] = m_new

    @pl.when(i == 0)
    def _():
        accumulate(diag_scores(), i)

    @pl.when(i > 0)
    def _():
        def body(j, s_cur):
            s_next = scores(j + 1)
            accumulate(s_cur, j)
            return s_next

        s_last = lax.fori_loop(0, i - 1, body, scores(0))
        s_diag = diag_scores()
        accumulate(s_last, i - 1)
        accumulate(s_diag, i)

    acc = acc_sc[...]
    o = acc[:, :LANES] / acc[:, LANES:LANES + 1]
    lam = (jnp.exp(jnp.sum(lq1_ref[...] * lk1_ref[...], axis=-1, keepdims=True))
           - jnp.exp(jnp.sum(lq2_ref[...] * lk2_ref[...], axis=-1, keepdims=True)) + LAMBDA_INIT)
    a = o[0:TQ, :] - lam * o[TQ:2 * TQ, :]
    y = a * lax.rsqrt(jnp.mean(a * a, axis=-1, keepdims=True) + EPS) * ng_ref[...]
    o_ref[0] = (y * (1.0 - LAMBDA_INIT)).astype(BF16)


def _attn(p_all, lq1, lk1, lq2, lk2, ng):
    B, S, _ = p_all.shape
    nsec = 512 // LANES
    lam_spec = pl.BlockSpec((1, A_HEAD_DIM), lambda b, h, i: (0, 0))
    return pl.pallas_call(
        _attn_kernel,
        grid=(B, A_HEADS, S // TQ),
        in_specs=[lam_spec, lam_spec, lam_spec, lam_spec,
                  pl.BlockSpec((1, TQ, LANES), lambda b, h, i: (b, i, 3 * nsec + h)),
                  pl.BlockSpec((1, S, LANES), lambda b, h, i: (b, 0, 4 * nsec + h)),
                  pl.BlockSpec((1, S, LANES), lambda b, h, i: (b, 0, 5 * nsec + h)),
                  pl.BlockSpec((1, LANES), lambda b, h, i: (0, h))],
        out_specs=pl.BlockSpec((1, TQ, LANES), lambda b, h, i: (b, i, h)),
        out_shape=jax.ShapeDtypeStruct((B, S, 512), BF16),
        scratch_shapes=[pltpu.VMEM((2 * TQ, LANES), BF16),
                        pltpu.VMEM((2 * TQ, LANES), F32),
                        pltpu.VMEM((2 * TQ, 2 * LANES), F32)],
        compiler_params=_cparams(("arbitrary", "arbitrary", "arbitrary")),
        name="attn",
    )(lq1, lk1, lq2, lk2, p_all, p_all, p_all, ng)


def _outproj_kernel(hm_ref, ha_ref, x_ref, mod_ref, g2_ref, wo_ref, wrh_ref, wrl_ref, br_ref, ltri_ref,
                    x1_ref, h2_ref, idx_ref, gate_ref, rank_ref, cnt_ref, cnt_sc):
    @pl.when((pl.program_id(0) == 0) & (pl.program_id(1) == 0))
    def _():
        cnt_sc[...] = jnp.zeros_like(cnt_sc)

    hcat = jnp.concatenate([hm_ref[0], ha_ref[0]], axis=1)
    mix = _dot(hcat, wo_ref[...])
    gate1 = mod_ref[0, 2:3, :]
    shift2 = mod_ref[0, 3:4, :]
    scale2 = mod_ref[0, 4:5, :]
    x1 = x_ref[0] + gate1 * mix
    x1_ref[0] = x1
    ms = jnp.mean(x1 * x1, axis=-1, keepdims=True)
    h2 = (x1 * lax.rsqrt(ms + EPS) * g2_ref[...]) * (1.0 + scale2) + shift2
    tm = h2.shape[0]
    for s in range(SUBLANES):
        h2_ref[pl.ds(s, tm, stride=SUBLANES), :] = h2[:, s * LANES:(s + 1) * LANES]
    hb = h2.astype(BF16)
    hl = (h2 - hb.astype(F32)).astype(BF16)
    logits = _dot(hb, wrh_ref[...]) + _dot(hl, wrh_ref[...]) + _dot(hb, wrl_ref[...]) + br_ref[...]
    lane = lax.broadcasted_iota(jnp.int32, logits.shape, 1).astype(F32)
    vals, idxs = [], []
    work = logits
    for _ in range(TOP_K):
        mx = jnp.max(work, axis=-1, keepdims=True)
        ix = jnp.min(jnp.where(work == mx, lane, float(LANES)), axis=-1, keepdims=True)
        vals.append(mx)
        idxs.append(ix)
        work = jnp.where(lane == ix, -jnp.inf, work)
    es = [jnp.exp(v - vals[0]) for v in vals]
    tot = es[0] + es[1] + es[2] + es[3]
    gsel = jnp.zeros_like(logits)
    isel = jnp.zeros_like(logits)
    for k in range(TOP_K):
        gsel = jnp.where(lane == float(k), es[k] / tot, gsel)
        isel = jnp.where(lane == float(k), idxs[k], isel)
    gate_ref[0] = gsel[:, :TOP_K]
    idx_ref[0] = isel[:, :TOP_K].astype(jnp.int32)
    chosen = [lane == ix for ix in idxs]
    multi = jnp.zeros_like(logits)
    for ch in chosen:
        multi = jnp.where(ch, 1.0, multi)
    before = _dot(ltri_ref[...], multi.astype(BF16)) + cnt_sc[...]
    rsel = jnp.zeros_like(logits)
    for k in range(TOP_K):
        rk = jnp.sum(jnp.where(chosen[k], before, 0.0), axis=-1, keepdims=True)
        rsel = jnp.where(lane == float(k), rk, rsel)
    rank_ref[0] = rsel[:, :TOP_K].astype(jnp.int32)
    cnt_sc[...] = cnt_sc[...] + jnp.sum(multi, axis=0, keepdims=True)
    cnt_ref[...] = cnt_sc[...]


def _outproj(hm, ha, x, mod3, g2, wo, wr_hi, wr_lo, br, ltri):
    B, S, D = x.shape
    nt = S // TM
    return pl.pallas_call(
        _outproj_kernel,
        grid=(B, nt),
        in_specs=[pl.BlockSpec((1, TM, 512), lambda b, s: (b, s, 0)),
                  pl.BlockSpec((1, TM, 512), lambda b, s: (b, s, 0)),
                  pl.BlockSpec((1, TM, D), lambda b, s: (b, s, 0)),
                  pl.BlockSpec((1, 6, D), lambda b, s: (b, 0, 0)),
                  pl.BlockSpec((1, D), lambda b, s: (0, 0)),
                  pl.BlockSpec((D, D), lambda b, s: (0, 0)),
                  pl.BlockSpec((D, LANES), lambda b, s: (0, 0)),
                  pl.BlockSpec((D, LANES), lambda b, s: (0, 0)),
                  pl.BlockSpec((1, LANES), lambda b, s: (0, 0)),
                  pl.BlockSpec((TM, TM), lambda b, s: (0, 0))],
        out_specs=[pl.BlockSpec((1, TM, D), lambda b, s: (b, s, 0)),
                   pl.BlockSpec((TM * SUBLANES, LANES), lambda b, s: (b * nt + s, 0)),
                   pl.BlockSpec((1, TM, TOP_K), lambda b, s: (b, s, 0)),
                   pl.BlockSpec((1, TM, TOP_K), lambda b, s: (b, s, 0)),
                   pl.BlockSpec((1, TM, TOP_K), lambda b, s: (b, s, 0)),
                   pl.BlockSpec((1, LANES), lambda b, s: (0, 0))],
        out_shape=[jax.ShapeDtypeStruct((B, S, D), F32),
                   jax.ShapeDtypeStruct((B * S * SUBLANES, LANES), F32),
                   jax.ShapeDtypeStruct((B, S, TOP_K), jnp.int32),
                   jax.ShapeDtypeStruct((B, S, TOP_K), F32),
                   jax.ShapeDtypeStruct((B, S, TOP_K), jnp.int32),
                   jax.ShapeDtypeStruct((1, LANES), F32)],
        scratch_shapes=[pltpu.VMEM((1, LANES), F32)],
        compiler_params=_cparams(("arbitrary", "arbitrary")),
        name="outproj",
    )(hm, ha, x, mod3, g2, wo, wr_hi, wr_lo, br, ltri)


DMA_UNROLL = 8


def _row_copy(src, dst, src_row, dst_row, sem):
    def tile_start(row):
        start = row * SUBLANES
        return start if isinstance(start, int) else pl.multiple_of(start, SUBLANES)

    return pltpu.make_async_copy(
        src.at[pl.ds(tile_start(src_row), SUBLANES), :],
        dst.at[pl.ds(tile_start(dst_row), SUBLANES), :],
        sem)


def _wait_rows(hbm, n_rows, sem):
    span = hbm.at[pl.ds(0, n_rows * SUBLANES), :]
    pltpu.make_async_copy(span, span, sem).wait()


def _dispatch_kernel(pad_ref, pos_ref, h2_ref, xs_hbm, zero_sc, sem, sem_pad, *, n_pad):
    i = pl.program_id(0)
    n = TOP_K * TM

    @pl.when(i == 0)
    def _():
        zero_sc[...] = jnp.zeros_like(zero_sc)

        def issue_pad(g, carry):
            for u in range(DMA_UNROLL):
                _row_copy(zero_sc, xs_hbm, 0, pad_ref[g * DMA_UNROLL + u], sem_pad).start(priority=u % 2)
            return carry

        lax.fori_loop(0, n_pad // DMA_UNROLL, issue_pad, 0)

    def issue(g, carry):
        for u in range(DMA_UNROLL):
            j = g * DMA_UNROLL + u
            tok = g * (DMA_UNROLL // TOP_K) + u // TOP_K
            _row_copy(h2_ref, xs_hbm, tok, pos_ref[0, 0, j], sem).start(priority=u % 2)
        return carry

    lax.fori_loop(0, n // DMA_UNROLL, issue, 0)
    _wait_rows(xs_hbm, n, sem)

    @pl.when(i == 0)
    def _():
        _wait_rows(xs_hbm, n_pad, sem_pad)


def _dispatch(pad_rows, pos_tiles, h2_rows, n_rows):
    n_tiles = pos_tiles.shape[0]
    n_pad = pad_rows.shape[0]
    return pl.pallas_call(
        functools.partial(_dispatch_kernel, n_pad=n_pad),
        grid_spec=pltpu.PrefetchScalarGridSpec(
            num_scalar_prefetch=1,
            grid=(n_tiles,),
            in_specs=[pl.BlockSpec((1, 1, TOP_K * TM), lambda i, p: (i, 0, 0), memory_space=pltpu.SMEM),
                      pl.BlockSpec((TM * SUBLANES, LANES), lambda i, p: (i, 0))],
            out_specs=pl.BlockSpec(memory_space=pl.ANY),
            scratch_shapes=[pltpu.VMEM((SUBLANES, LANES), F32),
                            pltpu.SemaphoreType.DMA(()),
                            pltpu.SemaphoreType.DMA(())]),
        out_shape=jax.ShapeDtypeStruct((n_rows * SUBLANES, LANES), F32),
        compiler_params=_cparams(("arbitrary",)),
        name="dispatch",
    )(pad_rows, pos_tiles, h2_rows)


def _wprep_kernel(w_ref, p_ref, o_ref):
    half = o_ref.shape[2] // 2
    for cblk in range(o_ref.shape[2] // 256):
        blk = _dot(w_ref[0, :, cblk * 256:(cblk + 1) * 256].astype(BF16), p_ref[...])
        o_ref[0, :, cblk * LANES:(cblk + 1) * LANES] = blk[:, :LANES].astype(BF16)
        o_ref[0, :, half + cblk * LANES:half + (cblk + 1) * LANES] = blk[:, LANES:].astype(BF16)


def _wprep(w_gu, psel):
    E, D, F2 = w_gu.shape
    return pl.pallas_call(
        _wprep_kernel,
        grid=(E,),
        in_specs=[pl.BlockSpec((1, D, F2), lambda e: (e, 0, 0)),
                  pl.BlockSpec((256, 256), lambda e: (0, 0))],
        out_specs=pl.BlockSpec((1, D, F2), lambda e: (e, 0, 0)),
        out_shape=jax.ShapeDtypeStruct((E, D, F2), BF16),
        compiler_params=_cparams(("arbitrary",)),
        name="wprep",
    )(w_gu, psel)


def _expert_kernel(be_ref, nused_ref, xs_ref, wgu_ref, bg_ref, bu_ref, wd_ref, bd_ref, ys_ref):
    i = pl.program_id(0)

    @pl.when(i < nused_ref[0])
    def _():
        x = jnp.concatenate([xs_ref[pl.ds(s, RB, stride=SUBLANES), :] for s in range(SUBLANES)], axis=1).astype(BF16)
        gu = _dot(x, wgu_ref[0])
        half = gu.shape[1] // 2
        gate = jnp.minimum(gu[:, :half] + bg_ref[0], SWIGLU_LIMIT)
        up = jnp.clip(gu[:, half:] + bu_ref[0], -SWIGLU_LIMIT, SWIGLU_LIMIT)
        act = (up + 1.0) * (gate * jax.nn.sigmoid(SWIGLU_ALPHA * gate))
        out = _dot(act.astype(BF16), wd_ref[0]) + bd_ref[0]
        for s in range(SUBLANES):
            ys_ref[pl.ds(s, RB, stride=SUBLANES), :] = out[:, s * LANES:(s + 1) * LANES]

    @pl.when(i >= nused_ref[0])
    def _():
        ys_ref[...] = jnp.zeros_like(ys_ref)


def _experts(block_expert, nused, xs, wgu, bg, bu, wd, bd, n_blocks):
    D = wgu.shape[1]
    F = wgu.shape[2] // 2
    wmap = lambda i, be, n: (be[i], 0, 0)
    return pl.pallas_call(
        _expert_kernel,
        grid_spec=pltpu.PrefetchScalarGridSpec(
            num_scalar_prefetch=2,
            grid=(n_blocks,),
            in_specs=[pl.BlockSpec((RB * SUBLANES, LANES), lambda i, be, n: (i, 0)),
                      pl.BlockSpec((1, D, 2 * F), wmap),
                      pl.BlockSpec((1, 1, F), wmap),
                      pl.BlockSpec((1, 1, F), wmap),
                      pl.BlockSpec((1, F, D), wmap),
                      pl.BlockSpec((1, 1, D), wmap)],
            out_specs=pl.BlockSpec((RB * SUBLANES, LANES), lambda i, be, n: (i, 0))),
        out_shape=jax.ShapeDtypeStruct((n_blocks * RB * SUBLANES, LANES), F32),
        compiler_params=_cparams(("arbitrary",)),
        name="experts",
    )(block_expert, nused, xs, wgu, bg, bu, wd, bd)


def _combine_kernel(pos_ref, ys_hbm, x1_ref, mod_ref, gate_ref, o_ref, buf, sem):
    n = TOP_K * TC

    def issue(g, carry):
        for u in range(DMA_UNROLL):
            j = g * DMA_UNROLL + u
            _row_copy(ys_hbm, buf, pos_ref[0, 0, j], j, sem).start(priority=u % 2)
        return carry

    lax.fori_loop(0, n // DMA_UNROLL, issue, 0)
    _wait_rows(ys_hbm, n, sem)

    gates = gate_ref[0]
    y = None
    for k in range(TOP_K):
        rows = jnp.concatenate(
            [buf[pl.ds(k * TC * SUBLANES + s, TC, stride=SUBLANES), :] for s in range(SUBLANES)], axis=1)
        term = gates[:, k:k + 1] * rows
        y = term if y is None else y + term
    gate2 = mod_ref[0, 5:6, :]
    o_ref[0] = x1_ref[0] + gate2 * y


def _combine(pos_tiles, ys, x1, mod3, gates):
    B, S, D = x1.shape
    nt = S // TC
    return pl.pallas_call(
        _combine_kernel,
        grid=(B, nt),
        in_specs=[pl.BlockSpec((1, 1, TOP_K * TC), lambda b, s: (b * nt + s, 0, 0), memory_space=pltpu.SMEM),
                  pl.BlockSpec(memory_space=pl.ANY),
                  pl.BlockSpec((1, TC, D), lambda b, s: (b, s, 0)),
                  pl.BlockSpec((1, 6, D), lambda b, s: (b, 0, 0)),
                  pl.BlockSpec((1, TC, TOP_K), lambda b, s: (b, s, 0))],
        out_specs=pl.BlockSpec((1, TC, D), lambda b, s: (b, s, 0)),
        out_shape=jax.ShapeDtypeStruct((B, S, D), F32),
        scratch_shapes=[pltpu.VMEM((TOP_K * TC * SUBLANES, LANES), F32),
                        pltpu.SemaphoreType.DMA(())],
        compiler_params=_cparams(("arbitrary", "arbitrary")),
        name="combine",
    )(pos_tiles, ys, x1, mod3, gates)


def _lookup(table, idx):
    hit = idx[..., None] == jnp.arange(table.shape[0], dtype=jnp.int32)
    return jnp.sum(jnp.where(hit, table, 0), axis=-1).astype(jnp.int32)


def _routing_tables(top_idx, rank, counts, n_tokens):
    n_slots = n_tokens * TOP_K
    n_rows = n_slots + N_EXPERTS * RB
    n_blocks = n_rows // RB
    sizes = counts[0, :N_EXPERTS].astype(jnp.int32)
    padded = ((sizes + RB - 1) // RB) * RB
    padded_end = jnp.cumsum(padded).astype(jnp.int32)
    padded_start = padded_end - padded
    pos = _lookup(padded_start, top_idx) + rank
    n_used = padded_end[-1] // RB
    blk = jnp.arange(n_blocks, dtype=jnp.int32)
    block_expert = jnp.minimum(jnp.sum(padded_end[None, :] <= (blk * RB)[:, None], axis=1), N_EXPERTS - 1)
    last_e = jnp.minimum(jnp.sum(padded_end <= (n_used - 1) * RB), N_EXPERTS - 1)
    block_expert = jnp.where(blk < n_used, block_expert, last_e).astype(jnp.int32)
    pad_cnt = padded - sizes
    pad_end = jnp.cumsum(pad_cnt).astype(jnp.int32)
    j = jnp.arange(N_EXPERTS * RB, dtype=jnp.int32)
    e_j = jnp.sum(pad_end[None, :] <= j[:, None], axis=1).astype(jnp.int32)
    in_group = e_j < N_EXPERTS
    e_c = jnp.minimum(e_j, N_EXPERTS - 1)
    group_row = _lookup(padded_start + sizes - (pad_end - pad_cnt), e_c) + j
    tail_row = padded_end[-1] + (j - pad_end[-1])
    pad_rows = jnp.where(in_group, group_row, tail_row).astype(jnp.int32)
    return pos, pad_rows, block_expert, n_used.reshape(1).astype(jnp.int32), n_rows, n_blocks


def kernel(x, c, w_ada, b_ada, norm1_g, w_in, conv_w, conv_b, b_igate, b_fgate, mlstm_norm_g, q_norm_g, k_norm_g,
           lambda_q1, lambda_k1, lambda_q2, lambda_k2, diff_norm_g, w_out, norm2_g, w_router, b_router, w_gu, b_gu,
           w_down, b_down):
    B, S, D = x.shape
    N = B * S
    l = 0
    mq, mk, mv, mo = 256, 256, 512, 512
    o_mq, o_mk, o_mv, o_mo = 0, 256, 512, 1024
    o_mi, o_mf = 1536, 1540
    o_aq, o_ak, o_av = 1544, 2056, 2568

    wi = w_in[l]
    w_main = jnp.concatenate([wi[:, o_mq:o_mq + 512], wi[:, o_mv:o_mv + 512], wi[:, o_mo:o_mo + 512],
                              wi[:, o_aq:o_aq + 512], wi[:, o_ak:o_ak + 512], wi[:, o_av:o_av + 512]],
                             axis=1).astype(BF16)
    wg = jnp.pad(wi[:, o_mi:o_mi + 2 * M_HEADS], ((0, 0), (0, GW - 2 * M_HEADS)))
    wg_hi = wg.astype(BF16)
    wg_lo = (wg - wg_hi.astype(F32)).astype(BF16)
    pos_ids = jnp.arange(S, dtype=F32)
    inv_freq = ROPE_THETA ** (-jnp.arange(0, A_HEAD_DIM, 2, dtype=F32) / A_HEAD_DIM)
    ang = pos_ids[:, None] * inv_freq[None, :]
    cos_h, sin_h = jnp.cos(ang), jnp.sin(ang)
    cos_t = jnp.concatenate([cos_h, cos_h, cos_h, cos_h], axis=1)
    sin_t = jnp.concatenate([-sin_h, sin_h, -sin_h, sin_h], axis=1)
    qg = jnp.tile(q_norm_g[l], 512 // A_HEAD_DIM).reshape(1, 512)
    kg = jnp.tile(k_norm_g[l], 512 // A_HEAD_DIM).reshape(1, 512)
    gid = jnp.arange(512) // A_HEAD_DIM
    bd = (gid[:, None] == gid[None, :]).astype(BF16)
    bias_c = jnp.pad(jnp.concatenate([b_igate[l], b_fgate[l]]), (0, GW - 2 * M_HEADS)).reshape(1, GW)
    bias_r = jnp.concatenate([b_igate[l], b_fgate[l]]).reshape(2 * M_HEADS, 1)
    wr = jnp.pad(w_router[l], ((0, 0), (0, LANES - N_EXPERTS)))
    wr_hi = wr.astype(BF16)
    wr_lo = (wr - wr_hi.astype(F32)).astype(BF16)
    br = jnp.pad(b_router[l], (0, LANES - N_EXPERTS), constant_values=-jnp.inf).reshape(1, LANES)
    col = jnp.arange(256)
    psel = (jnp.where(col % 2 == 0, col // 2, LANES + col // 2)[:, None] == col[None, :]).astype(BF16)
    ltri = (jnp.arange(TM)[None, :] < jnp.arange(TM)[:, None]).astype(BF16)
    bgate = b_gu[l][:, None, 0::2]
    bup = b_gu[l][:, None, 1::2]
    wdn = w_down[l].astype(BF16)
    bdn = b_down[l][:, None, :]

    mod3 = _ada(c, w_ada[l], b_ada[l]).reshape(B, 6, D)
    p_all, gates_c = _inproj(x, mod3, norm1_g[l].reshape(1, D), w_main, wg_hi, wg_lo, cos_t, sin_t, qg, kg, bd)
    gates_r = gates_c[:, :, :2 * M_HEADS].reshape(B, S // CHUNK, CHUNK, 2 * M_HEADS).transpose(0, 1, 3, 2)
    hm = _mlstm(p_all, gates_c, gates_r, bias_c, bias_r, conv_w[l], conv_b[l].reshape(1, 512),
                mlstm_norm_g[l].reshape(1, 512))
    ha = _attn(p_all, lambda_q1[l].reshape(1, -1), lambda_k1[l].reshape(1, -1), lambda_q2[l].reshape(1, -1),
               lambda_k2[l].reshape(1, -1), diff_norm_g[l].reshape(1, 512))

    x1, h2_rows, top_idx, gates, rank, counts = _outproj(hm, ha, x, mod3, norm2_g[l].reshape(1, D),
                                                         w_out[l].astype(BF16), wr_hi, wr_lo, br, ltri)

    pos, pad_rows, block_expert, n_used, n_rows, n_blocks = _routing_tables(top_idx, rank, counts, N)
    xs = _dispatch(pad_rows, pos.reshape(N // TM, 1, TM * TOP_K), h2_rows, n_rows)
    wgu = _wprep(w_gu[l], psel)
    ys = _experts(block_expert, n_used, xs, wgu, bgate, bup, wdn, bdn, n_blocks)
    pos_tiles = pos.reshape(N // TC, TC, TOP_K).transpose(0, 2, 1).reshape(N // TC, 1, TOP_K * TC)
    return _combine(pos_tiles, ys, x1, mod3, gates)
```

```python
import functools
import math

import jax
import jax.numpy as jnp
from jax import lax
from jax.experimental import pallas as pl
from jax.experimental.pallas import tpu as pltpu

F32 = jnp.float32
BF16 = jnp.bfloat16
HIGHEST = lax.Precision.HIGHEST

LANES = 128
SUBLANES = 8
VMEM_LIMIT = 48 * 1024 * 1024

CHUNK = 64
M_HEADS = 4
M_QK_DIM = 64
M_V_DIM = 128
CONV_WIDTH = 4
A_HEADS = 4
A_HEAD_DIM = 64
A_V_DIM = 128
ROPE_THETA = 10000.0
N_EXPERTS = 32
TOP_K = 4
SWIGLU_ALPHA = 1.702
SWIGLU_LIMIT = 7.0
EPS = 1e-6
LAMBDA_INIT = 0.8 - 0.6 * math.exp(-0.3 * 0)

TM = 512
GCH = 8
TQ = 512
RB = 512
TC = 512
GW = 128


def _dot(a, b):
    return jnp.dot(a, b, preferred_element_type=F32)


def _cparams(sem):
    return pltpu.CompilerParams(dimension_semantics=sem, vmem_limit_bytes=VMEM_LIMIT)


def _ada_kernel(c_ref, w_ref, b_ref, o_ref):
    c = c_ref[...]
    cond = c * jax.nn.sigmoid(c)
    o_ref[...] = jnp.dot(cond, w_ref[...], preferred_element_type=F32, precision=HIGHEST) + b_ref[...]


def _ada(c, w, b):
    B, D = c.shape
    n = w.shape[1]
    tn = 1024
    return pl.pallas_call(
        _ada_kernel,
        grid=(n // tn,),
        in_specs=[pl.BlockSpec((B, D), lambda j: (0, 0)),
                  pl.BlockSpec((D, tn), lambda j: (0, j)),
                  pl.BlockSpec((1, tn), lambda j: (0, j))],
        out_specs=pl.BlockSpec((B, tn), lambda j: (0, j)),
        out_shape=jax.ShapeDtypeStruct((B, n), F32),
        compiler_params=_cparams(("arbitrary",)),
        name="ada",
    )(c, w, b.reshape(1, n))


def _inproj_kernel(x_ref, mod_ref, g1_ref, w_ref, wgh_ref, wgl_ref, cos_ref, sin_ref, qg_ref, kg_ref,
                   bd_ref, p_ref, gates_ref):
    x = x_ref[0]
    ms = jnp.mean(x * x, axis=-1, keepdims=True)
    shift = mod_ref[0, 0:1, :]
    scale = mod_ref[0, 1:2, :]
    h = (x * lax.rsqrt(ms + EPS) * g1_ref[...]) * (1.0 + scale) + shift
    hb = h.astype(BF16)
    hl = (h - hb.astype(F32)).astype(BF16)
    gates_ref[0] = _dot(hb, wgh_ref[...]) + _dot(hl, wgh_ref[...]) + _dot(hb, wgl_ref[...])

    tm = x.shape[0]
    lane = lax.broadcasted_iota(jnp.int32, (tm, 512), 1)
    first_half = (lane & 63) < 32
    cos = jnp.concatenate([cos_ref[...]] * 4, axis=1)
    sin = jnp.concatenate([sin_ref[...]] * 4, axis=1)
    for sec in range(6):
        acc = _dot(hb, w_ref[:, sec * 512:(sec + 1) * 512])
        if sec in (3, 4):
            g = qg_ref[...] if sec == 3 else kg_ref[...]
            ssq = _dot((acc * acc).astype(BF16), bd_ref[...])
            y = acc * lax.rsqrt(ssq * (1.0 / A_HEAD_DIM) + EPS) * g
            swapped = jnp.where(first_half, pltpu.roll(y, 512 - 32, 1), pltpu.roll(y, 32, 1))
            acc = y * cos + swapped * sin
            if sec == 3:
                acc = acc * (A_HEAD_DIM ** -0.5)
        p_ref[0, :, sec * 512:(sec + 1) * 512] = acc.astype(BF16)


def _inproj(x, mod3, g1, w_main, wg_hi, wg_lo, cos_t, sin_t, qg, kg, bd):
    B, S, D = x.shape
    nw = w_main.shape[1]
    return pl.pallas_call(
        _inproj_kernel,
        grid=(B, S // TM),
        in_specs=[pl.BlockSpec((1, TM, D), lambda b, s: (b, s, 0)),
                  pl.BlockSpec((1, 6, D), lambda b, s: (b, 0, 0)),
                  pl.BlockSpec((1, D), lambda b, s: (0, 0)),
                  pl.BlockSpec((D, nw), lambda b, s: (0, 0)),
                  pl.BlockSpec((D, GW), lambda b, s: (0, 0)),
                  pl.BlockSpec((D, GW), lambda b, s: (0, 0)),
                  pl.BlockSpec((TM, LANES), lambda b, s: (s, 0)),
                  pl.BlockSpec((TM, LANES), lambda b, s: (s, 0)),
                  pl.BlockSpec((1, 512), lambda b, s: (0, 0)),
                  pl.BlockSpec((1, 512), lambda b, s: (0, 0)),
                  pl.BlockSpec((512, 512), lambda b, s: (0, 0))],
        out_specs=[pl.BlockSpec((1, TM, nw), lambda b, s: (b, s, 0)),
                   pl.BlockSpec((1, TM, GW), lambda b, s: (b, s, 0))],
        out_shape=[jax.ShapeDtypeStruct((B, S, nw), BF16),
                   jax.ShapeDtypeStruct((B, S, GW), F32)],
        compiler_params=_cparams(("arbitrary", "arbitrary")),
        name="inproj",
    )(x, mod3, g1, w_main, wg_hi, wg_lo, cos_t, sin_t, qg, kg, bd)


def _log_sigmoid(z):
    return jnp.minimum(z, 0.0) - jnp.log1p(jnp.exp(-jnp.abs(z)))


def _mlstm_kernel(qk_ref, v_ref, o_ref, gc_ref, gr_ref, bc_ref, br_ref, cw_ref, cb_ref, ng_ref,
                  out_ref, ubuf, q_sc, k_sc, gcs, grs, cst, msc):
    g = pl.program_id(1)
    T = GCH * CHUNK
    HW = M_HEADS * M_QK_DIM

    @pl.when(g == 0)
    def _():
        cst[...] = jnp.zeros_like(cst)
        msc[...] = jnp.zeros_like(msc)
        ubuf[0:SUBLANES, :] = jnp.zeros((SUBLANES, 2 * HW), F32)

    @pl.when(g > 0)
    def _():
        ubuf[0:SUBLANES, :] = ubuf[T:T + SUBLANES, :]

    ubuf[SUBLANES:SUBLANES + T, :] = qk_ref[0].astype(F32)

    def conv_strip(c, carry):
        r0 = pl.multiple_of(c * CHUNK, CHUNK)
        strip = ubuf[pl.ds(r0, CHUNK + SUBLANES), :]
        y = cb_ref[...]
        for j in range(CONV_WIDTH):
            off = SUBLANES - (CONV_WIDTH - 1) + j
            y = y + cw_ref[j:j + 1, :] * strip[off:off + CHUNK, :]
        qk = y * jax.nn.sigmoid(y)
        q_sc[pl.ds(r0, CHUNK), :] = qk[:, :HW].astype(BF16)
        k_sc[pl.ds(r0, CHUNK), :] = (qk[:, HW:] * (M_QK_DIM ** -0.5)).astype(BF16)
        return carry

    lax.fori_loop(0, GCH, conv_strip, 0)

    gc = gc_ref[0] + bc_ref[...]
    lane = lax.broadcasted_iota(jnp.int32, gc.shape, 1)
    gcs[...] = jnp.where(lane < M_HEADS, gc, _log_sigmoid(gc))
    gr = gr_ref[0] + br_ref[...]
    row = lax.broadcasted_iota(jnp.int32, gr.shape, 1)
    grs[...] = jnp.where(row < M_HEADS, gr, _log_sigmoid(gr))

    ti = lax.broadcasted_iota(jnp.int32, (CHUNK, CHUNK), 0)
    si = lax.broadcasted_iota(jnp.int32, (CHUNK, CHUNK), 1)
    causal = si <= ti
    tri = causal.astype(F32)
    tri_t = (ti <= si).astype(F32)
    lane256 = lax.broadcasted_iota(jnp.int32, (CHUNK, HW), 1)
    ones_blk = jnp.ones((CHUNK, LANES), BF16)
    ones_sq = jnp.ones((LANES, LANES), BF16)

    def twice(a):
        return jnp.concatenate([a, a], axis=1)

    def rep(col):
        return jnp.broadcast_to(col, (CHUNK, LANES))

    def chunk(c):
        r0 = pl.multiple_of(c * CHUNK, CHUNK)
        qc = q_sc[pl.ds(r0, CHUNK), :]
        kc = k_sc[pl.ds(r0, CHUNK), :]
        vc = v_ref[0, pl.ds(r0, CHUNK), :]
        gcc = gcs[pl.ds(r0, CHUNK), :]
        grr = grs[c]
        b_c = jnp.dot(tri, gcc, preferred_element_type=F32, precision=HIGHEST)
        b_r = jnp.dot(grr, tri_t, preferred_element_type=F32, precision=HIGHEST)
        qstack = jnp.concatenate(
            [jnp.where((lane256 // M_QK_DIM) == h, qc, jnp.zeros_like(qc)) for h in range(M_HEADS)], axis=0)
        kstack = jnp.concatenate(
            [jnp.where((lane256 // M_QK_DIM) == h, kc, jnp.zeros_like(kc)) for h in range(M_HEADS)], axis=0)
        s_all = lax.dot_general(qstack, kc, (((1,), (1,)), ((), ())), preferred_element_type=F32)
        i_all = _dot(qstack, cst[...].astype(BF16))
        vws, decays, freshs = [], [], []
        for h in range(M_HEADS):
            rs = slice(h * CHUNK, (h + 1) * CHUNK)
            bcol = rep(b_c[:, M_HEADS + h:M_HEADS + h + 1])
            icol = rep(gcc[:, h:h + 1])
            brow = b_r[M_HEADS + h:M_HEADS + h + 1, :]
            irow = grr[h:h + 1, :]
            b_last = bcol[CHUNK - 1:CHUNK, :]
            m_prev = msc[h:h + 1, :]
            dlog = jnp.where(causal, bcol[:, :CHUNK] - brow + irow, -jnp.inf)
            m_inter = bcol + m_prev
            m_t = jnp.maximum(m_inter, jnp.max(dlog, axis=-1, keepdims=True))
            d_w = jnp.exp(dlog - m_t[:, :CHUNK])
            inter_w = jnp.exp(m_inter - m_t)
            p = (s_all[rs, :] * d_w).astype(BF16)
            vaug = jnp.concatenate([vc[:, h * M_V_DIM:(h + 1) * M_V_DIM], ones_blk], axis=1)
            r = _dot(p, vaug) + twice(inter_w) * i_all[rs, :]
            hv = r[:, :M_V_DIM] / jnp.maximum(jnp.abs(r[:, M_V_DIM:]), jnp.exp(-m_t))
            hh = hv * hv
            hh_hi = hh.astype(BF16)
            hh_lo = (hh - hh_hi.astype(F32)).astype(BF16)
            ssq = _dot(hh_hi, ones_sq) + _dot(hh_lo, ones_sq)
            hn = hv * lax.rsqrt(ssq * (1.0 / M_V_DIM) + EPS) * ng_ref[:, h * M_V_DIM:(h + 1) * M_V_DIM]
            og = o_ref[0, pl.ds(r0, CHUNK), h * M_V_DIM:(h + 1) * M_V_DIM].astype(F32)
            out_ref[0, pl.ds(r0, CHUNK), h * M_V_DIM:(h + 1) * M_V_DIM] = (hn * jax.nn.sigmoid(og)).astype(BF16)
            a_col = b_last - bcol + icol
            m_loc = jnp.max(a_col, axis=0, keepdims=True)
            w_col = jnp.exp(a_col - m_loc)
            vws.append((vaug.astype(F32) * twice(w_col)).astype(BF16))
            m_new = jnp.maximum(b_last + m_prev, m_loc)
            decays.append(jnp.exp(b_last + m_prev - m_new))
            freshs.append(jnp.exp(m_loc - m_new))
            msc[h:h + 1, :] = m_new
        c_loc = lax.dot_general(kstack, jnp.concatenate(vws, axis=0), (((0,), (0,)), ((), ())),
                                preferred_element_type=F32)
        for h in range(M_HEADS):
            ks = slice(h * M_QK_DIM, (h + 1) * M_QK_DIM)
            cst[ks, :] = twice(decays[h]) * cst[ks, :] + twice(freshs[h]) * c_loc[ks, :]

    def chunk_pair(c2, carry):
        chunk(2 * c2)
        chunk(2 * c2 + 1)
        return carry

    lax.fori_loop(0, GCH // 2, chunk_pair, 0)


def _mlstm(p_all, gates_c, gates_r, bias_c, bias_r, conv_w, conv_b, ng):
    B, S, _ = p_all.shape
    T = GCH * CHUNK
    return pl.pallas_call(
        _mlstm_kernel,
        grid=(B, S // T),
        in_specs=[pl.BlockSpec((1, T, 512), lambda b, g: (b, g, 0)),
                  pl.BlockSpec((1, T, 512), lambda b, g: (b, g, 1)),
                  pl.BlockSpec((1, T, 512), lambda b, g: (b, g, 2)),
                  pl.BlockSpec((1, T, GW), lambda b, g: (b, g, 0)),
                  pl.BlockSpec((1, GCH, SUBLANES, CHUNK), lambda b, g: (b, g, 0, 0)),
                  pl.BlockSpec((1, GW), lambda b, g: (0, 0)),
                  pl.BlockSpec((SUBLANES, 1), lambda b, g: (0, 0)),
                  pl.BlockSpec((CONV_WIDTH, 512), lambda b, g: (0, 0)),
                  pl.BlockSpec((1, 512), lambda b, g: (0, 0)),
                  pl.BlockSpec((1, 512), lambda b, g: (0, 0))],
        out_specs=pl.BlockSpec((1, T, 512), lambda b, g: (b, g, 0)),
        out_shape=jax.ShapeDtypeStruct((B, S, 512), BF16),
        scratch_shapes=[pltpu.VMEM((T + SUBLANES, 512), F32),
                        pltpu.VMEM((T, 256), BF16),
                        pltpu.VMEM((T, 256), BF16),
                        pltpu.VMEM((T, GW), F32),
                        pltpu.VMEM((GCH, SUBLANES, CHUNK), F32),
                        pltpu.VMEM((M_HEADS * M_QK_DIM, 2 * M_V_DIM), F32),
                        pltpu.VMEM((SUBLANES, LANES), F32)],
        compiler_params=_cparams(("arbitrary", "arbitrary")),
        name="mlstm",
    )(p_all, p_all, p_all, gates_c, gates_r, bias_c, bias_r, conv_w, conv_b, ng)


def _attn_kernel(lq1_ref, lk1_ref, lq2_ref, lk2_ref, q_ref, k_ref, v_ref, ng_ref, o_ref,
                 qs_sc, m_sc, acc_sc, sa_sc, sb_sc):
    i = pl.program_id(2)
    q = q_ref[0]
    lane = lax.broadcasted_iota(jnp.int32, q.shape, 1)
    qs_sc[0:TQ, :] = jnp.where(lane < A_HEAD_DIM, q, jnp.zeros_like(q))
    qs_sc[TQ:2 * TQ, :] = jnp.where(lane >= A_HEAD_DIM, q, jnp.zeros_like(q))
    m_sc[...] = jnp.full(m_sc.shape, -jnp.inf, F32)
    acc_sc[...] = jnp.zeros_like(acc_sc)
    ones_blk = jnp.where(lax.broadcasted_iota(jnp.int32, (TQ, LANES), 1) == 0, 1.0, 0.0).astype(BF16)

    def scores(j):
        k = k_ref[0, pl.ds(pl.multiple_of(j * TQ, TQ), TQ), :]
        return lax.dot_general(qs_sc[...], k, (((1,), (1,)), ((), ())), preferred_element_type=F32)

    def diag_scores():
        ri = lax.broadcasted_iota(jnp.int32, (2 * TQ, TQ), 0)
        ci = lax.broadcasted_iota(jnp.int32, (2 * TQ, TQ), 1)
        visible = (ci // CHUNK) <= ((ri & (TQ - 1)) // CHUNK)
        return jnp.where(visible, scores(i), -jnp.inf)

    def accumulate(s_ref, j):
        s = s_ref[...]
        v = v_ref[0, pl.ds(pl.multiple_of(j * TQ, TQ), TQ), :]
        m_old = m_sc[...]
        m_new = jnp.maximum(m_old, jnp.max(s, axis=-1, keepdims=True))
        alpha = jnp.exp(m_old - m_new)
        p = jnp.exp(s - jnp.concatenate([m_new] * (TQ // LANES), axis=1))
        pv = _dot(p.astype(BF16), jnp.concatenate([v, ones_blk], axis=1))
        acc_sc[...] = jnp.concatenate([alpha, alpha], axis=1) * acc_sc[...] + pv
        m_sc[...] = m_new

    def kv_of(t):
        return jnp.where(t == 0, i, t - 1)

    sa_sc[...] = diag_scores()
    pairs = i // 2

    def body(u, carry):
        t = 2 * u
        sb_sc[...] = scores(t)
        accumulate(sa_sc, kv_of(t))
        sa_sc[...] = scores(t + 1)
        accumulate(sb_sc, t)
        return carry

    lax.fori_loop(0, pairs, body, 0)
    t0 = 2 * pairs

    @pl.when(i == t0)
    def _():
        accumulate(sa_sc, kv_of(t0))

    @pl.when(i > t0)
    def _():
        sb_sc[...] = scores(t0)
        accumulate(sa_sc, kv_of(t0))
        accumulate(sb_sc, t0)

    acc = acc_sc[...]
    o = acc[:, :LANES] / acc[:, LANES:LANES + 1]
    lam = (jnp.exp(jnp.sum(lq1_ref[...] * lk1_ref[...], axis=-1, keepdims=True))
           - jnp.exp(jnp.sum(lq2_ref[...] * lk2_ref[...], axis=-1, keepdims=True)) + LAMBDA_INIT)
    a = o[0:TQ, :] - lam * o[TQ:2 * TQ, :]
    y = a * lax.rsqrt(jnp.mean(a * a, axis=-1, keepdims=True) + EPS) * ng_ref[...]
    o_ref[0] = (y * (1.0 - LAMBDA_INIT)).astype(BF16)


def _attn(p_all, lq1, lk1, lq2, lk2, ng):
    B, S, _ = p_all.shape
    nsec = 512 // LANES
    lam_spec = pl.BlockSpec((1, A_HEAD_DIM), lambda b, h, i: (0, 0))
    return pl.pallas_call(
        _attn_kernel,
        grid=(B, A_HEADS, S // TQ),
        in_specs=[lam_spec, lam_spec, lam_spec, lam_spec,
                  pl.BlockSpec((1, TQ, LANES), lambda b, h, i: (b, i, 3 * nsec + h)),
                  pl.BlockSpec((1, S, LANES), lambda b, h, i: (b, 0, 4 * nsec + h)),
                  pl.BlockSpec((1, S, LANES), lambda b, h, i: (b, 0, 5 * nsec + h)),
                  pl.BlockSpec((1, LANES), lambda b, h, i: (0, h))],
        out_specs=pl.BlockSpec((1, TQ, LANES), lambda b, h, i: (b, i, h)),
        out_shape=jax.ShapeDtypeStruct((B, S, 512), BF16),
        scratch_shapes=[pltpu.VMEM((2 * TQ, LANES), BF16),
                        pltpu.VMEM((2 * TQ, LANES), F32),
                        pltpu.VMEM((2 * TQ, 2 * LANES), F32),
                        pltpu.VMEM((2 * TQ, TQ), F32),
                        pltpu.VMEM((2 * TQ, TQ), F32)],
        compiler_params=_cparams(("arbitrary", "arbitrary", "arbitrary")),
        name="attn",
    )(lq1, lk1, lq2, lk2, p_all, p_all, p_all, ng)


def _outproj_kernel(hm_ref, ha_ref, x_ref, mod_ref, g2_ref, wo_ref, wrh_ref, wrl_ref, br_ref, ltri_ref,
                    x1_ref, h2_ref, idx_ref, gate_ref, rank_ref, cnt_ref, cnt_sc):
    @pl.when((pl.program_id(0) == 0) & (pl.program_id(1) == 0))
    def _():
        cnt_sc[...] = jnp.zeros_like(cnt_sc)

    hcat = jnp.concatenate([hm_ref[0], ha_ref[0]], axis=1)
    mix = _dot(hcat, wo_ref[...])
    gate1 = mod_ref[0, 2:3, :]
    shift2 = mod_ref[0, 3:4, :]
    scale2 = mod_ref[0, 4:5, :]
    x1 = x_ref[0] + gate1 * mix
    x1_ref[0] = x1
    ms = jnp.mean(x1 * x1, axis=-1, keepdims=True)
    h2 = (x1 * lax.rsqrt(ms + EPS) * g2_ref[...]) * (1.0 + scale2) + shift2
    tm = h2.shape[0]
    for s in range(SUBLANES):
        h2_ref[pl.ds(s, tm, stride=SUBLANES), :] = h2[:, s * LANES:(s + 1) * LANES]
    hb = h2.astype(BF16)
    hl = (h2 - hb.astype(F32)).astype(BF16)
    logits = _dot(hb, wrh_ref[...]) + _dot(hl, wrh_ref[...]) + _dot(hb, wrl_ref[...]) + br_ref[...]
    lane = lax.broadcasted_iota(jnp.int32, logits.shape, 1).astype(F32)
    vals, idxs = [], []
    work = logits
    for _ in range(TOP_K):
        mx = jnp.max(work, axis=-1, keepdims=True)
        ix = jnp.min(jnp.where(work == mx, lane, float(LANES)), axis=-1, keepdims=True)
        vals.append(mx)
        idxs.append(ix)
        work = jnp.where(lane == ix, -jnp.inf, work)
    es = [jnp.exp(v - vals[0]) for v in vals]
    tot = es[0] + es[1] + es[2] + es[3]
    gsel = jnp.zeros_like(logits)
    isel = jnp.zeros_like(logits)
    for k in range(TOP_K):
        gsel = jnp.where(lane == float(k), es[k] / tot, gsel)
        isel = jnp.where(lane == float(k), idxs[k], isel)
    gate_ref[0] = gsel[:, :TOP_K]
    idx_ref[0] = isel[:, :TOP_K].astype(jnp.int32)
    chosen = [lane == ix for ix in idxs]
    multi = jnp.zeros_like(logits)
    for ch in chosen:
        multi = jnp.where(ch, 1.0, multi)
    before = _dot(ltri_ref[...], multi.astype(BF16)) + cnt_sc[...]
    rsel = jnp.zeros_like(logits)
    for k in range(TOP_K):
        rk = jnp.sum(jnp.where(chosen[k], before, 0.0), axis=-1, keepdims=True)
        rsel = jnp.where(lane == float(k), rk, rsel)
    rank_ref[0] = rsel[:, :TOP_K].astype(jnp.int32)
    cnt_sc[...] = cnt_sc[...] + jnp.sum(multi, axis=0, keepdims=True)
    cnt_ref[...] = cnt_sc[...]


def _outproj(hm, ha, x, mod3, g2, wo, wr_hi, wr_lo, br, ltri):
    B, S, D = x.shape
    nt = S // TM
    return pl.pallas_call(
        _outproj_kernel,
        grid=(B, nt),
        in_specs=[pl.BlockSpec((1, TM, 512), lambda b, s: (b, s, 0)),
                  pl.BlockSpec((1, TM, 512), lambda b, s: (b, s, 0)),
                  pl.BlockSpec((1, TM, D), lambda b, s: (b, s, 0)),
                  pl.BlockSpec((1, 6, D), lambda b, s: (b, 0, 0)),
                  pl.BlockSpec((1, D), lambda b, s: (0, 0)),
                  pl.BlockSpec((D, D), lambda b, s: (0, 0)),
                  pl.BlockSpec((D, LANES), lambda b, s: (0, 0)),
                  pl.BlockSpec((D, LANES), lambda b, s: (0, 0)),
                  pl.BlockSpec((1, LANES), lambda b, s: (0, 0)),
                  pl.BlockSpec((TM, TM), lambda b, s: (0, 0))],
        out_specs=[pl.BlockSpec((1, TM, D), lambda b, s: (b, s, 0)),
                   pl.BlockSpec((TM * SUBLANES, LANES), lambda b, s: (b * nt + s, 0)),
                   pl.BlockSpec((1, TM, TOP_K), lambda b, s: (b, s, 0)),
                   pl.BlockSpec((1, TM, TOP_K), lambda b, s: (b, s, 0)),
                   pl.BlockSpec((1, TM, TOP_K), lambda b, s: (b, s, 0)),
                   pl.BlockSpec((1, LANES), lambda b, s: (0, 0))],
        out_shape=[jax.ShapeDtypeStruct((B, S, D), F32),
                   jax.ShapeDtypeStruct((B * S * SUBLANES, LANES), F32),
                   jax.ShapeDtypeStruct((B, S, TOP_K), jnp.int32),
                   jax.ShapeDtypeStruct((B, S, TOP_K), F32),
                   jax.ShapeDtypeStruct((B, S, TOP_K), jnp.int32),
                   jax.ShapeDtypeStruct((1, LANES), F32)],
        scratch_shapes=[pltpu.VMEM((1, LANES), F32)],
        compiler_params=_cparams(("arbitrary", "arbitrary")),
        name="outproj",
    )(hm, ha, x, mod3, g2, wo, wr_hi, wr_lo, br, ltri)


DMA_UNROLL = 8


def _row_copy(src, dst, src_row, dst_row, sem):
    def tile_start(row):
        start = row * SUBLANES
        return start if isinstance(start, int) else pl.multiple_of(start, SUBLANES)

    return pltpu.make_async_copy(
        src.at[pl.ds(tile_start(src_row), SUBLANES), :],
        dst.at[pl.ds(tile_start(dst_row), SUBLANES), :],
        sem)


def _wait_rows(hbm, n_rows, sem):
    span = hbm.at[pl.ds(0, n_rows * SUBLANES), :]
    pltpu.make_async_copy(span, span, sem).wait()


def _dispatch_kernel(pad_ref, pos_ref, h2_ref, xs_hbm, zero_sc, sem, sem_pad, *, n_pad):
    i = pl.program_id(0)
    n = TOP_K * TM

    @pl.when(i == 0)
    def _():
        zero_sc[...] = jnp.zeros_like(zero_sc)

        def issue_pad(g, carry):
            for u in range(DMA_UNROLL):
                _row_copy(zero_sc, xs_hbm, 0, pad_ref[g * DMA_UNROLL + u], sem_pad).start(priority=u % 2)
            return carry

        lax.fori_loop(0, n_pad // DMA_UNROLL, issue_pad, 0)

    def issue(g, carry):
        for u in range(DMA_UNROLL):
            j = g * DMA_UNROLL + u
            tok = g * (DMA_UNROLL // TOP_K) + u // TOP_K
            _row_copy(h2_ref, xs_hbm, tok, pos_ref[0, 0, j], sem).start(priority=u % 2)
        return carry

    lax.fori_loop(0, n // DMA_UNROLL, issue, 0)
    _wait_rows(xs_hbm, n, sem)

    @pl.when(i == 0)
    def _():
        _wait_rows(xs_hbm, n_pad, sem_pad)


def _dispatch(pad_rows, pos_tiles, h2_rows, n_rows):
    n_tiles = pos_tiles.shape[0]
    n_pad = pad_rows.shape[0]
    return pl.pallas_call(
        functools.partial(_dispatch_kernel, n_pad=n_pad),
        grid_spec=pltpu.PrefetchScalarGridSpec(
            num_scalar_prefetch=1,
            grid=(n_tiles,),
            in_specs=[pl.BlockSpec((1, 1, TOP_K * TM), lambda i, p: (i, 0, 0), memory_space=pltpu.SMEM),
                      pl.BlockSpec((TM * SUBLANES, LANES), lambda i, p: (i, 0))],
            out_specs=pl.BlockSpec(memory_space=pl.ANY),
            scratch_shapes=[pltpu.VMEM((SUBLANES, LANES), F32),
                            pltpu.SemaphoreType.DMA(()),
                            pltpu.SemaphoreType.DMA(())]),
        out_shape=jax.ShapeDtypeStruct((n_rows * SUBLANES, LANES), F32),
        compiler_params=_cparams(("arbitrary",)),
        name="dispatch",
    )(pad_rows, pos_tiles, h2_rows)


def _wprep_kernel(w_ref, p_ref, o_ref):
    half = o_ref.shape[2] // 2
    for cblk in range(o_ref.shape[2] // 256):
        blk = _dot(w_ref[0, :, cblk * 256:(cblk + 1) * 256].astype(BF16), p_ref[...])
        o_ref[0, :, cblk * LANES:(cblk + 1) * LANES] = blk[:, :LANES].astype(BF16)
        o_ref[0, :, half + cblk * LANES:half + (cblk + 1) * LANES] = blk[:, LANES:].astype(BF16)


def _wprep(w_gu, psel):
    E, D, F2 = w_gu.shape
    return pl.pallas_call(
        _wprep_kernel,
        grid=(E,),
        in_specs=[pl.BlockSpec((1, D, F2), lambda e: (e, 0, 0)),
                  pl.BlockSpec((256, 256), lambda e: (0, 0))],
        out_specs=pl.BlockSpec((1, D, F2), lambda e: (e, 0, 0)),
        out_shape=jax.ShapeDtypeStruct((E, D, F2), BF16),
        compiler_params=_cparams(("arbitrary",)),
        name="wprep",
    )(w_gu, psel)


def _expert_kernel(be_ref, nused_ref, xs_ref, wgu_ref, bg_ref, bu_ref, wd_ref, bd_ref, ys_ref):
    i = pl.program_id(0)

    @pl.when(i < nused_ref[0])
    def _():
        x = jnp.concatenate([xs_ref[pl.ds(s, RB, stride=SUBLANES), :] for s in range(SUBLANES)], axis=1).astype(BF16)
        gu = _dot(x, wgu_ref[0])
        half = gu.shape[1] // 2
        gate = jnp.minimum(gu[:, :half] + bg_ref[0], SWIGLU_LIMIT)
        up = jnp.clip(gu[:, half:] + bu_ref[0], -SWIGLU_LIMIT, SWIGLU_LIMIT)
        act = (up + 1.0) * (gate * jax.nn.sigmoid(SWIGLU_ALPHA * gate))
        out = _dot(act.astype(BF16), wd_ref[0]) + bd_ref[0]
        for s in range(SUBLANES):
            ys_ref[pl.ds(s, RB, stride=SUBLANES), :] = out[:, s * LANES:(s + 1) * LANES]

    @pl.when(i >= nused_ref[0])
    def _():
        ys_ref[...] = jnp.zeros_like(ys_ref)


def _experts(block_expert, nused, xs, wgu, bg, bu, wd, bd, n_blocks):
    D = wgu.shape[1]
    F = wgu.shape[2] // 2
    wmap = lambda i, be, n: (be[i], 0, 0)
    return pl.pallas_call(
        _expert_kernel,
        grid_spec=pltpu.PrefetchScalarGridSpec(
            num_scalar_prefetch=2,
            grid=(n_blocks,),
            in_specs=[pl.BlockSpec((RB * SUBLANES, LANES), lambda i, be, n: (i, 0)),
                      pl.BlockSpec((1, D, 2 * F), wmap),
                      pl.BlockSpec((1, 1, F), wmap),
                      pl.BlockSpec((1, 1, F), wmap),
                      pl.BlockSpec((1, F, D), wmap),
                      pl.BlockSpec((1, 1, D), wmap)],
            out_specs=pl.BlockSpec((RB * SUBLANES, LANES), lambda i, be, n: (i, 0))),
        out_shape=jax.ShapeDtypeStruct((n_blocks * RB * SUBLANES, LANES), F32),
        compiler_params=_cparams(("arbitrary",)),
        name="experts",
    )(block_expert, nused, xs, wgu, bg, bu, wd, bd)


def _combine_kernel(pos_ref, ys_hbm, x1_ref, mod_ref, gate_ref, o_ref, buf, sem):
    n = TOP_K * TC

    def issue(g, carry):
        for u in range(DMA_UNROLL):
            j = g * DMA_UNROLL + u
            _row_copy(ys_hbm, buf, pos_ref[0, 0, j], j, sem).start(priority=u % 2)
        return carry

    lax.fori_loop(0, n // DMA_UNROLL, issue, 0)
    _wait_rows(ys_hbm, n, sem)

    gates = gate_ref[0]
    y = None
    for k in range(TOP_K):
        rows = jnp.concatenate(
            [buf[pl.ds(k * TC * SUBLANES + s, TC, stride=SUBLANES), :] for s in range(SUBLANES)], axis=1)
        term = gates[:, k:k + 1] * rows
        y = term if y is None else y + term
    gate2 = mod_ref[0, 5:6, :]
    o_ref[0] = x1_ref[0] + gate2 * y


def _combine(pos_tiles, ys, x1, mod3, gates):
    B, S, D = x1.shape
    nt = S // TC
    return pl.pallas_call(
        _combine_kernel,
        grid=(B, nt),
        in_specs=[pl.BlockSpec((1, 1, TOP_K * TC), lambda b, s: (b * nt + s, 0, 0), memory_space=pltpu.SMEM),
                  pl.BlockSpec(memory_space=pl.ANY),
                  pl.BlockSpec((1, TC, D), lambda b, s: (b, s, 0)),
                  pl.BlockSpec((1, 6, D), lambda b, s: (b, 0, 0)),
                  pl.BlockSpec((1, TC, TOP_K), lambda b, s: (b, s, 0))],
        out_specs=pl.BlockSpec((1, TC, D), lambda b, s: (b, s, 0)),
        out_shape=jax.ShapeDtypeStruct((B, S, D), F32),
        scratch_shapes=[pltpu.VMEM((TOP_K * TC * SUBLANES, LANES), F32),
                        pltpu.SemaphoreType.DMA(())],
        compiler_params=_cparams(("arbitrary", "arbitrary")),
        name="combine",
    )(pos_tiles, ys, x1, mod3, gates)


def _lookup(table, idx):
    hit = idx[..., None] == jnp.arange(table.shape[0], dtype=jnp.int32)
    return jnp.sum(jnp.where(hit, table, 0), axis=-1).astype(jnp.int32)


def _routing_tables(top_idx, rank, counts, n_tokens):
    n_slots = n_tokens * TOP_K
    n_rows = n_slots + N_EXPERTS * RB
    n_blocks = n_rows // RB
    sizes = counts[0, :N_EXPERTS].astype(jnp.int32)
    padded = ((sizes + RB - 1) // RB) * RB
    padded_end = jnp.cumsum(padded).astype(jnp.int32)
    padded_start = padded_end - padded
    pos = _lookup(padded_start, top_idx) + rank
    n_used = padded_end[-1] // RB
    blk = jnp.arange(n_blocks, dtype=jnp.int32)
    block_expert = jnp.minimum(jnp.sum(padded_end[None, :] <= (blk * RB)[:, None], axis=1), N_EXPERTS - 1)
    last_e = jnp.minimum(jnp.sum(padded_end <= (n_used - 1) * RB), N_EXPERTS - 1)
    block_expert = jnp.where(blk < n_used, block_expert, last_e).astype(jnp.int32)
    pad_cnt = padded - sizes
    pad_end = jnp.cumsum(pad_cnt).astype(jnp.int32)
    j = jnp.arange(N_EXPERTS * RB, dtype=jnp.int32)
    e_j = jnp.sum(pad_end[None, :] <= j[:, None], axis=1).astype(jnp.int32)
    in_group = e_j < N_EXPERTS
    e_c = jnp.minimum(e_j, N_EXPERTS - 1)
    group_row = _lookup(padded_start + sizes - (pad_end - pad_cnt), e_c) + j
    tail_row = padded_end[-1] + (j - pad_end[-1])
    pad_rows = jnp.where(in_group, group_row, tail_row).astype(jnp.int32)
    return pos, pad_rows, block_expert, n_used.reshape(1).astype(jnp.int32), n_rows, n_blocks


def kernel(x, c, w_ada, b_ada, norm1_g, w_in, conv_w, conv_b, b_igate, b_fgate, mlstm_norm_g, q_norm_g, k_norm_g,
           lambda_q1, lambda_k1, lambda_q2, lambda_k2, diff_norm_g, w_out, norm2_g, w_router, b_router, w_gu, b_gu,
           w_down, b_down):
    B, S, D = x.shape
    N = B * S
    l = 0
    mq, mk, mv, mo = 256, 256, 512, 512
    o_mq, o_mk, o_mv, o_mo = 0, 256, 512, 1024
    o_mi, o_mf = 1536, 1540
    o_aq, o_ak, o_av = 1544, 2056, 2568

    wi = w_in[l]
    w_main = jnp.concatenate([wi[:, o_mq:o_mq + 512], wi[:, o_mv:o_mv + 512], wi[:, o_mo:o_mo + 512],
                              wi[:, o_aq:o_aq + 512], wi[:, o_ak:o_ak + 512], wi[:, o_av:o_av + 512]],
                             axis=1).astype(BF16)
    wg = jnp.pad(wi[:, o_mi:o_mi + 2 * M_HEADS], ((0, 0), (0, GW - 2 * M_HEADS)))
    wg_hi = wg.astype(BF16)
    wg_lo = (wg - wg_hi.astype(F32)).astype(BF16)
    pos_ids = jnp.arange(S, dtype=F32)
    inv_freq = ROPE_THETA ** (-jnp.arange(0, A_HEAD_DIM, 2, dtype=F32) / A_HEAD_DIM)
    ang = pos_ids[:, None] * inv_freq[None, :]
    cos_h, sin_h = jnp.cos(ang), jnp.sin(ang)
    cos_t = jnp.concatenate([cos_h, cos_h, cos_h, cos_h], axis=1)
    sin_t = jnp.concatenate([-sin_h, sin_h, -sin_h, sin_h], axis=1)
    qg = jnp.tile(q_norm_g[l], 512 // A_HEAD_DIM).reshape(1, 512)
    kg = jnp.tile(k_norm_g[l], 512 // A_HEAD_DIM).reshape(1, 512)
    gid = jnp.arange(512) // A_HEAD_DIM
    bd = (gid[:, None] == gid[None, :]).astype(BF16)
    bias_c = jnp.pad(jnp.concatenate([b_igate[l], b_fgate[l]]), (0, GW - 2 * M_HEADS)).reshape(1, GW)
    bias_r = jnp.concatenate([b_igate[l], b_fgate[l]]).reshape(2 * M_HEADS, 1)
    wr = jnp.pad(w_router[l], ((0, 0), (0, LANES - N_EXPERTS)))
    wr_hi = wr.astype(BF16)
    wr_lo = (wr - wr_hi.astype(F32)).astype(BF16)
    br = jnp.pad(b_router[l], (0, LANES - N_EXPERTS), constant_values=-jnp.inf).reshape(1, LANES)
    col = jnp.arange(256)
    psel = (jnp.where(col % 2 == 0, col // 2, LANES + col // 2)[:, None] == col[None, :]).astype(BF16)
    ltri = (jnp.arange(TM)[None, :] < jnp.arange(TM)[:, None]).astype(BF16)
    bgate = b_gu[l][:, None, 0::2]
    bup = b_gu[l][:, None, 1::2]
    wdn = w_down[l].astype(BF16)
    bdn = b_down[l][:, None, :]

    mod3 = _ada(c, w_ada[l], b_ada[l]).reshape(B, 6, D)
    p_all, gates_c = _inproj(x, mod3, norm1_g[l].reshape(1, D), w_main, wg_hi, wg_lo, cos_t, sin_t, qg, kg, bd)
    gates_r = gates_c[:, :, :2 * M_HEADS].reshape(B, S // CHUNK, CHUNK, 2 * M_HEADS).transpose(0, 1, 3, 2)
    hm = _mlstm(p_all, gates_c, gates_r, bias_c, bias_r, conv_w[l], conv_b[l].reshape(1, 512),
                mlstm_norm_g[l].reshape(1, 512))
    ha = _attn(p_all, lambda_q1[l].reshape(1, -1), lambda_k1[l].reshape(1, -1), lambda_q2[l].reshape(1, -1),
               lambda_k2[l].reshape(1, -1), diff_norm_g[l].reshape(1, 512))

    x1, h2_rows, top_idx, gates, rank, counts = _outproj(hm, ha, x, mod3, norm2_g[l].reshape(1, D),
                                                         w_out[l].astype(BF16), wr_hi, wr_lo, br, ltri)

    pos, pad_rows, block_expert, n_used, n_rows, n_blocks = _routing_tables(top_idx, rank, counts, N)
    xs = _dispatch(pad_rows, pos.reshape(N // TM, 1, TM * TOP_K), h2_rows, n_rows)
    wgu = _wprep(w_gu[l], psel)
    ys = _experts(block_expert, n_used, xs, wgu, bgate, bup, wdn, bdn, n_blocks)
    pos_tiles = pos.reshape(N // TC, TC, TOP_K).transpose(0, 2, 1).reshape(N // TC, 1, TOP_K * TC)
    return _combine(pos_tiles, ys, x1, mod3, gates)
```

```python
import functools
import math

import jax
import jax.numpy as jnp
from jax import lax
from jax.experimental import pallas as pl
from jax.experimental.pallas import tpu as pltpu

F32 = jnp.float32
BF16 = jnp.bfloat16
HIGHEST = lax.Precision.HIGHEST

LANES = 128
SUBLANES = 8
VMEM_LIMIT = 48 * 1024 * 1024

CHUNK = 64
M_HEADS = 4
M_QK_DIM = 64
M_V_DIM = 128
CONV_WIDTH = 4
A_HEADS = 4
A_HEAD_DIM = 64
A_V_DIM = 128
ROPE_THETA = 10000.0
N_EXPERTS = 32
TOP_K = 4
SWIGLU_ALPHA = 1.702
SWIGLU_LIMIT = 7.0
EPS = 1e-6
LAMBDA_INIT = 0.8 - 0.6 * math.exp(-0.3 * 0)

TM = 512
GCH = 8
CPT = 4
TQ = 512
RB = 512
TC = 512
GW = 128


def _dot(a, b):
    return jnp.dot(a, b, preferred_element_type=F32)


def _cparams(sem):
    return pltpu.CompilerParams(dimension_semantics=sem, vmem_limit_bytes=VMEM_LIMIT)


def _ada_kernel(c_ref, w_ref, b_ref, o_ref):
    c = c_ref[...]
    cond = c * jax.nn.sigmoid(c)
    o_ref[...] = jnp.dot(cond, w_ref[...], preferred_element_type=F32, precision=HIGHEST) + b_ref[...]


def _ada(c, w, b):
    B, D = c.shape
    n = w.shape[1]
    tn = 1024
    return pl.pallas_call(
        _ada_kernel,
        grid=(n // tn,),
        in_specs=[pl.BlockSpec((B, D), lambda j: (0, 0)),
                  pl.BlockSpec((D, tn), lambda j: (0, j)),
                  pl.BlockSpec((1, tn), lambda j: (0, j))],
        out_specs=pl.BlockSpec((B, tn), lambda j: (0, j)),
        out_shape=jax.ShapeDtypeStruct((B, n), F32),
        compiler_params=_cparams(("arbitrary",)),
        name="ada",
    )(c, w, b.reshape(1, n))


def _inproj_kernel(x_ref, mod_ref, g1_ref, w_ref, wgh_ref, wgl_ref, cos_ref, sin_ref, qg_ref, kg_ref,
                   bd_ref, p_ref, gates_ref):
    x = x_ref[0]
    ms = jnp.mean(x * x, axis=-1, keepdims=True)
    shift = mod_ref[0, 0:1, :]
    scale = mod_ref[0, 1:2, :]
    h = (x * lax.rsqrt(ms + EPS) * g1_ref[...]) * (1.0 + scale) + shift
    hb = h.astype(BF16)
    hl = (h - hb.astype(F32)).astype(BF16)
    gates_ref[0] = _dot(hb, wgh_ref[...]) + _dot(hl, wgh_ref[...]) + _dot(hb, wgl_ref[...])

    tm = x.shape[0]
    lane = lax.broadcasted_iota(jnp.int32, (tm, 512), 1)
    first_half = (lane & 63) < 32
    cos = jnp.concatenate([cos_ref[...]] * 4, axis=1)
    sin = jnp.concatenate([sin_ref[...]] * 4, axis=1)
    for sec in range(6):
        acc = _dot(hb, w_ref[:, sec * 512:(sec + 1) * 512])
        if sec in (3, 4):
            g = qg_ref[...] if sec == 3 else kg_ref[...]
            ssq = _dot((acc * acc).astype(BF16), bd_ref[...])
            y = acc * lax.rsqrt(ssq * (1.0 / A_HEAD_DIM) + EPS) * g
            swapped = jnp.where(first_half, pltpu.roll(y, 512 - 32, 1), pltpu.roll(y, 32, 1))
            acc = y * cos + swapped * sin
            if sec == 3:
                acc = acc * (A_HEAD_DIM ** -0.5)
        p_ref[0, :, sec * 512:(sec + 1) * 512] = acc.astype(BF16)


def _inproj(x, mod3, g1, w_main, wg_hi, wg_lo, cos_t, sin_t, qg, kg, bd):
    B, S, D = x.shape
    nw = w_main.shape[1]
    return pl.pallas_call(
        _inproj_kernel,
        grid=(B, S // TM),
        in_specs=[pl.BlockSpec((1, TM, D), lambda b, s: (b, s, 0)),
                  pl.BlockSpec((1, 6, D), lambda b, s: (b, 0, 0)),
                  pl.BlockSpec((1, D), lambda b, s: (0, 0)),
                  pl.BlockSpec((D, nw), lambda b, s: (0, 0)),
                  pl.BlockSpec((D, GW), lambda b, s: (0, 0)),
                  pl.BlockSpec((D, GW), lambda b, s: (0, 0)),
                  pl.BlockSpec((TM, LANES), lambda b, s: (s, 0)),
                  pl.BlockSpec((TM, LANES), lambda b, s: (s, 0)),
                  pl.BlockSpec((1, 512), lambda b, s: (0, 0)),
                  pl.BlockSpec((1, 512), lambda b, s: (0, 0)),
                  pl.BlockSpec((512, 512), lambda b, s: (0, 0))],
        out_specs=[pl.BlockSpec((1, TM, nw), lambda b, s: (b, s, 0)),
                   pl.BlockSpec((1, TM, GW), lambda b, s: (b, s, 0))],
        out_shape=[jax.ShapeDtypeStruct((B, S, nw), BF16),
                   jax.ShapeDtypeStruct((B, S, GW), F32)],
        compiler_params=_cparams(("arbitrary", "arbitrary")),
        name="inproj",
    )(x, mod3, g1, w_main, wg_hi, wg_lo, cos_t, sin_t, qg, kg, bd)


def _log_sigmoid(z):
    return jnp.minimum(z, 0.0) - jnp.log1p(jnp.exp(-jnp.abs(z)))


def _mlstm_kernel(qk_ref, v_ref, o_ref, gc_ref, gr_ref, bc_ref, br_ref, cw_ref, cb_ref, ng_ref,
                  out_ref, ubuf, q_sc, k_sc, gcs, grs, cst, msc):
    g = pl.program_id(1)
    T = GCH * CHUNK
    HW = M_HEADS * M_QK_DIM

    @pl.when(g == 0)
    def _():
        cst[...] = jnp.zeros_like(cst)
        msc[...] = jnp.zeros_like(msc)
        ubuf[0:SUBLANES, :] = jnp.zeros((SUBLANES, 2 * HW), F32)

    @pl.when(g > 0)
    def _():
        ubuf[0:SUBLANES, :] = ubuf[T:T + SUBLANES, :]

    ubuf[SUBLANES:SUBLANES + T, :] = qk_ref[0].astype(F32)

    def conv_strip(c, carry):
        r0 = pl.multiple_of(c * CHUNK, CHUNK)
        strip = ubuf[pl.ds(r0, CHUNK + SUBLANES), :]
        y = cb_ref[...]
        for j in range(CONV_WIDTH):
            off = SUBLANES - (CONV_WIDTH - 1) + j
            y = y + cw_ref[j:j + 1, :] * strip[off:off + CHUNK, :]
        qk = y * jax.nn.sigmoid(y)
        q_sc[pl.ds(r0, CHUNK), :] = qk[:, :HW].astype(BF16)
        k_sc[pl.ds(r0, CHUNK), :] = (qk[:, HW:] * (M_QK_DIM ** -0.5)).astype(BF16)
        return carry

    lax.fori_loop(0, GCH, conv_strip, 0)

    gc = gc_ref[0] + bc_ref[...]
    lane = lax.broadcasted_iota(jnp.int32, gc.shape, 1)
    gcs[...] = jnp.where(lane < M_HEADS, gc, _log_sigmoid(gc))
    gr = gr_ref[0] + br_ref[...]
    row = lax.broadcasted_iota(jnp.int32, gr.shape, 1)
    grs[...] = jnp.where(row < M_HEADS, gr, _log_sigmoid(gr))

    ti = lax.broadcasted_iota(jnp.int32, (CHUNK, CHUNK), 0)
    si = lax.broadcasted_iota(jnp.int32, (CHUNK, CHUNK), 1)
    causal = si <= ti
    tri = causal.astype(F32)
    tri_t = (ti <= si).astype(F32)
    lane256 = lax.broadcasted_iota(jnp.int32, (CHUNK, HW), 1)
    ones_blk = jnp.ones((CHUNK, LANES), BF16)
    ones_sq = jnp.ones((LANES, LANES), BF16)

    def twice(a):
        return jnp.concatenate([a, a], axis=1)

    def rep(col):
        return jnp.broadcast_to(col, (CHUNK, LANES))

    def stage1(c):
        r0 = pl.multiple_of(c * CHUNK, CHUNK)
        qc = q_sc[pl.ds(r0, CHUNK), :]
        kc = k_sc[pl.ds(r0, CHUNK), :]
        vc = v_ref[0, pl.ds(r0, CHUNK), :]
        gcc = gcs[pl.ds(r0, CHUNK), :]
        grr = grs[c]
        b_c = jnp.dot(tri, gcc, preferred_element_type=F32, precision=HIGHEST)
        b_r = jnp.dot(grr, tri_t, preferred_element_type=F32, precision=HIGHEST)
        qstack = jnp.concatenate(
            [jnp.where((lane256 // M_QK_DIM) == h, qc, jnp.zeros_like(qc)) for h in range(M_HEADS)], axis=0)
        kstack = jnp.concatenate(
            [jnp.where((lane256 // M_QK_DIM) == h, kc, jnp.zeros_like(kc)) for h in range(M_HEADS)], axis=0)
        s_all = lax.dot_general(qstack, kc, (((1,), (1,)), ((), ())), preferred_element_type=F32)
        heads, vws = [], []
        for h in range(M_HEADS):
            bcol = rep(b_c[:, M_HEADS + h:M_HEADS + h + 1])
            icol = rep(gcc[:, h:h + 1])
            brow = b_r[M_HEADS + h:M_HEADS + h + 1, :]
            irow = grr[h:h + 1, :]
            b_last = bcol[CHUNK - 1:CHUNK, :]
            dlog = jnp.where(causal, bcol[:, :CHUNK] - brow + irow, -jnp.inf)
            m_intra = jnp.max(dlog, axis=-1, keepdims=True)
            vaug = jnp.concatenate([vc[:, h * M_V_DIM:(h + 1) * M_V_DIM], ones_blk], axis=1)
            a_col = b_last - bcol + icol
            m_loc = jnp.max(a_col, axis=0, keepdims=True)
            w_col = jnp.exp(a_col - m_loc)
            vws.append((vaug.astype(F32) * twice(w_col)).astype(BF16))
            heads.append((bcol, b_last, dlog, m_intra, vaug, m_loc))
        c_loc = lax.dot_general(kstack, jnp.concatenate(vws, axis=0), (((0,), (0,)), ((), ())),
                                preferred_element_type=F32)
        return r0, qstack, s_all, heads, c_loc

    def stage2(sts):
        items = []
        m_run = [msc[h:h + 1, :] for h in range(M_HEADS)]
        for r0, qstack, s_all, heads, c_loc in sts:
            i_all = _dot(qstack, cst[...].astype(BF16))
            for h in range(M_HEADS):
                bcol, b_last, dlog, m_intra, vaug, m_loc = heads[h]
                ks = slice(h * M_QK_DIM, (h + 1) * M_QK_DIM)
                m_prev = m_run[h]
                m_new = jnp.maximum(b_last + m_prev, m_loc)
                decay = jnp.exp(b_last + m_prev - m_new)
                fresh = jnp.exp(m_loc - m_new)
                cst[ks, :] = twice(decay) * cst[ks, :] + twice(fresh) * c_loc[ks, :]
                m_run[h] = m_new
                items.append((r0, h, bcol, dlog, m_intra, vaug, m_prev, s_all, i_all))
        for h in range(M_HEADS):
            msc[h:h + 1, :] = m_run[h]
        prods = []
        for r0, h, bcol, dlog, m_intra, vaug, m_prev, s_all, i_all in items:
            rs = slice(h * CHUNK, (h + 1) * CHUNK)
            m_inter = bcol + m_prev
            m_t = jnp.maximum(m_inter, m_intra)
            d_w = jnp.exp(dlog - m_t[:, :CHUNK])
            inter_w = jnp.exp(m_inter - m_t)
            p = (s_all[rs, :] * d_w).astype(BF16)
            prods.append((_dot(p, vaug) + twice(inter_w) * i_all[rs, :], m_t))
        normed = []
        for r, m_t in prods:
            hv = r[:, :M_V_DIM] / jnp.maximum(jnp.abs(r[:, M_V_DIM:]), jnp.exp(-m_t))
            hh = hv * hv
            hh_hi = hh.astype(BF16)
            hh_lo = (hh - hh_hi.astype(F32)).astype(BF16)
            normed.append((hv, _dot(hh_hi, ones_sq) + _dot(hh_lo, ones_sq)))
        for (r0, h, *_), (hv, ssq) in zip(items, normed):
            hs = slice(h * M_V_DIM, (h + 1) * M_V_DIM)
            hn = hv * lax.rsqrt(ssq * (1.0 / M_V_DIM) + EPS) * ng_ref[:, hs]
            og = o_ref[0, pl.ds(r0, CHUNK), hs].astype(F32)
            out_ref[0, pl.ds(r0, CHUNK), hs] = (hn * jax.nn.sigmoid(og)).astype(BF16)

    def chunk_group(cg, carry):
        stage2([stage1(CPT * cg + k) for k in range(CPT)])
        return carry

    lax.fori_loop(0, GCH // CPT, chunk_group, 0)


def _mlstm(p_all, gates_c, gates_r, bias_c, bias_r, conv_w, conv_b, ng):
    B, S, _ = p_all.shape
    T = GCH * CHUNK
    return pl.pallas_call(
        _mlstm_kernel,
        grid=(B, S // T),
        in_specs=[pl.BlockSpec((1, T, 512), lambda b, g: (b, g, 0)),
                  pl.BlockSpec((1, T, 512), lambda b, g: (b, g, 1)),
                  pl.BlockSpec((1, T, 512), lambda b, g: (b, g, 2)),
                  pl.BlockSpec((1, T, GW), lambda b, g: (b, g, 0)),
                  pl.BlockSpec((1, GCH, SUBLANES, CHUNK), lambda b, g: (b, g, 0, 0)),
                  pl.BlockSpec((1, GW), lambda b, g: (0, 0)),
                  pl.BlockSpec((SUBLANES, 1), lambda b, g: (0, 0)),
                  pl.BlockSpec((CONV_WIDTH, 512), lambda b, g: (0, 0)),
                  pl.BlockSpec((1, 512), lambda b, g: (0, 0)),
                  pl.BlockSpec((1, 512), lambda b, g: (0, 0))],
        out_specs=pl.BlockSpec((1, T, 512), lambda b, g: (b, g, 0)),
        out_shape=jax.ShapeDtypeStruct((B, S, 512), BF16),
        scratch_shapes=[pltpu.VMEM((T + SUBLANES, 512), F32),
                        pltpu.VMEM((T, 256), BF16),
                        pltpu.VMEM((T, 256), BF16),
                        pltpu.VMEM((T, GW), F32),
                        pltpu.VMEM((GCH, SUBLANES, CHUNK), F32),
                        pltpu.VMEM((M_HEADS * M_QK_DIM, 2 * M_V_DIM), F32),
                        pltpu.VMEM((SUBLANES, LANES), F32)],
        compiler_params=_cparams(("arbitrary", "arbitrary")),
        name="mlstm",
    )(p_all, p_all, p_all, gates_c, gates_r, bias_c, bias_r, conv_w, conv_b, ng)


def _attn_kernel(lq1_ref, lk1_ref, lq2_ref, lk2_ref, q_ref, k_ref, v_ref, ng_ref, o_ref,
                 qs_sc, m_sc, acc_sc, sa_sc, sb_sc):
    i = pl.program_id(2)
    q = q_ref[0]
    lane = lax.broadcasted_iota(jnp.int32, q.shape, 1)
    qs_sc[0:TQ, :] = jnp.where(lane < A_HEAD_DIM, q, jnp.zeros_like(q))
    qs_sc[TQ:2 * TQ, :] = jnp.where(lane >= A_HEAD_DIM, q, jnp.zeros_like(q))
    m_sc[...] = jnp.full(m_sc.shape, -jnp.inf, F32)
    acc_sc[...] = jnp.zeros_like(acc_sc)
    ones_blk = jnp.ones((TQ, LANES), BF16)

    def scores(j):
        k = k_ref[0, pl.ds(pl.multiple_of(j * TQ, TQ), TQ), :]
        return lax.dot_general(qs_sc[...], k, (((1,), (1,)), ((), ())), preferred_element_type=F32)

    def diag_scores():
        ri = lax.broadcasted_iota(jnp.int32, (2 * TQ, TQ), 0)
        ci = lax.broadcasted_iota(jnp.int32, (2 * TQ, TQ), 1)
        visible = (ci // CHUNK) <= ((ri & (TQ - 1)) // CHUNK)
        return jnp.where(visible, scores(i), -jnp.inf)

    def accumulate(s_ref, j):
        s = s_ref[...]
        v = v_ref[0, pl.ds(pl.multiple_of(j * TQ, TQ), TQ), :]
        m_old = m_sc[...]
        m_new = jnp.maximum(m_old, jnp.max(s, axis=-1, keepdims=True))
        alpha = jnp.exp(m_old - m_new)
        p = jnp.exp(s - jnp.concatenate([m_new] * (TQ // LANES), axis=1))
        pv = _dot(p.astype(BF16), jnp.concatenate([v, ones_blk], axis=1))
        acc_sc[...] = jnp.concatenate([alpha, alpha], axis=1) * acc_sc[...] + pv
        m_sc[...] = m_new

    def kv_of(t):
        return jnp.where(t == 0, i, t - 1)

    sa_sc[...] = diag_scores()
    pairs = i // 2

    def body(u, carry):
        t = 2 * u
        sb_sc[...] = scores(t)
        accumulate(sa_sc, kv_of(t))
        sa_sc[...] = scores(t + 1)
        accumulate(sb_sc, t)
        return carry

    lax.fori_loop(0, pairs, body, 0)
    t0 = 2 * pairs

    @pl.when(i == t0)
    def _():
        accumulate(sa_sc, kv_of(t0))

    @pl.when(i > t0)
    def _():
        sb_sc[...] = scores(t0)
        accumulate(sa_sc, kv_of(t0))
        accumulate(sb_sc, t0)

    acc = acc_sc[...]
    o = acc[:, :LANES] / acc[:, LANES:]
    lam = (jnp.exp(jnp.sum(lq1_ref[...] * lk1_ref[...], axis=-1, keepdims=True))
           - jnp.exp(jnp.sum(lq2_ref[...] * lk2_ref[...], axis=-1, keepdims=True)) + LAMBDA_INIT)
    a = o[0:TQ, :] - lam * o[TQ:2 * TQ, :]
    y = a * lax.rsqrt(jnp.mean(a * a, axis=-1, keepdims=True) + EPS) * ng_ref[...]
    o_ref[0] = (y * (1.0 - LAMBDA_INIT)).astype(BF16)


def _attn(p_all, lq1, lk1, lq2, lk2, ng):
    B, S, _ = p_all.shape
    nsec = 512 // LANES
    lam_spec = pl.BlockSpec((1, A_HEAD_DIM), lambda b, h, i: (0, 0))
    return pl.pallas_call(
        _attn_kernel,
        grid=(B, A_HEADS, S // TQ),
        in_specs=[lam_spec, lam_spec, lam_spec, lam_spec,
                  pl.BlockSpec((1, TQ, LANES), lambda b, h, i: (b, i, 3 * nsec + h)),
                  pl.BlockSpec((1, S, LANES), lambda b, h, i: (b, 0, 4 * nsec + h)),
                  pl.BlockSpec((1, S, LANES), lambda b, h, i: (b, 0, 5 * nsec + h)),
                  pl.BlockSpec((1, LANES), lambda b, h, i: (0, h))],
        out_specs=pl.BlockSpec((1, TQ, LANES), lambda b, h, i: (b, i, h)),
        out_shape=jax.ShapeDtypeStruct((B, S, 512), BF16),
        scratch_shapes=[pltpu.VMEM((2 * TQ, LANES), BF16),
                        pltpu.VMEM((2 * TQ, LANES), F32),
                        pltpu.VMEM((2 * TQ, 2 * LANES), F32),
                        pltpu.VMEM((2 * TQ, TQ), F32),
                        pltpu.VMEM((2 * TQ, TQ), F32)],
        compiler_params=_cparams(("arbitrary", "arbitrary", "arbitrary")),
        name="attn",
    )(lq1, lk1, lq2, lk2, p_all, p_all, p_all, ng)


def _outproj_kernel(hm_ref, ha_ref, x_ref, mod_ref, g2_ref, wo_ref, wrh_ref, wrl_ref, br_ref, ltri_ref,
                    x1_ref, h2_ref, idx_ref, gate_ref, rank_ref, cnt_ref, cnt_sc):
    @pl.when((pl.program_id(0) == 0) & (pl.program_id(1) == 0))
    def _():
        cnt_sc[...] = jnp.zeros_like(cnt_sc)

    hcat = jnp.concatenate([hm_ref[0], ha_ref[0]], axis=1)
    mix = _dot(hcat, wo_ref[...])
    gate1 = mod_ref[0, 2:3, :]
    shift2 = mod_ref[0, 3:4, :]
    scale2 = mod_ref[0, 4:5, :]
    x1 = x_ref[0] + gate1 * mix
    x1_ref[0] = x1
    ms = jnp.mean(x1 * x1, axis=-1, keepdims=True)
    h2 = (x1 * lax.rsqrt(ms + EPS) * g2_ref[...]) * (1.0 + scale2) + shift2
    tm = h2.shape[0]
    for s in range(SUBLANES):
        h2_ref[pl.ds(s, tm, stride=SUBLANES), :] = h2[:, s * LANES:(s + 1) * LANES]
    hb = h2.astype(BF16)
    hl = (h2 - hb.astype(F32)).astype(BF16)
    logits = _dot(hb, wrh_ref[...]) + _dot(hl, wrh_ref[...]) + _dot(hb, wrl_ref[...]) + br_ref[...]
    lane = lax.broadcasted_iota(jnp.int32, logits.shape, 1).astype(F32)
    vals, idxs = [], []
    work = logits
    for _ in range(TOP_K):
        mx = jnp.max(work, axis=-1, keepdims=True)
        ix = jnp.min(jnp.where(work == mx, lane, float(LANES)), axis=-1, keepdims=True)
        vals.append(mx)
        idxs.append(ix)
        work = jnp.where(lane == ix, -jnp.inf, work)
    es = [jnp.exp(v - vals[0]) for v in vals]
    tot = es[0] + es[1] + es[2] + es[3]
    gsel = jnp.zeros_like(logits)
    isel = jnp.zeros_like(logits)
    for k in range(TOP_K):
        gsel = jnp.where(lane == float(k), es[k] / tot, gsel)
        isel = jnp.where(lane == float(k), idxs[k], isel)
    gate_ref[0] = gsel[:, :TOP_K]
    idx_ref[0] = isel[:, :TOP_K].astype(jnp.int32)
    chosen = [lane == ix for ix in idxs]
    multi = jnp.zeros_like(logits)
    for ch in chosen:
        multi = jnp.where(ch, 1.0, multi)
    before = _dot(ltri_ref[...], multi.astype(BF16)) + cnt_sc[...]
    rsel = jnp.zeros_like(logits)
    for k in range(TOP_K):
        rk = jnp.sum(jnp.where(chosen[k], before, 0.0), axis=-1, keepdims=True)
        rsel = jnp.where(lane == float(k), rk, rsel)
    rank_ref[0] = rsel[:, :TOP_K].astype(jnp.int32)
    cnt_sc[...] = cnt_sc[...] + jnp.sum(multi, axis=0, keepdims=True)
    cnt_ref[...] = cnt_sc[...]


def _outproj(hm, ha, x, mod3, g2, wo, wr_hi, wr_lo, br, ltri):
    B, S, D = x.shape
    nt = S // TM
    return pl.pallas_call(
        _outproj_kernel,
        grid=(B, nt),
        in_specs=[pl.BlockSpec((1, TM, 512), lambda b, s: (b, s, 0)),
                  pl.BlockSpec((1, TM, 512), lambda b, s: (b, s, 0)),
                  pl.BlockSpec((1, TM, D), lambda b, s: (b, s, 0)),
                  pl.BlockSpec((1, 6, D), lambda b, s: (b, 0, 0)),
                  pl.BlockSpec((1, D), lambda b, s: (0, 0)),
                  pl.BlockSpec((D, D), lambda b, s: (0, 0)),
                  pl.BlockSpec((D, LANES), lambda b, s: (0, 0)),
                  pl.BlockSpec((D, LANES), lambda b, s: (0, 0)),
                  pl.BlockSpec((1, LANES), lambda b, s: (0, 0)),
                  pl.BlockSpec((TM, TM), lambda b, s: (0, 0))],
        out_specs=[pl.BlockSpec((1, TM, D), lambda b, s: (b, s, 0)),
                   pl.BlockSpec((TM * SUBLANES, LANES), lambda b, s: (b * nt + s, 0)),
                   pl.BlockSpec((1, TM, TOP_K), lambda b, s: (b, s, 0)),
                   pl.BlockSpec((1, TM, TOP_K), lambda b, s: (b, s, 0)),
                   pl.BlockSpec((1, TM, TOP_K), lambda b, s: (b, s, 0)),
                   pl.BlockSpec((1, LANES), lambda b, s: (0, 0))],
        out_shape=[jax.ShapeDtypeStruct((B, S, D), F32),
                   jax.ShapeDtypeStruct((B * S * SUBLANES, LANES), F32),
                   jax.ShapeDtypeStruct((B, S, TOP_K), jnp.int32),
                   jax.ShapeDtypeStruct((B, S, TOP_K), F32),
                   jax.ShapeDtypeStruct((B, S, TOP_K), jnp.int32),
                   jax.ShapeDtypeStruct((1, LANES), F32)],
        scratch_shapes=[pltpu.VMEM((1, LANES), F32)],
        compiler_params=_cparams(("arbitrary", "arbitrary")),
        name="outproj",
    )(hm, ha, x, mod3, g2, wo, wr_hi, wr_lo, br, ltri)


DMA_UNROLL = 8


def _row_copy(src, dst, src_row, dst_row, sem):
    def tile_start(row):
        start = row * SUBLANES
        return start if isinstance(start, int) else pl.multiple_of(start, SUBLANES)

    return pltpu.make_async_copy(
        src.at[pl.ds(tile_start(src_row), SUBLANES), :],
        dst.at[pl.ds(tile_start(dst_row), SUBLANES), :],
        sem)


def _wait_rows(hbm, n_rows, sem):
    span = hbm.at[pl.ds(0, n_rows * SUBLANES), :]
    pltpu.make_async_copy(span, span, sem).wait()


def _dispatch_kernel(pad_ref, pos_ref, h2_ref, xs_hbm, zero_sc, sem, sem_pad, *, n_pad):
    i = pl.program_id(0)
    n = TOP_K * TM

    @pl.when(i == 0)
    def _():
        zero_sc[...] = jnp.zeros_like(zero_sc)

        def issue_pad(g, carry):
            for u in range(DMA_UNROLL):
                _row_copy(zero_sc, xs_hbm, 0, pad_ref[g * DMA_UNROLL + u], sem_pad).start(priority=u % 2)
            return carry

        lax.fori_loop(0, n_pad // DMA_UNROLL, issue_pad, 0)

    def issue(g, carry):
        for u in range(DMA_UNROLL):
            j = g * DMA_UNROLL + u
            tok = g * (DMA_UNROLL // TOP_K) + u // TOP_K
            _row_copy(h2_ref, xs_hbm, tok, pos_ref[0, 0, j], sem).start(priority=u % 2)
        return carry

    lax.fori_loop(0, n // DMA_UNROLL, issue, 0)
    _wait_rows(xs_hbm, n, sem)

    @pl.when(i == 0)
    def _():
        _wait_rows(xs_hbm, n_pad, sem_pad)


def _dispatch(pad_rows, pos_tiles, h2_rows, n_rows):
    n_tiles = pos_tiles.shape[0]
    n_pad = pad_rows.shape[0]
    return pl.pallas_call(
        functools.partial(_dispatch_kernel, n_pad=n_pad),
        grid_spec=pltpu.PrefetchScalarGridSpec(
            num_scalar_prefetch=1,
            grid=(n_tiles,),
            in_specs=[pl.BlockSpec((1, 1, TOP_K * TM), lambda i, p: (i, 0, 0), memory_space=pltpu.SMEM),
                      pl.BlockSpec((TM * SUBLANES, LANES), lambda i, p: (i, 0))],
            out_specs=pl.BlockSpec(memory_space=pl.ANY),
            scratch_shapes=[pltpu.VMEM((SUBLANES, LANES), F32),
                            pltpu.SemaphoreType.DMA(()),
                            pltpu.SemaphoreType.DMA(())]),
        out_shape=jax.ShapeDtypeStruct((n_rows * SUBLANES, LANES), F32),
        compiler_params=_cparams(("arbitrary",)),
        name="dispatch",
    )(pad_rows, pos_tiles, h2_rows)


def _wprep_kernel(w_ref, p_ref, o_ref):
    half = o_ref.shape[2] // 2
    for cblk in range(o_ref.shape[2] // 256):
        blk = _dot(w_ref[0, :, cblk * 256:(cblk + 1) * 256].astype(BF16), p_ref[...])
        o_ref[0, :, cblk * LANES:(cblk + 1) * LANES] = blk[:, :LANES].astype(BF16)
        o_ref[0, :, half + cblk * LANES:half + (cblk + 1) * LANES] = blk[:, LANES:].astype(BF16)


def _wprep(w_gu, psel):
    E, D, F2 = w_gu.shape
    return pl.pallas_call(
        _wprep_kernel,
        grid=(E,),
        in_specs=[pl.BlockSpec((1, D, F2), lambda e: (e, 0, 0)),
                  pl.BlockSpec((256, 256), lambda e: (0, 0))],
        out_specs=pl.BlockSpec((1, D, F2), lambda e: (e, 0, 0)),
        out_shape=jax.ShapeDtypeStruct((E, D, F2), BF16),
        compiler_params=_cparams(("arbitrary",)),
        name="wprep",
    )(w_gu, psel)


def _expert_kernel(be_ref, nused_ref, xs_ref, wgu_ref, bg_ref, bu_ref, wd_ref, bd_ref, ys_ref):
    i = pl.program_id(0)

    @pl.when(i < nused_ref[0])
    def _():
        x = jnp.concatenate([xs_ref[pl.ds(s, RB, stride=SUBLANES), :] for s in range(SUBLANES)], axis=1).astype(BF16)
        gu = _dot(x, wgu_ref[0])
        half = gu.shape[1] // 2
        gate = jnp.minimum(gu[:, :half] + bg_ref[0], SWIGLU_LIMIT)
        up = jnp.clip(gu[:, half:] + bu_ref[0], -SWIGLU_LIMIT, SWIGLU_LIMIT)
        act = (up + 1.0) * (gate * jax.nn.sigmoid(SWIGLU_ALPHA * gate))
        out = _dot(act.astype(BF16), wd_ref[0]) + bd_ref[0]
        for s in range(SUBLANES):
            ys_ref[pl.ds(s, RB, stride=SUBLANES), :] = out[:, s * LANES:(s + 1) * LANES]

    @pl.when(i >= nused_ref[0])
    def _():
        ys_ref[...] = jnp.zeros_like(ys_ref)


def _experts(block_expert, nused, xs, wgu, bg, bu, wd, bd, n_blocks):
    D = wgu.shape[1]
    F = wgu.shape[2] // 2
    wmap = lambda i, be, n: (be[i], 0, 0)
    return pl.pallas_call(
        _expert_kernel,
        grid_spec=pltpu.PrefetchScalarGridSpec(
            num_scalar_prefetch=2,
            grid=(n_blocks,),
            in_specs=[pl.BlockSpec((RB * SUBLANES, LANES), lambda i, be, n: (i, 0)),
                      pl.BlockSpec((1, D, 2 * F), wmap),
                      pl.BlockSpec((1, 1, F), wmap),
                      pl.BlockSpec((1, 1, F), wmap),
                      pl.BlockSpec((1, F, D), wmap),
                      pl.BlockSpec((1, 1, D), wmap)],
            out_specs=pl.BlockSpec((RB * SUBLANES, LANES), lambda i, be, n: (i, 0))),
        out_shape=jax.ShapeDtypeStruct((n_blocks * RB * SUBLANES, LANES), F32),
        compiler_params=_cparams(("arbitrary",)),
        name="experts",
    )(block_expert, nused, xs, wgu, bg, bu, wd, bd)


CB_TRIPS = 32


def _combine_kernel(pos_ref, posn_ref, ys_hbm, x1_ref, mod_ref, gate_ref, o_ref, buf_a, buf_b, sem_a, sem_b):
    s = pl.program_id(0)
    last = pl.num_programs(0) - 1
    n = TOP_K * TC
    per_trip = n // CB_TRIPS
    ts = TC // CB_TRIPS
    gate2 = mod_ref[0, 5:6, :]

    def issue_group(p_ref, buf, sem, g):
        for u in range(per_trip):
            j = g * per_trip + u
            _row_copy(ys_hbm, buf, p_ref[0, 0, j], j, sem).start(priority=u % 2)

    def sum_strip(buf, g):
        t0 = pl.multiple_of(g * ts, ts)
        gates = gate_ref[0, pl.ds(t0, ts), :]
        y = None
        for k in range(TOP_K):
            base = (k * TC + t0) * SUBLANES
            rows = jnp.concatenate(
                [buf[pl.ds(base + sl, ts, stride=SUBLANES), :] for sl in range(SUBLANES)], axis=1)
            term = gates[:, k:k + 1] * rows
            y = term if y is None else y + term
        o_ref[0, pl.ds(t0, ts), :] = x1_ref[0, pl.ds(t0, ts), :] + gate2 * y

    def step(cur_buf, cur_sem, nxt_buf, nxt_sem):
        @pl.when(s == 0)
        def _():
            def first(g, carry):
                issue_group(pos_ref, cur_buf, cur_sem, g)
                return carry

            lax.fori_loop(0, CB_TRIPS, first, 0)

        _wait_rows(ys_hbm, n, cur_sem)

        @pl.when(s < last)
        def _():
            def fused(g, carry):
                issue_group(posn_ref, nxt_buf, nxt_sem, g)
                sum_strip(cur_buf, g)
                return carry

            lax.fori_loop(0, CB_TRIPS, fused, 0)

        @pl.when(s == last)
        def _():
            def tail(g, carry):
                sum_strip(cur_buf, g)
                return carry

            lax.fori_loop(0, CB_TRIPS, tail, 0)

    @pl.when(s % 2 == 0)
    def _():
        step(buf_a, sem_a, buf_b, sem_b)

    @pl.when(s % 2 == 1)
    def _():
        step(buf_b, sem_b, buf_a, sem_a)


def _combine(pos_tiles, ys, x1, mod3, gates):
    B, S, D = x1.shape
    nt = S // TC
    n_steps = B * nt
    pos_spec = lambda off: pl.BlockSpec((1, 1, TOP_K * TC), lambda s: (jnp.minimum(s + off, n_steps - 1), 0, 0),
                                        memory_space=pltpu.SMEM)
    return pl.pallas_call(
        _combine_kernel,
        grid=(n_steps,),
        in_specs=[pos_spec(0), pos_spec(1),
                  pl.BlockSpec(memory_space=pl.ANY),
                  pl.BlockSpec((1, TC, D), lambda s: (s // nt, s % nt, 0)),
                  pl.BlockSpec((1, 6, D), lambda s: (s // nt, 0, 0)),
                  pl.BlockSpec((1, TC, TOP_K), lambda s: (s // nt, s % nt, 0))],
        out_specs=pl.BlockSpec((1, TC, D), lambda s: (s // nt, s % nt, 0)),
        out_shape=jax.ShapeDtypeStruct((B, S, D), F32),
        scratch_shapes=[pltpu.VMEM((TOP_K * TC * SUBLANES, LANES), F32),
                        pltpu.VMEM((TOP_K * TC * SUBLANES, LANES), F32),
                        pltpu.SemaphoreType.DMA(()),
                        pltpu.SemaphoreType.DMA(())],
        compiler_params=_cparams(("arbitrary",)),
        name="combine",
    )(pos_tiles, pos_tiles, ys, x1, mod3, gates)


def _lookup(table, idx):
    hit = idx[..., None] == jnp.arange(table.shape[0], dtype=jnp.int32)
    return jnp.sum(jnp.where(hit, table, 0), axis=-1).astype(jnp.int32)


def _routing_tables(top_idx, rank, counts, n_tokens):
    n_slots = n_tokens * TOP_K
    n_rows = n_slots + N_EXPERTS * RB
    n_blocks = n_rows // RB
    sizes = counts[0, :N_EXPERTS].astype(jnp.int32)
    padded = ((sizes + RB - 1) // RB) * RB
    padded_end = jnp.cumsum(padded).astype(jnp.int32)
    padded_start = padded_end - padded
    pos = _lookup(padded_start, top_idx) + rank
    n_used = padded_end[-1] // RB
    blk = jnp.arange(n_blocks, dtype=jnp.int32)
    block_expert = jnp.minimum(jnp.sum(padded_end[None, :] <= (blk * RB)[:, None], axis=1), N_EXPERTS - 1)
    last_e = jnp.minimum(jnp.sum(padded_end <= (n_used - 1) * RB), N_EXPERTS - 1)
    block_expert = jnp.where(blk < n_used, block_expert, last_e).astype(jnp.int32)
    pad_cnt = padded - sizes
    pad_end = jnp.cumsum(pad_cnt).astype(jnp.int32)
    j = jnp.arange(N_EXPERTS * RB, dtype=jnp.int32)
    e_j = jnp.sum(pad_end[None, :] <= j[:, None], axis=1).astype(jnp.int32)
    in_group = e_j < N_EXPERTS
    e_c = jnp.minimum(e_j, N_EXPERTS - 1)
    group_row = _lookup(padded_start + sizes - (pad_end - pad_cnt), e_c) + j
    tail_row = padded_end[-1] + (j - pad_end[-1])
    pad_rows = jnp.where(in_group, group_row, tail_row).astype(jnp.int32)
    return pos, pad_rows, block_expert, n_used.reshape(1).astype(jnp.int32), n_rows, n_blocks


def kernel(x, c, w_ada, b_ada, norm1_g, w_in, conv_w, conv_b, b_igate, b_fgate, mlstm_norm_g, q_norm_g, k_norm_g,
           lambda_q1, lambda_k1, lambda_q2, lambda_k2, diff_norm_g, w_out, norm2_g, w_router, b_router, w_gu, b_gu,
           w_down, b_down):
    B, S, D = x.shape
    N = B * S
    l = 0
    mq, mk, mv, mo = 256, 256, 512, 512
    o_mq, o_mk, o_mv, o_mo = 0, 256, 512, 1024
    o_mi, o_mf = 1536, 1540
    o_aq, o_ak, o_av = 1544, 2056, 2568

    wi = w_in[l]
    w_main = jnp.concatenate([wi[:, o_mq:o_mq + 512], wi[:, o_mv:o_mv + 512], wi[:, o_mo:o_mo + 512],
                              wi[:, o_aq:o_aq + 512], wi[:, o_ak:o_ak + 512], wi[:, o_av:o_av + 512]],
                             axis=1).astype(BF16)
    wg = jnp.pad(wi[:, o_mi:o_mi + 2 * M_HEADS], ((0, 0), (0, GW - 2 * M_HEADS)))
    wg_hi = wg.astype(BF16)
    wg_lo = (wg - wg_hi.astype(F32)).astype(BF16)
    pos_ids = jnp.arange(S, dtype=F32)
    inv_freq = ROPE_THETA ** (-jnp.arange(0, A_HEAD_DIM, 2, dtype=F32) / A_HEAD_DIM)
    ang = pos_ids[:, None] * inv_freq[None, :]
    cos_h, sin_h = jnp.cos(ang), jnp.sin(ang)
    cos_t = jnp.concatenate([cos_h, cos_h, cos_h, cos_h], axis=1)
    sin_t = jnp.concatenate([-sin_h, sin_h, -sin_h, sin_h], axis=1)
    qg = jnp.tile(q_norm_g[l], 512 // A_HEAD_DIM).reshape(1, 512)
    kg = jnp.tile(k_norm_g[l], 512 // A_HEAD_DIM).reshape(1, 512)
    gid = jnp.arange(512) // A_HEAD_DIM
    bd = (gid[:, None] == gid[None, :]).astype(BF16)
    bias_c = jnp.pad(jnp.concatenate([b_igate[l], b_fgate[l]]), (0, GW - 2 * M_HEADS)).reshape(1, GW)
    bias_r = jnp.concatenate([b_igate[l], b_fgate[l]]).reshape(2 * M_HEADS, 1)
    wr = jnp.pad(w_router[l], ((0, 0), (0, LANES - N_EXPERTS)))
    wr_hi = wr.astype(BF16)
    wr_lo = (wr - wr_hi.astype(F32)).astype(BF16)
    br = jnp.pad(b_router[l], (0, LANES - N_EXPERTS), constant_values=-jnp.inf).reshape(1, LANES)
    col = jnp.arange(256)
    psel = (jnp.where(col % 2 == 0, col // 2, LANES + col // 2)[:, None] == col[None, :]).astype(BF16)
    ltri = (jnp.arange(TM)[None, :] < jnp.arange(TM)[:, None]).astype(BF16)
    bgate = b_gu[l][:, None, 0::2]
    bup = b_gu[l][:, None, 1::2]
    wdn = w_down[l].astype(BF16)
    bdn = b_down[l][:, None, :]

    mod3 = _ada(c, w_ada[l], b_ada[l]).reshape(B, 6, D)
    p_all, gates_c = _inproj(x, mod3, norm1_g[l].reshape(1, D), w_main, wg_hi, wg_lo, cos_t, sin_t, qg, kg, bd)
    gates_r = gates_c[:, :, :2 * M_HEADS].reshape(B, S // CHUNK, CHUNK, 2 * M_HEADS).transpose(0, 1, 3, 2)
    hm = _mlstm(p_all, gates_c, gates_r, bias_c, bias_r, conv_w[l], conv_b[l].reshape(1, 512),
                mlstm_norm_g[l].reshape(1, 512))
    ha = _attn(p_all, lambda_q1[l].reshape(1, -1), lambda_k1[l].reshape(1, -1), lambda_q2[l].reshape(1, -1),
               lambda_k2[l].reshape(1, -1), diff_norm_g[l].reshape(1, 512))

    x1, h2_rows, top_idx, gates, rank, counts = _outproj(hm, ha, x, mod3, norm2_g[l].reshape(1, D),
                                                         w_out[l].astype(BF16), wr_hi, wr_lo, br, ltri)

    pos, pad_rows, block_expert, n_used, n_rows, n_blocks = _routing_tables(top_idx, rank, counts, N)
    xs = _dispatch(pad_rows, pos.reshape(N // TM, 1, TM * TOP_K), h2_rows, n_rows)
    wgu = _wprep(w_gu[l], psel)
    ys = _experts(block_expert, n_used, xs, wgu, bgate, bup, wdn, bdn, n_blocks)
    pos_tiles = pos.reshape(N // TC, TC, TOP_K).transpose(0, 2, 1).reshape(N // TC, 1, TOP_K * TC)
    return _combine(pos_tiles, ys, x1, mod3, gates)
```

```python
import functools
import math

import jax
import jax.numpy as jnp
from jax import lax
from jax.experimental import pallas as pl
from jax.experimental.pallas import tpu as pltpu

F32 = jnp.float32
BF16 = jnp.bfloat16
HIGHEST = lax.Precision.HIGHEST

LANES = 128
SUBLANES = 8
VMEM_LIMIT = 48 * 1024 * 1024

CHUNK = 64
M_HEADS = 4
M_QK_DIM = 64
M_V_DIM = 128
CONV_WIDTH = 4
A_HEADS = 4
A_HEAD_DIM = 64
A_V_DIM = 128
ROPE_THETA = 10000.0
N_EXPERTS = 32
TOP_K = 4
SWIGLU_ALPHA = 1.702
SWIGLU_LIMIT = 7.0
EPS = 1e-6
LAMBDA_INIT = 0.8 - 0.6 * math.exp(-0.3 * 0)

TM = 512
GCH = 8
CPT = 4
TQ = 512
RB = 512
TC = 512
GW = 128


def _dot(a, b):
    return jnp.dot(a, b, preferred_element_type=F32)


def _cparams(sem):
    return pltpu.CompilerParams(dimension_semantics=sem, vmem_limit_bytes=VMEM_LIMIT)


def _ada_kernel(c_ref, w_ref, b_ref, o_ref):
    c = c_ref[...]
    cond = c * jax.nn.sigmoid(c)
    o_ref[...] = jnp.dot(cond, w_ref[...], preferred_element_type=F32, precision=HIGHEST) + b_ref[...]


def _ada(c, w, b):
    B, D = c.shape
    n = w.shape[1]
    tn = 1024
    return pl.pallas_call(
        _ada_kernel,
        grid=(n // tn,),
        in_specs=[pl.BlockSpec((B, D), lambda j: (0, 0)),
                  pl.BlockSpec((D, tn), lambda j: (0, j)),
                  pl.BlockSpec((1, tn), lambda j: (0, j))],
        out_specs=pl.BlockSpec((B, tn), lambda j: (0, j)),
        out_shape=jax.ShapeDtypeStruct((B, n), F32),
        compiler_params=_cparams(("arbitrary",)),
        name="ada",
    )(c, w, b.reshape(1, n))


def _split_dot(hi, lo, w2, n_cols):
    both = _dot(hi, w2) + _dot(lo, w2)
    return both + pltpu.roll(both, both.shape[1] - n_cols, 1)


def _inproj_kernel(x_ref, mod_ref, g1_ref, w_ref, wg2_ref, cos_ref, sin_ref, qg_ref, kg_ref,
                   bd_ref, p_ref, gates_ref):
    x = x_ref[0]
    ms = jnp.mean(x * x, axis=-1, keepdims=True)
    shift = mod_ref[0, 0:1, :]
    scale = mod_ref[0, 1:2, :]
    h = (x * lax.rsqrt(ms + EPS) * g1_ref[...]) * (1.0 + scale) + shift
    hb = h.astype(BF16)
    hl = (h - hb.astype(F32)).astype(BF16)
    gates_ref[0] = _split_dot(hb, hl, wg2_ref[...], 2 * M_HEADS)

    tm = x.shape[0]
    lane = lax.broadcasted_iota(jnp.int32, (tm, 512), 1)
    first_half = (lane & 63) < 32
    cos = jnp.concatenate([cos_ref[...]] * 4, axis=1)
    sin = jnp.concatenate([sin_ref[...]] * 4, axis=1)
    for sec in range(6):
        acc = _dot(hb, w_ref[:, sec * 512:(sec + 1) * 512])
        if sec in (3, 4):
            g = qg_ref[...] if sec == 3 else kg_ref[...]
            ssq = _dot((acc * acc).astype(BF16), bd_ref[...])
            y = acc * lax.rsqrt(ssq * (1.0 / A_HEAD_DIM) + EPS) * g
            swapped = jnp.where(first_half, pltpu.roll(y, 512 - 32, 1), pltpu.roll(y, 32, 1))
            acc = y * cos + swapped * sin
            if sec == 3:
                acc = acc * (A_HEAD_DIM ** -0.5)
        p_ref[0, :, sec * 512:(sec + 1) * 512] = acc.astype(BF16)


def _inproj(x, mod3, g1, w_main, wg2, cos_t, sin_t, qg, kg, bd):
    B, S, D = x.shape
    nw = w_main.shape[1]
    return pl.pallas_call(
        _inproj_kernel,
        grid=(B, S // TM),
        in_specs=[pl.BlockSpec((1, TM, D), lambda b, s: (b, s, 0)),
                  pl.BlockSpec((1, 6, D), lambda b, s: (b, 0, 0)),
                  pl.BlockSpec((1, D), lambda b, s: (0, 0)),
                  pl.BlockSpec((D, nw), lambda b, s: (0, 0)),
                  pl.BlockSpec((D, GW), lambda b, s: (0, 0)),
                  pl.BlockSpec((TM, LANES), lambda b, s: (s, 0)),
                  pl.BlockSpec((TM, LANES), lambda b, s: (s, 0)),
                  pl.BlockSpec((1, 512), lambda b, s: (0, 0)),
                  pl.BlockSpec((1, 512), lambda b, s: (0, 0)),
                  pl.BlockSpec((512, 512), lambda b, s: (0, 0))],
        out_specs=[pl.BlockSpec((1, TM, nw), lambda b, s: (b, s, 0)),
                   pl.BlockSpec((1, TM, GW), lambda b, s: (b, s, 0))],
        out_shape=[jax.ShapeDtypeStruct((B, S, nw), BF16),
                   jax.ShapeDtypeStruct((B, S, GW), F32)],
        compiler_params=_cparams(("arbitrary", "arbitrary")),
        name="inproj",
    )(x, mod3, g1, w_main, wg2, cos_t, sin_t, qg, kg, bd)


def _log_sigmoid(z):
    return jnp.minimum(z, 0.0) - jnp.log1p(jnp.exp(-jnp.abs(z)))


def _mlstm_kernel(qk_ref, v_ref, o_ref, gc_ref, gr_ref, bc_ref, br_ref, cw_ref, cb_ref, ng_ref,
                  out_ref, ubuf, q_sc, k_sc, gcs, grs, cst, msc):
    g = pl.program_id(1)
    T = GCH * CHUNK
    HW = M_HEADS * M_QK_DIM

    @pl.when(g == 0)
    def _():
        cst[...] = jnp.zeros_like(cst)
        msc[...] = jnp.zeros_like(msc)
        ubuf[0:SUBLANES, :] = jnp.zeros((SUBLANES, 2 * HW), F32)

    @pl.when(g > 0)
    def _():
        ubuf[0:SUBLANES, :] = ubuf[T:T + SUBLANES, :]

    ubuf[SUBLANES:SUBLANES + T, :] = qk_ref[0].astype(F32)

    def conv_strip(c, carry):
        r0 = pl.multiple_of(c * CHUNK, CHUNK)
        strip = ubuf[pl.ds(r0, CHUNK + SUBLANES), :]
        y = cb_ref[...]
        for j in range(CONV_WIDTH):
            off = SUBLANES - (CONV_WIDTH - 1) + j
            y = y + cw_ref[j:j + 1, :] * strip[off:off + CHUNK, :]
        qk = y * jax.nn.sigmoid(y)
        q_sc[pl.ds(r0, CHUNK), :] = qk[:, :HW].astype(BF16)
        k_sc[pl.ds(r0, CHUNK), :] = (qk[:, HW:] * (M_QK_DIM ** -0.5)).astype(BF16)
        return carry

    lax.fori_loop(0, GCH, conv_strip, 0)

    gc = gc_ref[0] + bc_ref[...]
    lane = lax.broadcasted_iota(jnp.int32, gc.shape, 1)
    gcs[...] = jnp.where(lane < M_HEADS, gc, _log_sigmoid(gc))
    gr = gr_ref[0] + br_ref[...]
    row = lax.broadcasted_iota(jnp.int32, gr.shape, 1)
    grs[...] = jnp.where(row < M_HEADS, gr, _log_sigmoid(gr))

    ti = lax.broadcasted_iota(jnp.int32, (CHUNK, CHUNK), 0)
    si = lax.broadcasted_iota(jnp.int32, (CHUNK, CHUNK), 1)
    causal = si <= ti
    tri = causal.astype(F32)
    tri_t = (ti <= si).astype(F32)
    lane256 = lax.broadcasted_iota(jnp.int32, (CHUNK, HW), 1)
    ones_blk = jnp.ones((CHUNK, LANES), BF16)
    ones_sq = jnp.ones((LANES, LANES), BF16)

    def twice(a):
        return jnp.concatenate([a, a], axis=1)

    def rep(col):
        return jnp.broadcast_to(col, (CHUNK, LANES))

    def stage1(c):
        r0 = pl.multiple_of(c * CHUNK, CHUNK)
        qc = q_sc[pl.ds(r0, CHUNK), :]
        kc = k_sc[pl.ds(r0, CHUNK), :]
        vc = v_ref[0, pl.ds(r0, CHUNK), :]
        gcc = gcs[pl.ds(r0, CHUNK), :]
        grr = grs[c]
        b_c = jnp.dot(tri, gcc, preferred_element_type=F32, precision=HIGHEST)
        b_r = jnp.dot(grr, tri_t, preferred_element_type=F32, precision=HIGHEST)
        qstack = jnp.concatenate(
            [jnp.where((lane256 // M_QK_DIM) == h, qc, jnp.zeros_like(qc)) for h in range(M_HEADS)], axis=0)
        kstack = jnp.concatenate(
            [jnp.where((lane256 // M_QK_DIM) == h, kc, jnp.zeros_like(kc)) for h in range(M_HEADS)], axis=0)
        s_all = lax.dot_general(qstack, kc, (((1,), (1,)), ((), ())), preferred_element_type=F32)
        heads, vws = [], []
        for h in range(M_HEADS):
            bcol = rep(b_c[:, M_HEADS + h:M_HEADS + h + 1])
            icol = rep(gcc[:, h:h + 1])
            brow = b_r[M_HEADS + h:M_HEADS + h + 1, :]
            irow = grr[h:h + 1, :]
            b_last = bcol[CHUNK - 1:CHUNK, :]
            dlog = jnp.where(causal, bcol[:, :CHUNK] - brow + irow, -jnp.inf)
            m_intra = jnp.max(dlog, axis=-1, keepdims=True)
            vaug = jnp.concatenate([vc[:, h * M_V_DIM:(h + 1) * M_V_DIM], ones_blk], axis=1)
            a_col = b_last - bcol + icol
            m_loc = jnp.max(a_col, axis=0, keepdims=True)
            w_col = jnp.exp(a_col - m_loc)
            vws.append((vaug.astype(F32) * twice(w_col)).astype(BF16))
            heads.append((bcol, b_last, dlog, m_intra, vaug, m_loc))
        c_loc = lax.dot_general(kstack, jnp.concatenate(vws, axis=0), (((0,), (0,)), ((), ())),
                                preferred_element_type=F32)
        return r0, qstack, s_all, heads, c_loc

    def stage2(sts):
        items = []
        m_run = [msc[h:h + 1, :] for h in range(M_HEADS)]
        for r0, qstack, s_all, heads, c_loc in sts:
            i_all = _dot(qstack, cst[...].astype(BF16))
            for h in range(M_HEADS):
                bcol, b_last, dlog, m_intra, vaug, m_loc = heads[h]
                ks = slice(h * M_QK_DIM, (h + 1) * M_QK_DIM)
                m_prev = m_run[h]
                m_new = jnp.maximum(b_last + m_prev, m_loc)
                decay = jnp.exp(b_last + m_prev - m_new)
                fresh = jnp.exp(m_loc - m_new)
                cst[ks, :] = twice(decay) * cst[ks, :] + twice(fresh) * c_loc[ks, :]
                m_run[h] = m_new
                items.append((r0, h, bcol, dlog, m_intra, vaug, m_prev, s_all, i_all))
        for h in range(M_HEADS):
            msc[h:h + 1, :] = m_run[h]
        prods = []
        for r0, h, bcol, dlog, m_intra, vaug, m_prev, s_all, i_all in items:
            rs = slice(h * CHUNK, (h + 1) * CHUNK)
            m_inter = bcol + m_prev
            m_t = jnp.maximum(m_inter, m_intra)
            d_w = jnp.exp(dlog - m_t[:, :CHUNK])
            inter_w = jnp.exp(m_inter - m_t)
            p = (s_all[rs, :] * d_w).astype(BF16)
            prods.append((_dot(p, vaug) + twice(inter_w) * i_all[rs, :], m_t))
        normed = []
        for r, m_t in prods:
            hv = r[:, :M_V_DIM] / jnp.maximum(jnp.abs(r[:, M_V_DIM:]), jnp.exp(-m_t))
            hh = hv * hv
            hh_hi = hh.astype(BF16)
            hh_lo = (hh - hh_hi.astype(F32)).astype(BF16)
            normed.append((hv, _dot(hh_hi, ones_sq) + _dot(hh_lo, ones_sq)))
        for (r0, h, *_), (hv, ssq) in zip(items, normed):
            hs = slice(h * M_V_DIM, (h + 1) * M_V_DIM)
            hn = hv * lax.rsqrt(ssq * (1.0 / M_V_DIM) + EPS) * ng_ref[:, hs]
            og = o_ref[0, pl.ds(r0, CHUNK), hs].astype(F32)
            out_ref[0, pl.ds(r0, CHUNK), hs] = (hn * jax.nn.sigmoid(og)).astype(BF16)

    def chunk_group(cg, carry):
        stage2([stage1(CPT * cg + k) for k in range(CPT)])
        return carry

    lax.fori_loop(0, GCH // CPT, chunk_group, 0)


def _mlstm(p_all, gates_c, gates_r, bias_c, bias_r, conv_w, conv_b, ng):
    B, S, _ = p_all.shape
    T = GCH * CHUNK
    return pl.pallas_call(
        _mlstm_kernel,
        grid=(B, S // T),
        in_specs=[pl.BlockSpec((1, T, 512), lambda b, g: (b, g, 0)),
                  pl.BlockSpec((1, T, 512), lambda b, g: (b, g, 1)),
                  pl.BlockSpec((1, T, 512), lambda b, g: (b, g, 2)),
                  pl.BlockSpec((1, T, GW), lambda b, g: (b, g, 0)),
                  pl.BlockSpec((1, GCH, SUBLANES, CHUNK), lambda b, g: (b, g, 0, 0)),
                  pl.BlockSpec((1, GW), lambda b, g: (0, 0)),
                  pl.BlockSpec((SUBLANES, 1), lambda b, g: (0, 0)),
                  pl.BlockSpec((CONV_WIDTH, 512), lambda b, g: (0, 0)),
                  pl.BlockSpec((1, 512), lambda b, g: (0, 0)),
                  pl.BlockSpec((1, 512), lambda b, g: (0, 0))],
        out_specs=pl.BlockSpec((1, T, 512), lambda b, g: (b, g, 0)),
        out_shape=jax.ShapeDtypeStruct((B, S, 512), BF16),
        scratch_shapes=[pltpu.VMEM((T + SUBLANES, 512), F32),
                        pltpu.VMEM((T, 256), BF16),
                        pltpu.VMEM((T, 256), BF16),
                        pltpu.VMEM((T, GW), F32),
                        pltpu.VMEM((GCH, SUBLANES, CHUNK), F32),
                        pltpu.VMEM((M_HEADS * M_QK_DIM, 2 * M_V_DIM), F32),
                        pltpu.VMEM((SUBLANES, LANES), F32)],
        compiler_params=_cparams(("arbitrary", "arbitrary")),
        name="mlstm",
    )(p_all, p_all, p_all, gates_c, gates_r, bias_c, bias_r, conv_w, conv_b, ng)


def _attn_kernel(lq1_ref, lk1_ref, lq2_ref, lk2_ref, q_ref, k_ref, v_ref, ng_ref, o_ref,
                 qs_sc, m_sc, acc_sc, sa_sc, sb_sc):
    i = pl.program_id(2)
    q = q_ref[0]
    lane = lax.broadcasted_iota(jnp.int32, q.shape, 1)
    qs_sc[0:TQ, :] = jnp.where(lane < A_HEAD_DIM, q, jnp.zeros_like(q))
    qs_sc[TQ:2 * TQ, :] = jnp.where(lane >= A_HEAD_DIM, q, jnp.zeros_like(q))
    m_sc[...] = jnp.full(m_sc.shape, -jnp.inf, F32)
    acc_sc[...] = jnp.zeros_like(acc_sc)
    ones_blk = jnp.ones((TQ, LANES), BF16)

    def scores(j):
        k = k_ref[0, pl.ds(pl.multiple_of(j * TQ, TQ), TQ), :]
        return lax.dot_general(qs_sc[...], k, (((1,), (1,)), ((), ())), preferred_element_type=F32)

    def diag_scores():
        ri = lax.broadcasted_iota(jnp.int32, (2 * TQ, TQ), 0)
        ci = lax.broadcasted_iota(jnp.int32, (2 * TQ, TQ), 1)
        visible = (ci // CHUNK) <= ((ri & (TQ - 1)) // CHUNK)
        return jnp.where(visible, scores(i), -jnp.inf)

    def accumulate(s_ref, j):
        s = s_ref[...]
        v = v_ref[0, pl.ds(pl.multiple_of(j * TQ, TQ), TQ), :]
        m_old = m_sc[...]
        m_new = jnp.maximum(m_old, jnp.max(s, axis=-1, keepdims=True))
        alpha = jnp.exp(m_old - m_new)
        p = jnp.exp(s - jnp.concatenate([m_new] * (TQ // LANES), axis=1))
        pv = _dot(p.astype(BF16), jnp.concatenate([v, ones_blk], axis=1))
        acc_sc[...] = jnp.concatenate([alpha, alpha], axis=1) * acc_sc[...] + pv
        m_sc[...] = m_new

    def kv_of(t):
        return jnp.where(t == 0, i, t - 1)

    sa_sc[...] = diag_scores()
    pairs = i // 2

    def body(u, carry):
        t = 2 * u
        sb_sc[...] = scores(t)
        accumulate(sa_sc, kv_of(t))
        sa_sc[...] = scores(t + 1)
        accumulate(sb_sc, t)
        return carry

    lax.fori_loop(0, pairs, body, 0)
    t0 = 2 * pairs

    @pl.when(i == t0)
    def _():
        accumulate(sa_sc, kv_of(t0))

    @pl.when(i > t0)
    def _():
        sb_sc[...] = scores(t0)
        accumulate(sa_sc, kv_of(t0))
        accumulate(sb_sc, t0)

    acc = acc_sc[...]
    o = acc[:, :LANES] / acc[:, LANES:]
    lam = (jnp.exp(jnp.sum(lq1_ref[...] * lk1_ref[...], axis=-1, keepdims=True))
           - jnp.exp(jnp.sum(lq2_ref[...] * lk2_ref[...], axis=-1, keepdims=True)) + LAMBDA_INIT)
    a = o[0:TQ, :] - lam * o[TQ:2 * TQ, :]
    y = a * lax.rsqrt(jnp.mean(a * a, axis=-1, keepdims=True) + EPS) * ng_ref[...]
    o_ref[0] = (y * (1.0 - LAMBDA_INIT)).astype(BF16)


def _attn(p_all, lq1, lk1, lq2, lk2, ng):
    B, S, _ = p_all.shape
    nsec = 512 // LANES
    lam_spec = pl.BlockSpec((1, A_HEAD_DIM), lambda b, h, i: (0, 0))
    return pl.pallas_call(
        _attn_kernel,
        grid=(B, A_HEADS, S // TQ),
        in_specs=[lam_spec, lam_spec, lam_spec, lam_spec,
                  pl.BlockSpec((1, TQ, LANES), lambda b, h, i: (b, i, 3 * nsec + h)),
                  pl.BlockSpec((1, S, LANES), lambda b, h, i: (b, 0, 4 * nsec + h)),
                  pl.BlockSpec((1, S, LANES), lambda b, h, i: (b, 0, 5 * nsec + h)),
                  pl.BlockSpec((1, LANES), lambda b, h, i: (0, h))],
        out_specs=pl.BlockSpec((1, TQ, LANES), lambda b, h, i: (b, i, h)),
        out_shape=jax.ShapeDtypeStruct((B, S, 512), BF16),
        scratch_shapes=[pltpu.VMEM((2 * TQ, LANES), BF16),
                        pltpu.VMEM((2 * TQ, LANES), F32),
                        pltpu.VMEM((2 * TQ, 2 * LANES), F32),
                        pltpu.VMEM((2 * TQ, TQ), F32),
                        pltpu.VMEM((2 * TQ, TQ), F32)],
        compiler_params=_cparams(("arbitrary", "arbitrary", "arbitrary")),
        name="attn",
    )(lq1, lk1, lq2, lk2, p_all, p_all, p_all, ng)


def _outproj_kernel(hm_ref, ha_ref, x_ref, mod_ref, g2_ref, wo_ref, wr2_ref, br_ref, ltri_ref,
                    x1_ref, h2_ref, idx_ref, gate_ref, rank_ref, cnt_ref, cnt_sc):
    @pl.when((pl.program_id(0) == 0) & (pl.program_id(1) == 0))
    def _():
        cnt_sc[...] = jnp.zeros_like(cnt_sc)

    hcat = jnp.concatenate([hm_ref[0], ha_ref[0]], axis=1)
    mix = _dot(hcat, wo_ref[...])
    gate1 = mod_ref[0, 2:3, :]
    shift2 = mod_ref[0, 3:4, :]
    scale2 = mod_ref[0, 4:5, :]
    x1 = x_ref[0] + gate1 * mix
    x1_ref[0] = x1
    ms = jnp.mean(x1 * x1, axis=-1, keepdims=True)
    h2 = (x1 * lax.rsqrt(ms + EPS) * g2_ref[...]) * (1.0 + scale2) + shift2
    tm = h2.shape[0]
    for s in range(SUBLANES):
        h2_ref[pl.ds(s, tm, stride=SUBLANES), :] = h2[:, s * LANES:(s + 1) * LANES]
    hb = h2.astype(BF16)
    hl = (h2 - hb.astype(F32)).astype(BF16)
    logits = _split_dot(hb, hl, wr2_ref[...], N_EXPERTS) + br_ref[...]
    lane = lax.broadcasted_iota(jnp.int32, logits.shape, 1).astype(F32)
    vals, idxs = [], []
    work = logits
    for _ in range(TOP_K):
        mx = jnp.max(work, axis=-1, keepdims=True)
        ix = jnp.min(jnp.where(work == mx, lane, float(LANES)), axis=-1, keepdims=True)
        vals.append(mx)
        idxs.append(ix)
        work = jnp.where(lane == ix, -jnp.inf, work)
    es = [jnp.exp(v - vals[0]) for v in vals]
    tot = es[0] + es[1] + es[2] + es[3]
    gsel = jnp.zeros_like(logits)
    isel = jnp.zeros_like(logits)
    for k in range(TOP_K):
        gsel = jnp.where(lane == float(k), es[k] / tot, gsel)
        isel = jnp.where(lane == float(k), idxs[k], isel)
    gate_ref[0] = gsel[:, :TOP_K]
    idx_ref[0] = isel[:, :TOP_K].astype(jnp.int32)
    chosen = [lane == ix for ix in idxs]
    multi = jnp.zeros_like(logits)
    for ch in chosen:
        multi = jnp.where(ch, 1.0, multi)
    before = _dot(ltri_ref[...], multi.astype(BF16)) + cnt_sc[...]
    rsel = jnp.zeros_like(logits)
    for k in range(TOP_K):
        rk = jnp.sum(jnp.where(chosen[k], before, 0.0), axis=-1, keepdims=True)
        rsel = jnp.where(lane == float(k), rk, rsel)
    rank_ref[0] = rsel[:, :TOP_K].astype(jnp.int32)
    cnt_sc[...] = cnt_sc[...] + jnp.sum(multi, axis=0, keepdims=True)
    cnt_ref[...] = cnt_sc[...]


def _outproj(hm, ha, x, mod3, g2, wo, wr2, br, ltri):
    B, S, D = x.shape
    nt = S // TM
    return pl.pallas_call(
        _outproj_kernel,
        grid=(B, nt),
        in_specs=[pl.BlockSpec((1, TM, 512), lambda b, s: (b, s, 0)),
                  pl.BlockSpec((1, TM, 512), lambda b, s: (b, s, 0)),
                  pl.BlockSpec((1, TM, D), lambda b, s: (b, s, 0)),
                  pl.BlockSpec((1, 6, D), lambda b, s: (b, 0, 0)),
                  pl.BlockSpec((1, D), lambda b, s: (0, 0)),
                  pl.BlockSpec((D, D), lambda b, s: (0, 0)),
                  pl.BlockSpec((D, LANES), lambda b, s: (0, 0)),
                  pl.BlockSpec((1, LANES), lambda b, s: (0, 0)),
                  pl.BlockSpec((TM, TM), lambda b, s: (0, 0))],
        out_specs=[pl.BlockSpec((1, TM, D), lambda b, s: (b, s, 0)),
                   pl.BlockSpec((TM * SUBLANES, LANES), lambda b, s: (b * nt + s, 0)),
                   pl.BlockSpec((1, TM, TOP_K), lambda b, s: (b, s, 0)),
                   pl.BlockSpec((1, TM, TOP_K), lambda b, s: (b, s, 0)),
                   pl.BlockSpec((1, TM, TOP_K), lambda b, s: (b, s, 0)),
                   pl.BlockSpec((1, LANES), lambda b, s: (0, 0))],
        out_shape=[jax.ShapeDtypeStruct((B, S, D), F32),
                   jax.ShapeDtypeStruct((B * S * SUBLANES, LANES), F32),
                   jax.ShapeDtypeStruct((B, S, TOP_K), jnp.int32),
                   jax.ShapeDtypeStruct((B, S, TOP_K), F32),
                   jax.ShapeDtypeStruct((B, S, TOP_K), jnp.int32),
                   jax.ShapeDtypeStruct((1, LANES), F32)],
        scratch_shapes=[pltpu.VMEM((1, LANES), F32)],
        compiler_params=_cparams(("arbitrary", "arbitrary")),
        name="outproj",
    )(hm, ha, x, mod3, g2, wo, wr2, br, ltri)


DMA_UNROLL = 8


def _row_copy(src, dst, src_row, dst_row, sem):
    def tile_start(row):
        start = row * SUBLANES
        return start if isinstance(start, int) else pl.multiple_of(start, SUBLANES)

    return pltpu.make_async_copy(
        src.at[pl.ds(tile_start(src_row), SUBLANES), :],
        dst.at[pl.ds(tile_start(dst_row), SUBLANES), :],
        sem)


def _wait_rows(hbm, n_rows, sem):
    span = hbm.at[pl.ds(0, n_rows * SUBLANES), :]
    pltpu.make_async_copy(span, span, sem).wait()


def _dispatch_kernel(pad_ref, pos_ref, h2_ref, wgu_ref, wdn_ref, psel_ref, xs_hbm, wgu_out, wdn_out,
                     zero_sc, sem, sem_pad, *, n_pad, steps_per_expert):
    i = pl.program_id(0)
    n = TOP_K * TM

    @pl.when(i == 0)
    def _():
        zero_sc[...] = jnp.zeros_like(zero_sc)

        def issue_pad(g, carry):
            for u in range(DMA_UNROLL):
                _row_copy(zero_sc, xs_hbm, 0, pad_ref[g * DMA_UNROLL + u], sem_pad).start(priority=u % 2)
            return carry

        lax.fori_loop(0, n_pad // DMA_UNROLL, issue_pad, 0)

    def issue(g, carry):
        for u in range(DMA_UNROLL):
            j = g * DMA_UNROLL + u
            tok = g * (DMA_UNROLL // TOP_K) + u // TOP_K
            _row_copy(h2_ref, xs_hbm, tok, pos_ref[0, 0, j], sem).start(priority=u % 2)
        return carry

    lax.fori_loop(0, n // DMA_UNROLL, issue, 0)

    @pl.when(i % steps_per_expert == 0)
    def _():
        half = wgu_out.shape[2] // 2
        for cblk in range(wgu_out.shape[2] // 256):
            blk = _dot(wgu_ref[0, :, cblk * 256:(cblk + 1) * 256].astype(BF16), psel_ref[...])
            wgu_out[0, :, cblk * LANES:(cblk + 1) * LANES] = blk[:, :LANES].astype(BF16)
            wgu_out[0, :, half + cblk * LANES:half + (cblk + 1) * LANES] = blk[:, LANES:].astype(BF16)
        wdn_out[0] = wdn_ref[0].astype(BF16)

    _wait_rows(xs_hbm, n, sem)

    @pl.when(i == 0)
    def _():
        _wait_rows(xs_hbm, n_pad, sem_pad)


def _dispatch(pad_rows, pos_tiles, h2_rows, w_gu, w_down, psel, n_rows):
    n_tiles = pos_tiles.shape[0]
    n_pad = pad_rows.shape[0]
    E, D, F2 = w_gu.shape
    assert n_tiles % E == 0, "dispatch steps must be a multiple of the expert count"
    spe = n_tiles // E
    wmap = lambda i, p: (i // spe, 0, 0)
    return pl.pallas_call(
        functools.partial(_dispatch_kernel, n_pad=n_pad, steps_per_expert=spe),
        grid_spec=pltpu.PrefetchScalarGridSpec(
            num_scalar_prefetch=1,
            grid=(n_tiles,),
            in_specs=[pl.BlockSpec((1, 1, TOP_K * TM), lambda i, p: (i, 0, 0), memory_space=pltpu.SMEM),
                      pl.BlockSpec((TM * SUBLANES, LANES), lambda i, p: (i, 0)),
                      pl.BlockSpec((1, D, F2), wmap),
                      pl.BlockSpec((1, F2 // 2, D), wmap),
                      pl.BlockSpec((256, 256), lambda i, p: (0, 0))],
            out_specs=[pl.BlockSpec(memory_space=pl.ANY),
                       pl.BlockSpec((1, D, F2), wmap),
                       pl.BlockSpec((1, F2 // 2, D), wmap)],
            scratch_shapes=[pltpu.VMEM((SUBLANES, LANES), F32),
                            pltpu.SemaphoreType.DMA(()),
                            pltpu.SemaphoreType.DMA(())]),
        out_shape=[jax.ShapeDtypeStruct((n_rows * SUBLANES, LANES), F32),
                   jax.ShapeDtypeStruct((E, D, F2), BF16),
                   jax.ShapeDtypeStruct((E, F2 // 2, D), BF16)],
        compiler_params=_cparams(("arbitrary",)),
        name="dispatch",
    )(pad_rows, pos_tiles, h2_rows, w_gu, w_down, psel)


def _expert_kernel(be_ref, nused_ref, xs_ref, wgu_ref, bg_ref, bu_ref, wd_ref, bd_ref, ys_ref):
    i = pl.program_id(0)

    @pl.when(i < nused_ref[0])
    def _():
        x = jnp.concatenate([xs_ref[pl.ds(s, RB, stride=SUBLANES), :] for s in range(SUBLANES)], axis=1).astype(BF16)
        gu = _dot(x, wgu_ref[0])
        half = gu.shape[1] // 2
        gate = jnp.minimum(gu[:, :half] + bg_ref[0], SWIGLU_LIMIT)
        up = jnp.clip(gu[:, half:] + bu_ref[0], -SWIGLU_LIMIT, SWIGLU_LIMIT)
        act = (up + 1.0) * (gate * jax.nn.sigmoid(SWIGLU_ALPHA * gate))
        out = _dot(act.astype(BF16), wd_ref[0]) + bd_ref[0]
        for s in range(SUBLANES):
            ys_ref[pl.ds(s, RB, stride=SUBLANES), :] = out[:, s * LANES:(s + 1) * LANES]

    @pl.when(i >= nused_ref[0])
    def _():
        ys_ref[...] = jnp.zeros_like(ys_ref)


def _experts(block_expert, nused, xs, wgu, bg, bu, wd, bd, n_blocks):
    D = wgu.shape[1]
    F = wgu.shape[2] // 2
    wmap = lambda i, be, n: (be[i], 0, 0)
    return pl.pallas_call(
        _expert_kernel,
        grid_spec=pltpu.PrefetchScalarGridSpec(
            num_scalar_prefetch=2,
            grid=(n_blocks,),
            in_specs=[pl.BlockSpec((RB * SUBLANES, LANES), lambda i, be, n: (i, 0)),
                      pl.BlockSpec((1, D, 2 * F), wmap),
                      pl.BlockSpec((1, 1, F), wmap),
                      pl.BlockSpec((1, 1, F), wmap),
                      pl.BlockSpec((1, F, D), wmap),
                      pl.BlockSpec((1, 1, D), wmap)],
            out_specs=pl.BlockSpec((RB * SUBLANES, LANES), lambda i, be, n: (i, 0))),
        out_shape=jax.ShapeDtypeStruct((n_blocks * RB * SUBLANES, LANES), F32),
        compiler_params=_cparams(("arbitrary",)),
        name="experts",
    )(block_expert, nused, xs, wgu, bg, bu, wd, bd)


CB_TRIPS = 32


def _combine_kernel(pos_ref, posn_ref, ys_hbm, x1_ref, mod_ref, gate_ref, o_ref, buf_a, buf_b, sem_a, sem_b):
    s = pl.program_id(0)
    last = pl.num_programs(0) - 1
    n = TOP_K * TC
    per_trip = n // CB_TRIPS
    ts = TC // CB_TRIPS
    gate2 = mod_ref[0, 5:6, :]

    def issue_group(p_ref, buf, sem, g):
        for u in range(per_trip):
            j = g * per_trip + u
            _row_copy(ys_hbm, buf, p_ref[0, 0, j], j, sem).start(priority=u % 2)

    def sum_strip(buf, g):
        t0 = pl.multiple_of(g * ts, ts)
        gates = gate_ref[0, pl.ds(t0, ts), :]
        y = None
        for k in range(TOP_K):
            base = (k * TC + t0) * SUBLANES
            rows = jnp.concatenate(
                [buf[pl.ds(base + sl, ts, stride=SUBLANES), :] for sl in range(SUBLANES)], axis=1)
            term = gates[:, k:k + 1] * rows
            y = term if y is None else y + term
        o_ref[0, pl.ds(t0, ts), :] = x1_ref[0, pl.ds(t0, ts), :] + gate2 * y

    def step(cur_buf, cur_sem, nxt_buf, nxt_sem):
        @pl.when(s == 0)
        def _():
            def first(g, carry):
                issue_group(pos_ref, cur_buf, cur_sem, g)
                return carry

            lax.fori_loop(0, CB_TRIPS, first, 0)

        _wait_rows(ys_hbm, n, cur_sem)

        @pl.when(s < last)
        def _():
            def fused(g, carry):
                issue_group(posn_ref, nxt_buf, nxt_sem, g)
                sum_strip(cur_buf, g)
                return carry

            lax.fori_loop(0, CB_TRIPS, fused, 0)

        @pl.when(s == last)
        def _():
            def tail(g, carry):
                sum_strip(cur_buf, g)
                return carry

            lax.fori_loop(0, CB_TRIPS, tail, 0)

    @pl.when(s % 2 == 0)
    def _():
        step(buf_a, sem_a, buf_b, sem_b)

    @pl.when(s % 2 == 1)
    def _():
        step(buf_b, sem_b, buf_a, sem_a)


def _combine(pos_tiles, ys, x1, mod3, gates):
    B, S, D = x1.shape
    nt = S // TC
    n_steps = B * nt
    pos_spec = lambda off: pl.BlockSpec((1, 1, TOP_K * TC), lambda s: (jnp.minimum(s + off, n_steps - 1), 0, 0),
                                        memory_space=pltpu.SMEM)
    return pl.pallas_call(
        _combine_kernel,
        grid=(n_steps,),
        in_specs=[pos_spec(0), pos_spec(1),
                  pl.BlockSpec(memory_space=pl.ANY),
                  pl.BlockSpec((1, TC, D), lambda s: (s // nt, s % nt, 0)),
                  pl.BlockSpec((1, 6, D), lambda s: (s // nt, 0, 0)),
                  pl.BlockSpec((1, TC, TOP_K), lambda s: (s // nt, s % nt, 0))],
        out_specs=pl.BlockSpec((1, TC, D), lambda s: (s // nt, s % nt, 0)),
        out_shape=jax.ShapeDtypeStruct((B, S, D), F32),
        scratch_shapes=[pltpu.VMEM((TOP_K * TC * SUBLANES, LANES), F32),
                        pltpu.VMEM((TOP_K * TC * SUBLANES, LANES), F32),
                        pltpu.SemaphoreType.DMA(()),
                        pltpu.SemaphoreType.DMA(())],
        compiler_params=_cparams(("arbitrary",)),
        name="combine",
    )(pos_tiles, pos_tiles, ys, x1, mod3, gates)


def _lookup(table, idx):
    hit = idx[..., None] == jnp.arange(table.shape[0], dtype=jnp.int32)
    return jnp.sum(jnp.where(hit, table, 0), axis=-1).astype(jnp.int32)


def _routing_tables(top_idx, rank, counts, n_tokens):
    n_slots = n_tokens * TOP_K
    n_rows = n_slots + N_EXPERTS * RB
    n_blocks = n_rows // RB
    sizes = counts[0, :N_EXPERTS].astype(jnp.int32)
    padded = ((sizes + RB - 1) // RB) * RB
    padded_end = jnp.cumsum(padded).astype(jnp.int32)
    padded_start = padded_end - padded
    pos = _lookup(padded_start, top_idx) + rank
    n_used = padded_end[-1] // RB
    blk = jnp.arange(n_blocks, dtype=jnp.int32)
    block_expert = jnp.minimum(jnp.sum(padded_end[None, :] <= (blk * RB)[:, None], axis=1), N_EXPERTS - 1)
    last_e = jnp.minimum(jnp.sum(padded_end <= (n_used - 1) * RB), N_EXPERTS - 1)
    block_expert = jnp.where(blk < n_used, block_expert, last_e).astype(jnp.int32)
    pad_cnt = padded - sizes
    pad_end = jnp.cumsum(pad_cnt).astype(jnp.int32)
    j = jnp.arange(N_EXPERTS * RB, dtype=jnp.int32)
    e_j = jnp.sum(pad_end[None, :] <= j[:, None], axis=1).astype(jnp.int32)
    in_group = e_j < N_EXPERTS
    e_c = jnp.minimum(e_j, N_EXPERTS - 1)
    group_row = _lookup(padded_start + sizes - (pad_end - pad_cnt), e_c) + j
    tail_row = padded_end[-1] + (j - pad_end[-1])
    pad_rows = jnp.where(in_group, group_row, tail_row).astype(jnp.int32)
    return pos, pad_rows, block_expert, n_used.reshape(1).astype(jnp.int32), n_rows, n_blocks


def kernel(x, c, w_ada, b_ada, norm1_g, w_in, conv_w, conv_b, b_igate, b_fgate, mlstm_norm_g, q_norm_g, k_norm_g,
           lambda_q1, lambda_k1, lambda_q2, lambda_k2, diff_norm_g, w_out, norm2_g, w_router, b_router, w_gu, b_gu,
           w_down, b_down):
    B, S, D = x.shape
    N = B * S
    l = 0
    o_mq, o_mv, o_mo = 0, 512, 1024
    o_mi, o_mf = 1536, 1540
    o_aq, o_ak, o_av = 1544, 2056, 2568

    wi = w_in[l]
    w_main = jnp.concatenate([wi[:, o_mq:o_mq + 512], wi[:, o_mv:o_mv + 512], wi[:, o_mo:o_mo + 512],
                              wi[:, o_aq:o_aq + 512], wi[:, o_ak:o_ak + 512], wi[:, o_av:o_av + 512]],
                             axis=1).astype(BF16)
    def hi_lo_lanes(w, width):
        hi = w.astype(BF16)
        lo = (w - hi.astype(F32)).astype(BF16)
        return jnp.pad(jnp.concatenate([hi, lo], axis=1), ((0, 0), (0, width - 2 * w.shape[1])))

    wg2 = hi_lo_lanes(wi[:, o_mi:o_mi + 2 * M_HEADS], GW)
    pos_ids = jnp.arange(S, dtype=F32)
    inv_freq = ROPE_THETA ** (-jnp.arange(0, A_HEAD_DIM, 2, dtype=F32) / A_HEAD_DIM)
    ang = pos_ids[:, None] * inv_freq[None, :]
    cos_h, sin_h = jnp.cos(ang), jnp.sin(ang)
    cos_t = jnp.concatenate([cos_h, cos_h, cos_h, cos_h], axis=1)
    sin_t = jnp.concatenate([-sin_h, sin_h, -sin_h, sin_h], axis=1)
    qg = jnp.tile(q_norm_g[l], 512 // A_HEAD_DIM).reshape(1, 512)
    kg = jnp.tile(k_norm_g[l], 512 // A_HEAD_DIM).reshape(1, 512)
    gid = jnp.arange(512) // A_HEAD_DIM
    bd = (gid[:, None] == gid[None, :]).astype(BF16)
    bias_c = jnp.pad(jnp.concatenate([b_igate[l], b_fgate[l]]), (0, GW - 2 * M_HEADS)).reshape(1, GW)
    bias_r = jnp.concatenate([b_igate[l], b_fgate[l]]).reshape(2 * M_HEADS, 1)
    wr2 = hi_lo_lanes(w_router[l], LANES)
    br = jnp.pad(b_router[l], (0, LANES - N_EXPERTS), constant_values=-jnp.inf).reshape(1, LANES)
    col = jnp.arange(256)
    psel = (jnp.where(col % 2 == 0, col // 2, LANES + col // 2)[:, None] == col[None, :]).astype(BF16)
    ltri = (jnp.arange(TM)[None, :] < jnp.arange(TM)[:, None]).astype(BF16)
    bgate = b_gu[l][:, None, 0::2]
    bup = b_gu[l][:, None, 1::2]
    bdn = b_down[l][:, None, :]

    mod3 = _ada(c, w_ada[l], b_ada[l]).reshape(B, 6, D)
    p_all, gates_c = _inproj(x, mod3, norm1_g[l].reshape(1, D), w_main, wg2, cos_t, sin_t, qg, kg, bd)
    gates_r = gates_c[:, :, :2 * M_HEADS].reshape(B, S // CHUNK, CHUNK, 2 * M_HEADS).transpose(0, 1, 3, 2)
    hm = _mlstm(p_all, gates_c, gates_r, bias_c, bias_r, conv_w[l], conv_b[l].reshape(1, 512),
                mlstm_norm_g[l].reshape(1, 512))
    ha = _attn(p_all, lambda_q1[l].reshape(1, -1), lambda_k1[l].reshape(1, -1), lambda_q2[l].reshape(1, -1),
               lambda_k2[l].reshape(1, -1), diff_norm_g[l].reshape(1, 512))

    x1, h2_rows, top_idx, gates, rank, counts = _outproj(hm, ha, x, mod3, norm2_g[l].reshape(1, D),
                                                         w_out[l].astype(BF16), wr2, br, ltri)

    pos, pad_rows, block_expert, n_used, n_rows, n_blocks = _routing_tables(top_idx, rank, counts, N)
    xs, wgu, wdn = _dispatch(pad_rows, pos.reshape(N // TM, 1, TM * TOP_K), h2_rows, w_gu[l], w_down[l], psel, n_rows)
    ys = _experts(block_expert, n_used, xs, wgu, bgate, bup, wdn, bdn, n_blocks)
    pos_tiles = pos.reshape(N // TC, TC, TOP_K).transpose(0, 2, 1).reshape(N // TC, 1, TOP_K * TC)
    return _combine(pos_tiles, ys, x1, mod3, gates)
```

```python
import functools
import math

import jax
import jax.numpy as jnp
from jax import lax
from jax.experimental import pallas as pl
from jax.experimental.pallas import tpu as pltpu

F32 = jnp.float32
BF16 = jnp.bfloat16
HIGHEST = lax.Precision.HIGHEST

LANES = 128
SUBLANES = 8
VMEM_LIMIT = 48 * 1024 * 1024

CHUNK = 64
M_HEADS = 4
M_QK_DIM = 64
M_V_DIM = 128
CONV_WIDTH = 4
A_HEADS = 4
A_HEAD_DIM = 64
A_V_DIM = 128
ROPE_THETA = 10000.0
N_EXPERTS = 32
TOP_K = 4
SWIGLU_ALPHA = 1.702
SWIGLU_LIMIT = 7.0
EPS = 1e-6
LAMBDA_INIT = 0.8 - 0.6 * math.exp(-0.3 * 0)

TM = 512
GCH = 8
CPT = 4
TQ = 512
RB = 512
TC = 512
GW = 128


def _dot(a, b):
    return jnp.dot(a, b, preferred_element_type=F32)


def _cparams(sem):
    return pltpu.CompilerParams(dimension_semantics=sem, vmem_limit_bytes=VMEM_LIMIT)


def _ada_kernel(c_ref, w_ref, b_ref, o_ref):
    c = c_ref[...]
    cond = c * jax.nn.sigmoid(c)
    o_ref[...] = jnp.dot(cond, w_ref[...], preferred_element_type=F32, precision=HIGHEST) + b_ref[...]


def _ada(c, w, b):
    B, D = c.shape
    n = w.shape[1]
    tn = 1024
    return pl.pallas_call(
        _ada_kernel,
        grid=(n // tn,),
        in_specs=[pl.BlockSpec((B, D), lambda j: (0, 0)),
                  pl.BlockSpec((D, tn), lambda j: (0, j)),
                  pl.BlockSpec((1, tn), lambda j: (0, j))],
        out_specs=pl.BlockSpec((B, tn), lambda j: (0, j)),
        out_shape=jax.ShapeDtypeStruct((B, n), F32),
        compiler_params=_cparams(("arbitrary",)),
        name="ada",
    )(c, w, b.reshape(1, n))


def _split_dot(hi, lo, w2, n_cols):
    both = _dot(hi, w2) + _dot(lo, w2)
    return both + pltpu.roll(both, both.shape[1] - n_cols, 1)


def _inproj_kernel(x_ref, mod_ref, g1_ref, w_ref, wg2_ref, cos_ref, sin_ref, qg_ref, kg_ref,
                   bd_ref, p_ref, gates_ref):
    x = x_ref[0]
    ms = jnp.mean(x * x, axis=-1, keepdims=True)
    shift = mod_ref[0, 0:1, :]
    scale = mod_ref[0, 1:2, :]
    h = (x * lax.rsqrt(ms + EPS) * g1_ref[...]) * (1.0 + scale) + shift
    hb = h.astype(BF16)
    hl = (h - hb.astype(F32)).astype(BF16)
    gates_ref[0] = _split_dot(hb, hl, wg2_ref[...], 2 * M_HEADS)

    tm = x.shape[0]
    lane = lax.broadcasted_iota(jnp.int32, (tm, 512), 1)
    first_half = (lane & 63) < 32
    cos = jnp.concatenate([cos_ref[...]] * 4, axis=1)
    sin = jnp.concatenate([sin_ref[...]] * 4, axis=1)
    for sec in range(6):
        acc = _dot(hb, w_ref[:, sec * 512:(sec + 1) * 512])
        if sec in (3, 4):
            g = qg_ref[...] if sec == 3 else kg_ref[...]
            ssq = _dot((acc * acc).astype(BF16), bd_ref[...])
            y = acc * lax.rsqrt(ssq * (1.0 / A_HEAD_DIM) + EPS) * g
            swapped = jnp.where(first_half, pltpu.roll(y, 512 - 32, 1), pltpu.roll(y, 32, 1))
            acc = y * cos + swapped * sin
            if sec == 3:
                acc = acc * (A_HEAD_DIM ** -0.5)
        p_ref[0, :, sec * 512:(sec + 1) * 512] = acc.astype(BF16)


def _inproj(x, mod3, g1, w_main, wg2, cos_t, sin_t, qg, kg, bd):
    B, S, D = x.shape
    nw = w_main.shape[1]
    return pl.pallas_call(
        _inproj_kernel,
        grid=(B, S // TM),
        in_specs=[pl.BlockSpec((1, TM, D), lambda b, s: (b, s, 0)),
                  pl.BlockSpec((1, 6, D), lambda b, s: (b, 0, 0)),
                  pl.BlockSpec((1, D), lambda b, s: (0, 0)),
                  pl.BlockSpec((D, nw), lambda b, s: (0, 0)),
                  pl.BlockSpec((D, GW), lambda b, s: (0, 0)),
                  pl.BlockSpec((TM, LANES), lambda b, s: (s, 0)),
                  pl.BlockSpec((TM, LANES), lambda b, s: (s, 0)),
                  pl.BlockSpec((1, 512), lambda b, s: (0, 0)),
                  pl.BlockSpec((1, 512), lambda b, s: (0, 0)),
                  pl.BlockSpec((512, 512), lambda b, s: (0, 0))],
        out_specs=[pl.BlockSpec((1, TM, nw), lambda b, s: (b, s, 0)),
                   pl.BlockSpec((1, TM, GW), lambda b, s: (b, s, 0))],
        out_shape=[jax.ShapeDtypeStruct((B, S, nw), BF16),
                   jax.ShapeDtypeStruct((B, S, GW), F32)],
        compiler_params=_cparams(("arbitrary", "arbitrary")),
        name="inproj",
    )(x, mod3, g1, w_main, wg2, cos_t, sin_t, qg, kg, bd)


def _log_sigmoid(z):
    return jnp.minimum(z, 0.0) - jnp.log1p(jnp.exp(-jnp.abs(z)))


def _mlstm_kernel(qk_ref, v_ref, o_ref, gc_ref, gr_ref, bc_ref, br_ref, cw_ref, cb_ref, ng_ref,
                  out_ref, ubuf, q_sc, k_sc, gcs, grs, cst, msc):
    g = pl.program_id(1)
    T = GCH * CHUNK
    HW = M_HEADS * M_QK_DIM

    @pl.when(g == 0)
    def _():
        cst[...] = jnp.zeros_like(cst)
        msc[...] = jnp.zeros_like(msc)
        ubuf[0:SUBLANES, :] = jnp.zeros((SUBLANES, 2 * HW), F32)

    @pl.when(g > 0)
    def _():
        ubuf[0:SUBLANES, :] = ubuf[T:T + SUBLANES, :]

    ubuf[SUBLANES:SUBLANES + T, :] = qk_ref[0].astype(F32)

    def conv_strip(c, carry):
        r0 = pl.multiple_of(c * CHUNK, CHUNK)
        strip = ubuf[pl.ds(r0, CHUNK + SUBLANES), :]
        y = cb_ref[...]
        for j in range(CONV_WIDTH):
            off = SUBLANES - (CONV_WIDTH - 1) + j
            y = y + cw_ref[j:j + 1, :] * strip[off:off + CHUNK, :]
        qk = y * jax.nn.sigmoid(y)
        q_sc[pl.ds(r0, CHUNK), :] = qk[:, :HW].astype(BF16)
        k_sc[pl.ds(r0, CHUNK), :] = (qk[:, HW:] * (M_QK_DIM ** -0.5)).astype(BF16)
        return carry

    lax.fori_loop(0, GCH, conv_strip, 0)

    gc = gc_ref[0] + bc_ref[...]
    lane = lax.broadcasted_iota(jnp.int32, gc.shape, 1)
    gcs[...] = jnp.where(lane < M_HEADS, gc, _log_sigmoid(gc))
    gr = gr_ref[0] + br_ref[...]
    row = lax.broadcasted_iota(jnp.int32, gr.shape, 1)
    grs[...] = jnp.where(row < M_HEADS, gr, _log_sigmoid(gr))

    ti = lax.broadcasted_iota(jnp.int32, (CHUNK, CHUNK), 0)
    si = lax.broadcasted_iota(jnp.int32, (CHUNK, CHUNK), 1)
    causal = si <= ti
    tri = causal.astype(F32)
    tri_t = (ti <= si).astype(F32)
    lane256 = lax.broadcasted_iota(jnp.int32, (CHUNK, HW), 1)
    ones_blk = jnp.ones((CHUNK, LANES), BF16)
    ones_sq = jnp.ones((LANES, LANES), BF16)

    def twice(a):
        return jnp.concatenate([a, a], axis=1)

    def rep(col):
        return jnp.broadcast_to(col, (CHUNK, LANES))

    def stage1(c):
        r0 = pl.multiple_of(c * CHUNK, CHUNK)
        qc = q_sc[pl.ds(r0, CHUNK), :]
        kc = k_sc[pl.ds(r0, CHUNK), :]
        vc = v_ref[0, pl.ds(r0, CHUNK), :]
        gcc = gcs[pl.ds(r0, CHUNK), :]
        grr = grs[c]
        b_c = jnp.dot(tri, gcc, preferred_element_type=F32, precision=HIGHEST)
        b_r = jnp.dot(grr, tri_t, preferred_element_type=F32, precision=HIGHEST)
        qstack = jnp.concatenate(
            [jnp.where((lane256 // M_QK_DIM) == h, qc, jnp.zeros_like(qc)) for h in range(M_HEADS)], axis=0)
        kstack = jnp.concatenate(
            [jnp.where((lane256 // M_QK_DIM) == h, kc, jnp.zeros_like(kc)) for h in range(M_HEADS)], axis=0)
        s_all = lax.dot_general(qstack, kc, (((1,), (1,)), ((), ())), preferred_element_type=F32)
        heads, vws = [], []
        for h in range(M_HEADS):
            bcol = rep(b_c[:, M_HEADS + h:M_HEADS + h + 1])
            icol = rep(gcc[:, h:h + 1])
            brow = b_r[M_HEADS + h:M_HEADS + h + 1, :]
            irow = grr[h:h + 1, :]
            b_last = bcol[CHUNK - 1:CHUNK, :]
            dlog = jnp.where(causal, bcol[:, :CHUNK] - brow + irow, -jnp.inf)
            m_intra = jnp.max(dlog, axis=-1, keepdims=True)
            vaug = jnp.concatenate([vc[:, h * M_V_DIM:(h + 1) * M_V_DIM], ones_blk], axis=1)
            a_col = b_last - bcol + icol
            m_loc = jnp.max(a_col, axis=0, keepdims=True)
            w_col = jnp.exp(a_col - m_loc)
            vws.append((vaug.astype(F32) * twice(w_col)).astype(BF16))
            heads.append((bcol, b_last, dlog, m_intra, vaug, m_loc))
        c_loc = lax.dot_general(kstack, jnp.concatenate(vws, axis=0), (((0,), (0,)), ((), ())),
                                preferred_element_type=F32)
        return r0, qstack, s_all, heads, c_loc

    def stage2(sts):
        items = []
        m_run = [msc[h:h + 1, :] for h in range(M_HEADS)]
        for r0, qstack, s_all, heads, c_loc in sts:
            i_all = _dot(qstack, cst[...].astype(BF16))
            for h in range(M_HEADS):
                bcol, b_last, dlog, m_intra, vaug, m_loc = heads[h]
                ks = slice(h * M_QK_DIM, (h + 1) * M_QK_DIM)
                m_prev = m_run[h]
                m_new = jnp.maximum(b_last + m_prev, m_loc)
                decay = jnp.exp(b_last + m_prev - m_new)
                fresh = jnp.exp(m_loc - m_new)
                cst[ks, :] = twice(decay) * cst[ks, :] + twice(fresh) * c_loc[ks, :]
                m_run[h] = m_new
                items.append((r0, h, bcol, dlog, m_intra, vaug, m_prev, s_all, i_all))
        for h in range(M_HEADS):
            msc[h:h + 1, :] = m_run[h]
        prods = []
        for r0, h, bcol, dlog, m_intra, vaug, m_prev, s_all, i_all in items:
            rs = slice(h * CHUNK, (h + 1) * CHUNK)
            m_inter = bcol + m_prev
            m_t = jnp.maximum(m_inter, m_intra)
            d_w = jnp.exp(dlog - m_t[:, :CHUNK])
            inter_w = jnp.exp(m_inter - m_t)
            p = (s_all[rs, :] * d_w).astype(BF16)
            prods.append((_dot(p, vaug) + twice(inter_w) * i_all[rs, :], m_t))
        normed = []
        for r, m_t in prods:
            hv = r[:, :M_V_DIM] / jnp.maximum(jnp.abs(r[:, M_V_DIM:]), jnp.exp(-m_t))
            hh = hv * hv
            hh_hi = hh.astype(BF16)
            hh_lo = (hh - hh_hi.astype(F32)).astype(BF16)
            normed.append((hv, _dot(hh_hi, ones_sq) + _dot(hh_lo, ones_sq)))
        for (r0, h, *_), (hv, ssq) in zip(items, normed):
            hs = slice(h * M_V_DIM, (h + 1) * M_V_DIM)
            hn = hv * lax.rsqrt(ssq * (1.0 / M_V_DIM) + EPS) * ng_ref[:, hs]
            og = o_ref[0, pl.ds(r0, CHUNK), hs].astype(F32)
            out_ref[0, pl.ds(r0, CHUNK), hs] = (hn * jax.nn.sigmoid(og)).astype(BF16)

    def chunk_group(cg, carry):
        stage2([stage1(CPT * cg + k) for k in range(CPT)])
        return carry

    lax.fori_loop(0, GCH // CPT, chunk_group, 0)


def _mlstm(p_all, gates_c, gates_r, bias_c, bias_r, conv_w, conv_b, ng):
    B, S, _ = p_all.shape
    T = GCH * CHUNK
    return pl.pallas_call(
        _mlstm_kernel,
        grid=(B, S // T),
        in_specs=[pl.BlockSpec((1, T, 512), lambda b, g: (b, g, 0)),
                  pl.BlockSpec((1, T, 512), lambda b, g: (b, g, 1)),
                  pl.BlockSpec((1, T, 512), lambda b, g: (b, g, 2)),
                  pl.BlockSpec((1, T, GW), lambda b, g: (b, g, 0)),
                  pl.BlockSpec((1, GCH, SUBLANES, CHUNK), lambda b, g: (b, g, 0, 0)),
                  pl.BlockSpec((1, GW), lambda b, g: (0, 0)),
                  pl.BlockSpec((SUBLANES, 1), lambda b, g: (0, 0)),
                  pl.BlockSpec((CONV_WIDTH, 512), lambda b, g: (0, 0)),
                  pl.BlockSpec((1, 512), lambda b, g: (0, 0)),
                  pl.BlockSpec((1, 512), lambda b, g: (0, 0))],
        out_specs=pl.BlockSpec((1, T, 512), lambda b, g: (b, g, 0)),
        out_shape=jax.ShapeDtypeStruct((B, S, 512), BF16),
        scratch_shapes=[pltpu.VMEM((T + SUBLANES, 512), F32),
                        pltpu.VMEM((T, 256), BF16),
                        pltpu.VMEM((T, 256), BF16),
                        pltpu.VMEM((T, GW), F32),
                        pltpu.VMEM((GCH, SUBLANES, CHUNK), F32),
                        pltpu.VMEM((M_HEADS * M_QK_DIM, 2 * M_V_DIM), F32),
                        pltpu.VMEM((SUBLANES, LANES), F32)],
        compiler_params=_cparams(("arbitrary", "arbitrary")),
        name="mlstm",
    )(p_all, p_all, p_all, gates_c, gates_r, bias_c, bias_r, conv_w, conv_b, ng)


HPS = 2


def _attn_kernel(lq1_ref, lk1_ref, lq2_ref, lk2_ref, q_ref, k_ref, v_ref, ng_ref, o_ref, *scratch):
    i = pl.program_id(2)
    qs_scs, m_scs, acc_scs, sa_scs, sb_scs = (scratch[n * HPS:(n + 1) * HPS] for n in range(5))
    heads = range(HPS)
    lanes_of = [slice(hh * LANES, (hh + 1) * LANES) for hh in heads]
    ones_blk = jnp.ones((TQ, LANES), BF16)

    for hh in heads:
        q = q_ref[0, :, lanes_of[hh]]
        lane = lax.broadcasted_iota(jnp.int32, q.shape, 1)
        qs_scs[hh][0:TQ, :] = jnp.where(lane < A_HEAD_DIM, q, jnp.zeros_like(q))
        qs_scs[hh][TQ:2 * TQ, :] = jnp.where(lane >= A_HEAD_DIM, q, jnp.zeros_like(q))
        m_scs[hh][...] = jnp.full(m_scs[hh].shape, -jnp.inf, F32)
        acc_scs[hh][...] = jnp.zeros_like(acc_scs[hh])

    def scores(hh, j):
        k = k_ref[0, pl.ds(pl.multiple_of(j * TQ, TQ), TQ), lanes_of[hh]]
        return lax.dot_general(qs_scs[hh][...], k, (((1,), (1,)), ((), ())), preferred_element_type=F32)

    def diag_scores(hh):
        ri = lax.broadcasted_iota(jnp.int32, (2 * TQ, TQ), 0)
        ci = lax.broadcasted_iota(jnp.int32, (2 * TQ, TQ), 1)
        visible = (ci // CHUNK) <= ((ri & (TQ - 1)) // CHUNK)
        return jnp.where(visible, scores(hh, i), -jnp.inf)

    def accumulate(hh, s_ref, j):
        s = s_ref[...]
        v = v_ref[0, pl.ds(pl.multiple_of(j * TQ, TQ), TQ), lanes_of[hh]]
        m_old = m_scs[hh][...]
        m_new = jnp.maximum(m_old, jnp.max(s, axis=-1, keepdims=True))
        alpha = jnp.exp(m_old - m_new)
        p = jnp.exp(s - jnp.concatenate([m_new] * (TQ // LANES), axis=1))
        pv = _dot(p.astype(BF16), jnp.concatenate([v, ones_blk], axis=1))
        acc_scs[hh][...] = jnp.concatenate([alpha, alpha], axis=1) * acc_scs[hh][...] + pv
        m_scs[hh][...] = m_new

    def kv_of(t):
        return jnp.where(t == 0, i, t - 1)

    for hh in heads:
        sa_scs[hh][...] = diag_scores(hh)
    pairs = i // 2

    def body(u, carry):
        t = 2 * u
        for hh in heads:
            sb_scs[hh][...] = scores(hh, t)
            accumulate(hh, sa_scs[hh], kv_of(t))
        for hh in heads:
            sa_scs[hh][...] = scores(hh, t + 1)
            accumulate(hh, sb_scs[hh], t)
        return carry

    lax.fori_loop(0, pairs, body, 0)
    t0 = 2 * pairs

    @pl.when(i == t0)
    def _():
        for hh in heads:
            accumulate(hh, sa_scs[hh], kv_of(t0))

    @pl.when(i > t0)
    def _():
        for hh in heads:
            sb_scs[hh][...] = scores(hh, t0)
            accumulate(hh, sa_scs[hh], kv_of(t0))
        for hh in heads:
            accumulate(hh, sb_scs[hh], t0)

    lam = (jnp.exp(jnp.sum(lq1_ref[...] * lk1_ref[...], axis=-1, keepdims=True))
           - jnp.exp(jnp.sum(lq2_ref[...] * lk2_ref[...], axis=-1, keepdims=True)) + LAMBDA_INIT)
    for hh in heads:
        acc = acc_scs[hh][...]
        o = acc[:, :LANES] / acc[:, LANES:]
        a = o[0:TQ, :] - lam * o[TQ:2 * TQ, :]
        y = a * lax.rsqrt(jnp.mean(a * a, axis=-1, keepdims=True) + EPS) * ng_ref[:, lanes_of[hh]]
        o_ref[0, :, lanes_of[hh]] = (y * (1.0 - LAMBDA_INIT)).astype(BF16)


def _attn(p_all, lq1, lk1, lq2, lk2, ng):
    B, S, _ = p_all.shape
    hw = HPS * LANES
    nsec = 512 // hw
    lam_spec = pl.BlockSpec((1, A_HEAD_DIM), lambda b, h, i: (0, 0))
    per_head = lambda shape, dtype: [pltpu.VMEM(shape, dtype) for _ in range(HPS)]
    return pl.pallas_call(
        _attn_kernel,
        grid=(B, A_HEADS // HPS, S // TQ),
        in_specs=[lam_spec, lam_spec, lam_spec, lam_spec,
                  pl.BlockSpec((1, TQ, hw), lambda b, h, i: (b, i, 3 * nsec + h)),
                  pl.BlockSpec((1, S, hw), lambda b, h, i: (b, 0, 4 * nsec + h)),
                  pl.BlockSpec((1, S, hw), lambda b, h, i: (b, 0, 5 * nsec + h)),
                  pl.BlockSpec((1, hw), lambda b, h, i: (0, h))],
        out_specs=pl.BlockSpec((1, TQ, hw), lambda b, h, i: (b, i, h)),
        out_shape=jax.ShapeDtypeStruct((B, S, 512), BF16),
        scratch_shapes=(per_head((2 * TQ, LANES), BF16) + per_head((2 * TQ, LANES), F32)
                        + per_head((2 * TQ, 2 * LANES), F32) + per_head((2 * TQ, TQ), F32)
                        + per_head((2 * TQ, TQ), F32)),
        compiler_params=_cparams(("arbitrary", "arbitrary", "arbitrary")),
        name="attn",
    )(lq1, lk1, lq2, lk2, p_all, p_all, p_all, ng)


def _outproj_kernel(hm_ref, ha_ref, x_ref, mod_ref, g2_ref, wo_ref, wrh_ref, wrl_ref, br_ref, ltri_ref,
                    x1_ref, h2_ref, idx_ref, gate_ref, rank_ref, cnt_ref, cnt_sc):
    @pl.when((pl.program_id(0) == 0) & (pl.program_id(1) == 0))
    def _():
        cnt_sc[...] = jnp.zeros_like(cnt_sc)

    hcat = jnp.concatenate([hm_ref[0], ha_ref[0]], axis=1)
    mix = _dot(hcat, wo_ref[...])
    gate1 = mod_ref[0, 2:3, :]
    shift2 = mod_ref[0, 3:4, :]
    scale2 = mod_ref[0, 4:5, :]
    x1 = x_ref[0] + gate1 * mix
    x1_ref[0] = x1
    ms = jnp.mean(x1 * x1, axis=-1, keepdims=True)
    h2 = (x1 * lax.rsqrt(ms + EPS) * g2_ref[...]) * (1.0 + scale2) + shift2
    tm = h2.shape[0]
    for s in range(SUBLANES):
        h2_ref[pl.ds(s, tm, stride=SUBLANES), :] = h2[:, s * LANES:(s + 1) * LANES]
    hb = h2.astype(BF16)
    hl = (h2 - hb.astype(F32)).astype(BF16)
    logits = _dot(hb, wrh_ref[...]) + _dot(hl, wrh_ref[...]) + _dot(hb, wrl_ref[...]) + br_ref[...]
    lane = lax.broadcasted_iota(jnp.int32, logits.shape, 1).astype(F32)
    vals, idxs = [], []
    work = logits
    for _ in range(TOP_K):
        mx = jnp.max(work, axis=-1, keepdims=True)
        ix = jnp.min(jnp.where(work == mx, lane, float(LANES)), axis=-1, keepdims=True)
        vals.append(mx)
        idxs.append(ix)
        work = jnp.where(lane == ix, -jnp.inf, work)
    es = [jnp.exp(v - vals[0]) for v in vals]
    tot = es[0] + es[1] + es[2] + es[3]
    gsel = jnp.zeros_like(logits)
    isel = jnp.zeros_like(logits)
    for k in range(TOP_K):
        gsel = jnp.where(lane == float(k), es[k] / tot, gsel)
        isel = jnp.where(lane == float(k), idxs[k], isel)
    gate_ref[0] = gsel[:, :TOP_K]
    idx_ref[0] = isel[:, :TOP_K].astype(jnp.int32)
    chosen = [lane == ix for ix in idxs]
    multi = jnp.zeros_like(logits)
    for ch in chosen:
        multi = jnp.where(ch, 1.0, multi)
    before = _dot(ltri_ref[...], multi.astype(BF16)) + cnt_sc[...]
    rsel = jnp.zeros_like(logits)
    for k in range(TOP_K):
        rk = jnp.sum(jnp.where(chosen[k], before, 0.0), axis=-1, keepdims=True)
        rsel = jnp.where(lane == float(k), rk, rsel)
    rank_ref[0] = rsel[:, :TOP_K].astype(jnp.int32)
    cnt_sc[...] = cnt_sc[...] + jnp.sum(multi, axis=0, keepdims=True)
    cnt_ref[...] = cnt_sc[...]


def _outproj(hm, ha, x, mod3, g2, wo, wr_hi, wr_lo, br, ltri):
    B, S, D = x.shape
    nt = S // TM
    return pl.pallas_call(
        _outproj_kernel,
        grid=(B, nt),
        in_specs=[pl.BlockSpec((1, TM, 512), lambda b, s: (b, s, 0)),
                  pl.BlockSpec((1, TM, 512), lambda b, s: (b, s, 0)),
                  pl.BlockSpec((1, TM, D), lambda b, s: (b, s, 0)),
                  pl.BlockSpec((1, 6, D), lambda b, s: (b, 0, 0)),
                  pl.BlockSpec((1, D), lambda b, s: (0, 0)),
                  pl.BlockSpec((D, D), lambda b, s: (0, 0)),
                  pl.BlockSpec((D, LANES), lambda b, s: (0, 0)),
                  pl.BlockSpec((D, LANES), lambda b, s: (0, 0)),
                  pl.BlockSpec((1, LANES), lambda b, s: (0, 0)),
                  pl.BlockSpec((TM, TM), lambda b, s: (0, 0))],
        out_specs=[pl.BlockSpec((1, TM, D), lambda b, s: (b, s, 0)),
                   pl.BlockSpec((TM * SUBLANES, LANES), lambda b, s: (b * nt + s, 0)),
                   pl.BlockSpec((1, TM, TOP_K), lambda b, s: (b, s, 0)),
                   pl.BlockSpec((1, TM, TOP_K), lambda b, s: (b, s, 0)),
                   pl.BlockSpec((1, TM, TOP_K), lambda b, s: (b, s, 0)),
                   pl.BlockSpec((1, LANES), lambda b, s: (0, 0))],
        out_shape=[jax.ShapeDtypeStruct((B, S, D), F32),
                   jax.ShapeDtypeStruct((B * S * SUBLANES, LANES), F32),
                   jax.ShapeDtypeStruct((B, S, TOP_K), jnp.int32),
                   jax.ShapeDtypeStruct((B, S, TOP_K), F32),
                   jax.ShapeDtypeStruct((B, S, TOP_K), jnp.int32),
                   jax.ShapeDtypeStruct((1, LANES), F32)],
        scratch_shapes=[pltpu.VMEM((1, LANES), F32)],
        compiler_params=_cparams(("arbitrary", "arbitrary")),
        name="outproj",
    )(hm, ha, x, mod3, g2, wo, wr_hi, wr_lo, br, ltri)


DMA_UNROLL = 8


def _row_copy(src, dst, src_row, dst_row, sem):
    def tile_start(row):
        start = row * SUBLANES
        return start if isinstance(start, int) else pl.multiple_of(start, SUBLANES)

    return pltpu.make_async_copy(
        src.at[pl.ds(tile_start(src_row), SUBLANES), :],
        dst.at[pl.ds(tile_start(dst_row), SUBLANES), :],
        sem)


def _wait_rows(hbm, n_rows, sem):
    span = hbm.at[pl.ds(0, n_rows * SUBLANES), :]
    pltpu.make_async_copy(span, span, sem).wait()


def _dispatch_kernel(pad_ref, pos_ref, h2_ref, wgu_ref, wdn_ref, psel_ref, xs_hbm, wgu_out, wdn_out,
                     zero_sc, sem, sem_pad, *, n_pad, steps_per_expert):
    i = pl.program_id(0)
    n = TOP_K * TM

    @pl.when(i == 0)
    def _():
        zero_sc[...] = jnp.zeros_like(zero_sc)

        def issue_pad(g, carry):
            for u in range(DMA_UNROLL):
                _row_copy(zero_sc, xs_hbm, 0, pad_ref[g * DMA_UNROLL + u], sem_pad).start(priority=u % 2)
            return carry

        lax.fori_loop(0, n_pad // DMA_UNROLL, issue_pad, 0)

    def issue(g, carry):
        for u in range(DMA_UNROLL):
            j = g * DMA_UNROLL + u
            tok = g * (DMA_UNROLL // TOP_K) + u // TOP_K
            _row_copy(h2_ref, xs_hbm, tok, pos_ref[0, 0, j], sem).start(priority=u % 2)
        return carry

    lax.fori_loop(0, n // DMA_UNROLL, issue, 0)

    @pl.when(i % steps_per_expert == 0)
    def _():
        half = wgu_out.shape[2] // 2
        for cblk in range(wgu_out.shape[2] // 256):
            blk = _dot(wgu_ref[0, :, cblk * 256:(cblk + 1) * 256].astype(BF16), psel_ref[...])
            wgu_out[0, :, cblk * LANES:(cblk + 1) * LANES] = blk[:, :LANES].astype(BF16)
            wgu_out[0, :, half + cblk * LANES:half + (cblk + 1) * LANES] = blk[:, LANES:].astype(BF16)
        wdn_out[0] = wdn_ref[0].astype(BF16)

    _wait_rows(xs_hbm, n, sem)

    @pl.when(i == 0)
    def _():
        _wait_rows(xs_hbm, n_pad, sem_pad)


def _dispatch(pad_rows, pos_tiles, h2_rows, w_gu, w_down, psel, n_rows):
    n_tiles = pos_tiles.shape[0]
    n_pad = pad_rows.shape[0]
    E, D, F2 = w_gu.shape
    assert n_tiles % E == 0, "dispatch steps must be a multiple of the expert count"
    spe = n_tiles // E
    wmap = lambda i, p: (i // spe, 0, 0)
    return pl.pallas_call(
        functools.partial(_dispatch_kernel, n_pad=n_pad, steps_per_expert=spe),
        grid_spec=pltpu.PrefetchScalarGridSpec(
            num_scalar_prefetch=1,
            grid=(n_tiles,),
            in_specs=[pl.BlockSpec((1, 1, TOP_K * TM), lambda i, p: (i, 0, 0), memory_space=pltpu.SMEM),
                      pl.BlockSpec((TM * SUBLANES, LANES), lambda i, p: (i, 0)),
                      pl.BlockSpec((1, D, F2), wmap),
                      pl.BlockSpec((1, F2 // 2, D), wmap),
                      pl.BlockSpec((256, 256), lambda i, p: (0, 0))],
            out_specs=[pl.BlockSpec(memory_space=pl.ANY),
                       pl.BlockSpec((1, D, F2), wmap),
                       pl.BlockSpec((1, F2 // 2, D), wmap)],
            scratch_shapes=[pltpu.VMEM((SUBLANES, LANES), F32),
                            pltpu.SemaphoreType.DMA(()),
                            pltpu.SemaphoreType.DMA(())]),
        out_shape=[jax.ShapeDtypeStruct((n_rows * SUBLANES, LANES), F32),
                   jax.ShapeDtypeStruct((E, D, F2), BF16),
                   jax.ShapeDtypeStruct((E, F2 // 2, D), BF16)],
        compiler_params=_cparams(("arbitrary",)),
        name="dispatch",
    )(pad_rows, pos_tiles, h2_rows, w_gu, w_down, psel)


def _expert_kernel(be_ref, nused_ref, xs_ref, wgu_ref, bg_ref, bu_ref, wd_ref, bd_ref, ys_ref):
    i = pl.program_id(0)

    @pl.when(i < nused_ref[0])
    def _():
        x = jnp.concatenate([xs_ref[pl.ds(s, RB, stride=SUBLANES), :] for s in range(SUBLANES)], axis=1).astype(BF16)
        gu = _dot(x, wgu_ref[0])
        half = gu.shape[1] // 2
        gate = jnp.minimum(gu[:, :half] + bg_ref[0], SWIGLU_LIMIT)
        up = jnp.clip(gu[:, half:] + bu_ref[0], -SWIGLU_LIMIT, SWIGLU_LIMIT)
        act = (up + 1.0) * (gate * jax.nn.sigmoid(SWIGLU_ALPHA * gate))
        out = _dot(act.astype(BF16), wd_ref[0]) + bd_ref[0]
        for s in range(SUBLANES):
            ys_ref[pl.ds(s, RB, stride=SUBLANES), :] = out[:, s * LANES:(s + 1) * LANES]

    @pl.when(i >= nused_ref[0])
    def _():
        ys_ref[...] = jnp.zeros_like(ys_ref)


def _experts(block_expert, nused, xs, wgu, bg, bu, wd, bd, n_blocks):
    D = wgu.shape[1]
    F = wgu.shape[2] // 2
    wmap = lambda i, be, n: (be[i], 0, 0)
    return pl.pallas_call(
        _expert_kernel,
        grid_spec=pltpu.PrefetchScalarGridSpec(
            num_scalar_prefetch=2,
            grid=(n_blocks,),
            in_specs=[pl.BlockSpec((RB * SUBLANES, LANES), lambda i, be, n: (i, 0)),
                      pl.BlockSpec((1, D, 2 * F), wmap),
                      pl.BlockSpec((1, 1, F), wmap),
                      pl.BlockSpec((1, 1, F), wmap),
                      pl.BlockSpec((1, F, D), wmap),
                      pl.BlockSpec((1, 1, D), wmap)],
            out_specs=pl.BlockSpec((RB * SUBLANES, LANES), lambda i, be, n: (i, 0))),
        out_shape=jax.ShapeDtypeStruct((n_blocks * RB * SUBLANES, LANES), F32),
        compiler_params=_cparams(("arbitrary",)),
        name="experts",
    )(block_expert, nused, xs, wgu, bg, bu, wd, bd)


CB_TRIPS = 32


def _combine_kernel(pos_ref, posn_ref, ys_hbm, x1_ref, mod_ref, gate_ref, o_ref, buf_a, buf_b, sem_a, sem_b):
    s = pl.program_id(0)
    last = pl.num_programs(0) - 1
    n = TOP_K * TC
    per_trip = n // CB_TRIPS
    ts = TC // CB_TRIPS
    gate2 = mod_ref[0, 5:6, :]

    def issue_group(p_ref, buf, sem, g):
        for u in range(per_trip):
            j = g * per_trip + u
            _row_copy(ys_hbm, buf, p_ref[0, 0, j], j, sem).start(priority=u % 2)

    def sum_strip(buf, g):
        t0 = pl.multiple_of(g * ts, ts)
        gates = gate_ref[0, pl.ds(t0, ts), :]
        y = None
        for k in range(TOP_K):
            base = (k * TC + t0) * SUBLANES
            rows = jnp.concatenate(
                [buf[pl.ds(base + sl, ts, stride=SUBLANES), :] for sl in range(SUBLANES)], axis=1)
            term = gates[:, k:k + 1] * rows
            y = term if y is None else y + term
        o_ref[0, pl.ds(t0, ts), :] = x1_ref[0, pl.ds(t0, ts), :] + gate2 * y

    def step(cur_buf, cur_sem, nxt_buf, nxt_sem):
        @pl.when(s == 0)
        def _():
            def first(g, carry):
                issue_group(pos_ref, cur_buf, cur_sem, g)
                return carry

            lax.fori_loop(0, CB_TRIPS, first, 0)

        _wait_rows(ys_hbm, n, cur_sem)

        @pl.when(s < last)
        def _():
            def fused(g, carry):
                issue_group(posn_ref, nxt_buf, nxt_sem, g)
                sum_strip(cur_buf, g)
                return carry

            lax.fori_loop(0, CB_TRIPS, fused, 0)

        @pl.when(s == last)
        def _():
            def tail(g, carry):
                sum_strip(cur_buf, g)
                return carry

            lax.fori_loop(0, CB_TRIPS, tail, 0)

    @pl.when(s % 2 == 0)
    def _():
        step(buf_a, sem_a, buf_b, sem_b)

    @pl.when(s % 2 == 1)
    def _():
        step(buf_b, sem_b, buf_a, sem_a)


def _combine(pos_tiles, ys, x1, mod3, gates):
    B, S, D = x1.shape
    nt = S // TC
    n_steps = B * nt
    pos_spec = lambda off: pl.BlockSpec((1, 1, TOP_K * TC), lambda s: (jnp.minimum(s + off, n_steps - 1), 0, 0),
                                        memory_space=pltpu.SMEM)
    return pl.pallas_call(
        _combine_kernel,
        grid=(n_steps,),
        in_specs=[pos_spec(0), pos_spec(1),
                  pl.BlockSpec(memory_space=pl.ANY),
                  pl.BlockSpec((1, TC, D), lambda s: (s // nt, s % nt, 0)),
                  pl.BlockSpec((1, 6, D), lambda s: (s // nt, 0, 0)),
                  pl.BlockSpec((1, TC, TOP_K), lambda s: (s // nt, s % nt, 0))],
        out_specs=pl.BlockSpec((1, TC, D), lambda s: (s // nt, s % nt, 0)),
        out_shape=jax.ShapeDtypeStruct((B, S, D), F32),
        scratch_shapes=[pltpu.VMEM((TOP_K * TC * SUBLANES, LANES), F32),
                        pltpu.VMEM((TOP_K * TC * SUBLANES, LANES), F32),
                        pltpu.SemaphoreType.DMA(()),
                        pltpu.SemaphoreType.DMA(())],
        compiler_params=_cparams(("arbitrary",)),
        name="combine",
    )(pos_tiles, pos_tiles, ys, x1, mod3, gates)


def _lookup(table, idx):
    hit = idx[..., None] == jnp.arange(table.shape[0], dtype=jnp.int32)
    return jnp.sum(jnp.where(hit, table, 0), axis=-1).astype(jnp.int32)


def _routing_tables(top_idx, rank, counts, n_tokens):
    n_slots = n_tokens * TOP_K
    n_rows = n_slots + N_EXPERTS * RB
    n_blocks = n_rows // RB
    sizes = counts[0, :N_EXPERTS].astype(jnp.int32)
    padded = ((sizes + RB - 1) // RB) * RB
    padded_end = jnp.cumsum(padded).astype(jnp.int32)
    padded_start = padded_end - padded
    pos = _lookup(padded_start, top_idx) + rank
    n_used = padded_end[-1] // RB
    blk = jnp.arange(n_blocks, dtype=jnp.int32)
    block_expert = jnp.minimum(jnp.sum(padded_end[None, :] <= (blk * RB)[:, None], axis=1), N_EXPERTS - 1)
    last_e = jnp.minimum(jnp.sum(padded_end <= (n_used - 1) * RB), N_EXPERTS - 1)
    block_expert = jnp.where(blk < n_used, block_expert, last_e).astype(jnp.int32)
    pad_cnt = padded - sizes
    pad_end = jnp.cumsum(pad_cnt).astype(jnp.int32)
    j = jnp.arange(N_EXPERTS * RB, dtype=jnp.int32)
    e_j = jnp.sum(pad_end[None, :] <= j[:, None], axis=1).astype(jnp.int32)
    in_group = e_j < N_EXPERTS
    e_c = jnp.minimum(e_j, N_EXPERTS - 1)
    group_row = _lookup(padded_start + sizes - (pad_end - pad_cnt), e_c) + j
    tail_row = padded_end[-1] + (j - pad_end[-1])
    pad_rows = jnp.where(in_group, group_row, tail_row).astype(jnp.int32)
    return pos, pad_rows, block_expert, n_used.reshape(1).astype(jnp.int32), n_rows, n_blocks


def kernel(x, c, w_ada, b_ada, norm1_g, w_in, conv_w, conv_b, b_igate, b_fgate, mlstm_norm_g, q_norm_g, k_norm_g,
           lambda_q1, lambda_k1, lambda_q2, lambda_k2, diff_norm_g, w_out, norm2_g, w_router, b_router, w_gu, b_gu,
           w_down, b_down):
    B, S, D = x.shape
    N = B * S
    l = 0
    o_mq, o_mv, o_mo = 0, 512, 1024
    o_mi, o_mf = 1536, 1540
    o_aq, o_ak, o_av = 1544, 2056, 2568

    wi = w_in[l]
    w_main = jnp.concatenate([wi[:, o_mq:o_mq + 512], wi[:, o_mv:o_mv + 512], wi[:, o_mo:o_mo + 512],
                              wi[:, o_aq:o_aq + 512], wi[:, o_ak:o_ak + 512], wi[:, o_av:o_av + 512]],
                             axis=1).astype(BF16)
    def hi_lo_lanes(w, width):
        hi = w.astype(BF16)
        lo = (w - hi.astype(F32)).astype(BF16)
        return jnp.pad(jnp.concatenate([hi, lo], axis=1), ((0, 0), (0, width - 2 * w.shape[1])))

    wg2 = hi_lo_lanes(wi[:, o_mi:o_mi + 2 * M_HEADS], GW)
    pos_ids = jnp.arange(S, dtype=F32)
    inv_freq = ROPE_THETA ** (-jnp.arange(0, A_HEAD_DIM, 2, dtype=F32) / A_HEAD_DIM)
    ang = pos_ids[:, None] * inv_freq[None, :]
    cos_h, sin_h = jnp.cos(ang), jnp.sin(ang)
    cos_t = jnp.concatenate([cos_h, cos_h, cos_h, cos_h], axis=1)
    sin_t = jnp.concatenate([-sin_h, sin_h, -sin_h, sin_h], axis=1)
    qg = jnp.tile(q_norm_g[l], 512 // A_HEAD_DIM).reshape(1, 512)
    kg = jnp.tile(k_norm_g[l], 512 // A_HEAD_DIM).reshape(1, 512)
    gid = jnp.arange(512) // A_HEAD_DIM
    bd = (gid[:, None] == gid[None, :]).astype(BF16)
    bias_c = jnp.pad(jnp.concatenate([b_igate[l], b_fgate[l]]), (0, GW - 2 * M_HEADS)).reshape(1, GW)
    bias_r = jnp.concatenate([b_igate[l], b_fgate[l]]).reshape(2 * M_HEADS, 1)
    wr = jnp.pad(w_router[l], ((0, 0), (0, LANES - N_EXPERTS)))
    wr_hi = wr.astype(BF16)
    wr_lo = (wr - wr_hi.astype(F32)).astype(BF16)
    br = jnp.pad(b_router[l], (0, LANES - N_EXPERTS), constant_values=-jnp.inf).reshape(1, LANES)
    col = jnp.arange(256)
    psel = (jnp.where(col % 2 == 0, col // 2, LANES + col // 2)[:, None] == col[None, :]).astype(BF16)
    ltri = (jnp.arange(TM)[None, :] < jnp.arange(TM)[:, None]).astype(BF16)
    bgate = b_gu[l][:, None, 0::2]
    bup = b_gu[l][:, None, 1::2]
    bdn = b_down[l][:, None, :]

    mod3 = _ada(c, w_ada[l], b_ada[l]).reshape(B, 6, D)
    p_all, gates_c = _inproj(x, mod3, norm1_g[l].reshape(1, D), w_main, wg2, cos_t, sin_t, qg, kg, bd)
    gates_r = gates_c[:, :, :2 * M_HEADS].reshape(B, S // CHUNK, CHUNK, 2 * M_HEADS).transpose(0, 1, 3, 2)
    hm = _mlstm(p_all, gates_c, gates_r, bias_c, bias_r, conv_w[l], conv_b[l].reshape(1, 512),
                mlstm_norm_g[l].reshape(1, 512))
    ha = _attn(p_all, lambda_q1[l].reshape(1, -1), lambda_k1[l].reshape(1, -1), lambda_q2[l].reshape(1, -1),
               lambda_k2[l].reshape(1, -1), diff_norm_g[l].reshape(1, 512))

    x1, h2_rows, top_idx, gates, rank, counts = _outproj(hm, ha, x, mod3, norm2_g[l].reshape(1, D),
                                                         w_out[l].astype(BF16), wr_hi, wr_lo, br, ltri)

    pos, pad_rows, block_expert, n_used, n_rows, n_blocks = _routing_tables(top_idx, rank, counts, N)
    xs, wgu, wdn = _dispatch(pad_rows, pos.reshape(N // TM, 1, TM * TOP_K), h2_rows, w_gu[l], w_down[l], psel, n_rows)
    ys = _experts(block_expert, n_used, xs, wgu, bgate, bup, wdn, bdn, n_blocks)
    pos_tiles = pos.reshape(N // TC, TC, TOP_K).transpose(0, 2, 1).reshape(N // TC, 1, TOP_K * TC)
    return _combine(pos_tiles, ys, x1, mod3, gates)
```

```python
import functools
import math

import jax
import jax.numpy as jnp
from jax import lax
from jax.experimental import pallas as pl
from jax.experimental.pallas import tpu as pltpu

F32 = jnp.float32
BF16 = jnp.bfloat16
HIGHEST = lax.Precision.HIGHEST

LANES = 128
SUBLANES = 8
VMEM_LIMIT = 48 * 1024 * 1024

CHUNK = 64
M_HEADS = 4
M_QK_DIM = 64
M_V_DIM = 128
CONV_WIDTH = 4
A_HEADS = 4
A_HEAD_DIM = 64
A_V_DIM = 128
ROPE_THETA = 10000.0
N_EXPERTS = 32
TOP_K = 4
SWIGLU_ALPHA = 1.702
SWIGLU_LIMIT = 7.0
EPS = 1e-6
LAMBDA_INIT = 0.8 - 0.6 * math.exp(-0.3 * 0)

TM = 512
GCH = 8
CPT = 4
TQ = 512
RB = 512
TC = 512
GW = 128


def _dot(a, b):
    return jnp.dot(a, b, preferred_element_type=F32)


def _cparams(sem):
    return pltpu.CompilerParams(dimension_semantics=sem, vmem_limit_bytes=VMEM_LIMIT)


def _ada_kernel(c_ref, w_ref, b_ref, o_ref):
    c = c_ref[...]
    cond = c * jax.nn.sigmoid(c)
    o_ref[...] = jnp.dot(cond, w_ref[...], preferred_element_type=F32, precision=HIGHEST) + b_ref[...]


def _ada(c, w, b):
    B, D = c.shape
    n = w.shape[1]
    tn = 1024
    return pl.pallas_call(
        _ada_kernel,
        grid=(n // tn,),
        in_specs=[pl.BlockSpec((B, D), lambda j: (0, 0)),
                  pl.BlockSpec((D, tn), lambda j: (0, j)),
                  pl.BlockSpec((1, tn), lambda j: (0, j))],
        out_specs=pl.BlockSpec((B, tn), lambda j: (0, j)),
        out_shape=jax.ShapeDtypeStruct((B, n), F32),
        compiler_params=_cparams(("arbitrary",)),
        name="ada",
    )(c, w, b.reshape(1, n))


def _split_dot(hi, lo, w2, n_cols):
    both = _dot(hi, w2) + _dot(lo, w2)
    return both + pltpu.roll(both, both.shape[1] - n_cols, 1)


def _inproj_kernel(x_ref, mod_ref, g1_ref, w_ref, wg2_ref, cos_ref, sin_ref, qg_ref, kg_ref,
                   bd_ref, p_ref, gates_ref):
    x = x_ref[0]
    ms = jnp.mean(x * x, axis=-1, keepdims=True)
    shift = mod_ref[0, 0:1, :]
    scale = mod_ref[0, 1:2, :]
    h = (x * lax.rsqrt(ms + EPS) * g1_ref[...]) * (1.0 + scale) + shift
    hb = h.astype(BF16)
    hl = (h - hb.astype(F32)).astype(BF16)
    gates_ref[0] = _split_dot(hb, hl, wg2_ref[...], 2 * M_HEADS)

    tm = x.shape[0]
    lane = lax.broadcasted_iota(jnp.int32, (tm, 512), 1)
    first_half = (lane & 63) < 32
    cos = jnp.concatenate([cos_ref[...]] * 4, axis=1)
    sin = jnp.concatenate([sin_ref[...]] * 4, axis=1)
    for sec in range(6):
        acc = _dot(hb, w_ref[:, sec * 512:(sec + 1) * 512])
        if sec in (3, 4):
            g = qg_ref[...] if sec == 3 else kg_ref[...]
            sq = (acc * acc).astype(BF16)
            ssq = jnp.concatenate([_dot(sq[:, c * 256:(c + 1) * 256], bd_ref[...]) for c in range(2)], axis=1)
            y = acc * lax.rsqrt(ssq * (1.0 / A_HEAD_DIM) + EPS) * g
            swapped = jnp.where(first_half, pltpu.roll(y, 512 - 32, 1), pltpu.roll(y, 32, 1))
            acc = y * cos + swapped * sin
            if sec == 3:
                acc = acc * (A_HEAD_DIM ** -0.5)
        p_ref[0, :, sec * 512:(sec + 1) * 512] = acc.astype(BF16)


def _inproj(x, mod3, g1, w_main, wg2, cos_t, sin_t, qg, kg, bd):
    B, S, D = x.shape
    nw = w_main.shape[1]
    return pl.pallas_call(
        _inproj_kernel,
        grid=(B, S // TM),
        in_specs=[pl.BlockSpec((1, TM, D), lambda b, s: (b, s, 0)),
                  pl.BlockSpec((1, 6, D), lambda b, s: (b, 0, 0)),
                  pl.BlockSpec((1, D), lambda b, s: (0, 0)),
                  pl.BlockSpec((D, nw), lambda b, s: (0, 0)),
                  pl.BlockSpec((D, GW), lambda b, s: (0, 0)),
                  pl.BlockSpec((TM, LANES), lambda b, s: (s, 0)),
                  pl.BlockSpec((TM, LANES), lambda b, s: (s, 0)),
                  pl.BlockSpec((1, 512), lambda b, s: (0, 0)),
                  pl.BlockSpec((1, 512), lambda b, s: (0, 0)),
                  pl.BlockSpec((256, 256), lambda b, s: (0, 0))],
        out_specs=[pl.BlockSpec((1, TM, nw), lambda b, s: (b, s, 0)),
                   pl.BlockSpec((1, TM, GW), lambda b, s: (b, s, 0))],
        out_shape=[jax.ShapeDtypeStruct((B, S, nw), BF16),
                   jax.ShapeDtypeStruct((B, S, GW), F32)],
        compiler_params=_cparams(("arbitrary", "arbitrary")),
        name="inproj",
    )(x, mod3, g1, w_main, wg2, cos_t, sin_t, qg, kg, bd)


def _log_sigmoid(z):
    return jnp.minimum(z, 0.0) - jnp.log1p(jnp.exp(-jnp.abs(z)))


def _mlstm_kernel(qk_ref, v_ref, o_ref, gc_ref, gr_ref, bc_ref, br_ref, cw_ref, cb_ref, ng_ref,
                  out_ref, ubuf, q_sc, k_sc, gcs, grs, cst, msc):
    g = pl.program_id(1)
    T = GCH * CHUNK
    HW = M_HEADS * M_QK_DIM

    @pl.when(g == 0)
    def _():
        cst[...] = jnp.zeros_like(cst)
        msc[...] = jnp.zeros_like(msc)
        ubuf[0:SUBLANES, :] = jnp.zeros((SUBLANES, 2 * HW), F32)

    @pl.when(g > 0)
    def _():
        ubuf[0:SUBLANES, :] = ubuf[T:T + SUBLANES, :]

    ubuf[SUBLANES:SUBLANES + T, :] = qk_ref[0].astype(F32)

    def conv_strip(c, carry):
        r0 = pl.multiple_of(c * CHUNK, CHUNK)
        strip = ubuf[pl.ds(r0, CHUNK + SUBLANES), :]
        y = cb_ref[...]
        for j in range(CONV_WIDTH):
            off = SUBLANES - (CONV_WIDTH - 1) + j
            y = y + cw_ref[j:j + 1, :] * strip[off:off + CHUNK, :]
        qk = y * jax.nn.sigmoid(y)
        q_sc[pl.ds(r0, CHUNK), :] = qk[:, :HW].astype(BF16)
        k_sc[pl.ds(r0, CHUNK), :] = (qk[:, HW:] * (M_QK_DIM ** -0.5)).astype(BF16)
        return carry

    lax.fori_loop(0, GCH, conv_strip, 0)

    gc = gc_ref[0] + bc_ref[...]
    lane = lax.broadcasted_iota(jnp.int32, gc.shape, 1)
    gcs[...] = jnp.where(lane < M_HEADS, gc, _log_sigmoid(gc))
    gr = gr_ref[0] + br_ref[...]
    row = lax.broadcasted_iota(jnp.int32, gr.shape, 1)
    grs[...] = jnp.where(row < M_HEADS, gr, _log_sigmoid(gr))

    ti = lax.broadcasted_iota(jnp.int32, (CHUNK, CHUNK), 0)
    si = lax.broadcasted_iota(jnp.int32, (CHUNK, CHUNK), 1)
    causal = si <= ti
    tri = causal.astype(F32)
    tri_t = (ti <= si).astype(F32)
    lane256 = lax.broadcasted_iota(jnp.int32, (CHUNK, HW), 1)
    ones_blk = jnp.ones((CHUNK, LANES), BF16)
    ones_sq = jnp.ones((LANES, LANES), BF16)

    def twice(a):
        return jnp.concatenate([a, a], axis=1)

    def rep(col):
        return jnp.broadcast_to(col, (CHUNK, LANES))

    def stage1(c):
        r0 = pl.multiple_of(c * CHUNK, CHUNK)
        qc = q_sc[pl.ds(r0, CHUNK), :]
        kc = k_sc[pl.ds(r0, CHUNK), :]
        vc = v_ref[0, pl.ds(r0, CHUNK), :]
        gcc = gcs[pl.ds(r0, CHUNK), :]
        grr = grs[c]
        b_c = jnp.dot(tri, gcc, preferred_element_type=F32, precision=HIGHEST)
        b_r = jnp.dot(grr, tri_t, preferred_element_type=F32, precision=HIGHEST)
        qstack = jnp.concatenate(
            [jnp.where((lane256 // M_QK_DIM) == h, qc, jnp.zeros_like(qc)) for h in range(M_HEADS)], axis=0)
        kstack = jnp.concatenate(
            [jnp.where((lane256 // M_QK_DIM) == h, kc, jnp.zeros_like(kc)) for h in range(M_HEADS)], axis=0)
        s_all = lax.dot_general(qstack, kc, (((1,), (1,)), ((), ())), preferred_element_type=F32)
        heads, vws = [], []
        for h in range(M_HEADS):
            bcol = rep(b_c[:, M_HEADS + h:M_HEADS + h + 1])
            icol = rep(gcc[:, h:h + 1])
            brow = b_r[M_HEADS + h:M_HEADS + h + 1, :]
            irow = grr[h:h + 1, :]
            b_last = bcol[CHUNK - 1:CHUNK, :]
            dlog = jnp.where(causal, bcol[:, :CHUNK] - brow + irow, -jnp.inf)
            m_intra = jnp.max(dlog, axis=-1, keepdims=True)
            vaug = jnp.concatenate([vc[:, h * M_V_DIM:(h + 1) * M_V_DIM], ones_blk], axis=1)
            a_col = b_last - bcol + icol
            m_loc = jnp.max(a_col, axis=0, keepdims=True)
            w_col = jnp.exp(a_col - m_loc)
            vws.append((vaug.astype(F32) * twice(w_col)).astype(BF16))
            heads.append((bcol, b_last, dlog, m_intra, vaug, m_loc))
        c_loc = lax.dot_general(kstack, jnp.concatenate(vws, axis=0), (((0,), (0,)), ((), ())),
                                preferred_element_type=F32)
        return r0, qstack, s_all, heads, c_loc

    def stage2(sts):
        items = []
        m_run = [msc[h:h + 1, :] for h in range(M_HEADS)]
        for r0, qstack, s_all, heads, c_loc in sts:
            i_all = _dot(qstack, cst[...].astype(BF16))
            for h in range(M_HEADS):
                bcol, b_last, dlog, m_intra, vaug, m_loc = heads[h]
                ks = slice(h * M_QK_DIM, (h + 1) * M_QK_DIM)
                m_prev = m_run[h]
                m_new = jnp.maximum(b_last + m_prev, m_loc)
                decay = jnp.exp(b_last + m_prev - m_new)
                fresh = jnp.exp(m_loc - m_new)
                cst[ks, :] = twice(decay) * cst[ks, :] + twice(fresh) * c_loc[ks, :]
                m_run[h] = m_new
                items.append((r0, h, bcol, dlog, m_intra, vaug, m_prev, s_all, i_all))
        for h in range(M_HEADS):
            msc[h:h + 1, :] = m_run[h]
        prods = []
        for r0, h, bcol, dlog, m_intra, vaug, m_prev, s_all, i_all in items:
            rs = slice(h * CHUNK, (h + 1) * CHUNK)
            m_inter = bcol + m_prev
            m_t = jnp.maximum(m_inter, m_intra)
            d_w = jnp.exp(dlog - m_t[:, :CHUNK])
            inter_w = jnp.exp(m_inter - m_t)
            p = (s_all[rs, :] * d_w).astype(BF16)
            prods.append((_dot(p, vaug) + twice(inter_w) * i_all[rs, :], m_t))
        normed = []
        for r, m_t in prods:
            hv = r[:, :M_V_DIM] / jnp.maximum(jnp.abs(r[:, M_V_DIM:]), jnp.exp(-m_t))
            hh = hv * hv
            hh_hi = hh.astype(BF16)
            hh_lo = (hh - hh_hi.astype(F32)).astype(BF16)
            normed.append((hv, _dot(hh_hi, ones_sq) + _dot(hh_lo, ones_sq)))
        for (r0, h, *_), (hv, ssq) in zip(items, normed):
            hs = slice(h * M_V_DIM, (h + 1) * M_V_DIM)
            hn = hv * lax.rsqrt(ssq * (1.0 / M_V_DIM) + EPS) * ng_ref[:, hs]
            og = o_ref[0, pl.ds(r0, CHUNK), hs].astype(F32)
            out_ref[0, pl.ds(r0, CHUNK), hs] = (hn * jax.nn.sigmoid(og)).astype(BF16)

    def chunk_group(cg, carry):
        stage2([stage1(CPT * cg + k) for k in range(CPT)])
        return carry

    lax.fori_loop(0, GCH // CPT, chunk_group, 0)


def _mlstm(p_all, gates_c, gates_r, bias_c, bias_r, conv_w, conv_b, ng):
    B, S, _ = p_all.shape
    T = GCH * CHUNK
    return pl.pallas_call(
        _mlstm_kernel,
        grid=(B, S // T),
        in_specs=[pl.BlockSpec((1, T, 512), lambda b, g: (b, g, 0)),
                  pl.BlockSpec((1, T, 512), lambda b, g: (b, g, 1)),
                  pl.BlockSpec((1, T, 512), lambda b, g: (b, g, 2)),
                  pl.BlockSpec((1, T, GW), lambda b, g: (b, g, 0)),
                  pl.BlockSpec((1, GCH, SUBLANES, CHUNK), lambda b, g: (b, g, 0, 0)),
                  pl.BlockSpec((1, GW), lambda b, g: (0, 0)),
                  pl.BlockSpec((SUBLANES, 1), lambda b, g: (0, 0)),
                  pl.BlockSpec((CONV_WIDTH, 512), lambda b, g: (0, 0)),
                  pl.BlockSpec((1, 512), lambda b, g: (0, 0)),
                  pl.BlockSpec((1, 512), lambda b, g: (0, 0))],
        out_specs=pl.BlockSpec((1, T, 512), lambda b, g: (b, g, 0)),
        out_shape=jax.ShapeDtypeStruct((B, S, 512), BF16),
        scratch_shapes=[pltpu.VMEM((T + SUBLANES, 512), F32),
                        pltpu.VMEM((T, 256), BF16),
                        pltpu.VMEM((T, 256), BF16),
                        pltpu.VMEM((T, GW), F32),
                        pltpu.VMEM((GCH, SUBLANES, CHUNK), F32),
                        pltpu.VMEM((M_HEADS * M_QK_DIM, 2 * M_V_DIM), F32),
                        pltpu.VMEM((SUBLANES, LANES), F32)],
        compiler_params=_cparams(("arbitrary", "arbitrary")),
        name="mlstm",
    )(p_all, p_all, p_all, gates_c, gates_r, bias_c, bias_r, conv_w, conv_b, ng)


HPS = 2


def _attn_kernel(lq1_ref, lk1_ref, lq2_ref, lk2_ref, q_ref, k_ref, v_ref, ng_ref, o_ref, *scratch):
    i = pl.program_id(2)
    qs_scs, m_scs, acc_scs, sa_scs, sb_scs = (scratch[n * HPS:(n + 1) * HPS] for n in range(5))
    heads = range(HPS)
    lanes_of = [slice(hh * LANES, (hh + 1) * LANES) for hh in heads]
    ones_blk = jnp.ones((TQ, LANES), BF16)

    for hh in heads:
        q = q_ref[0, :, lanes_of[hh]]
        lane = lax.broadcasted_iota(jnp.int32, q.shape, 1)
        qs_scs[hh][0:TQ, :] = jnp.where(lane < A_HEAD_DIM, q, jnp.zeros_like(q))
        qs_scs[hh][TQ:2 * TQ, :] = jnp.where(lane >= A_HEAD_DIM, q, jnp.zeros_like(q))
        m_scs[hh][...] = jnp.full(m_scs[hh].shape, -jnp.inf, F32)
        acc_scs[hh][...] = jnp.zeros_like(acc_scs[hh])

    def scores(hh, j):
        k = k_ref[0, pl.ds(pl.multiple_of(j * TQ, TQ), TQ), lanes_of[hh]]
        return lax.dot_general(qs_scs[hh][...], k, (((1,), (1,)), ((), ())), preferred_element_type=F32)

    def diag_scores(hh):
        ri = lax.broadcasted_iota(jnp.int32, (2 * TQ, TQ), 0)
        ci = lax.broadcasted_iota(jnp.int32, (2 * TQ, TQ), 1)
        visible = (ci // CHUNK) <= ((ri & (TQ - 1)) // CHUNK)
        return jnp.where(visible, scores(hh, i), -jnp.inf)

    def accumulate(hh, s_ref, j):
        s = s_ref[...]
        v = v_ref[0, pl.ds(pl.multiple_of(j * TQ, TQ), TQ), lanes_of[hh]]
        m_old = m_scs[hh][...]
        m_new = jnp.maximum(m_old, jnp.max(s, axis=-1, keepdims=True))
        alpha = jnp.exp(m_old - m_new)
        p = jnp.exp(s - jnp.concatenate([m_new] * (TQ // LANES), axis=1))
        pv = _dot(p.astype(BF16), jnp.concatenate([v, ones_blk], axis=1))
        acc_scs[hh][...] = jnp.concatenate([alpha, alpha], axis=1) * acc_scs[hh][...] + pv
        m_scs[hh][...] = m_new

    def kv_of(t):
        return jnp.where(t == 0, i, t - 1)

    for hh in heads:
        sa_scs[hh][...] = diag_scores(hh)
    pairs = i // 2

    def body(u, carry):
        t = 2 * u
        for hh in heads:
            sb_scs[hh][...] = scores(hh, t)
            accumulate(hh, sa_scs[hh], kv_of(t))
        for hh in heads:
            sa_scs[hh][...] = scores(hh, t + 1)
            accumulate(hh, sb_scs[hh], t)
        return carry

    lax.fori_loop(0, pairs, body, 0)
    t0 = 2 * pairs

    @pl.when(i == t0)
    def _():
        for hh in heads:
            accumulate(hh, sa_scs[hh], kv_of(t0))

    @pl.when(i > t0)
    def _():
        for hh in heads:
            sb_scs[hh][...] = scores(hh, t0)
            accumulate(hh, sa_scs[hh], kv_of(t0))
        for hh in heads:
            accumulate(hh, sb_scs[hh], t0)

    lam = (jnp.exp(jnp.sum(lq1_ref[...] * lk1_ref[...], axis=-1, keepdims=True))
           - jnp.exp(jnp.sum(lq2_ref[...] * lk2_ref[...], axis=-1, keepdims=True)) + LAMBDA_INIT)
    for hh in heads:
        acc = acc_scs[hh][...]
        o = acc[:, :LANES] / acc[:, LANES:]
        a = o[0:TQ, :] - lam * o[TQ:2 * TQ, :]
        y = a * lax.rsqrt(jnp.mean(a * a, axis=-1, keepdims=True) + EPS) * ng_ref[:, lanes_of[hh]]
        o_ref[0, :, lanes_of[hh]] = (y * (1.0 - LAMBDA_INIT)).astype(BF16)


def _attn(p_all, lq1, lk1, lq2, lk2, ng):
    B, S, _ = p_all.shape
    hw = HPS * LANES
    nsec = 512 // hw
    lam_spec = pl.BlockSpec((1, A_HEAD_DIM), lambda b, h, i: (0, 0))
    per_head = lambda shape, dtype: [pltpu.VMEM(shape, dtype) for _ in range(HPS)]
    return pl.pallas_call(
        _attn_kernel,
        grid=(B, A_HEADS // HPS, S // TQ),
        in_specs=[lam_spec, lam_spec, lam_spec, lam_spec,
                  pl.BlockSpec((1, TQ, hw), lambda b, h, i: (b, i, 3 * nsec + h)),
                  pl.BlockSpec((1, S, hw), lambda b, h, i: (b, 0, 4 * nsec + h)),
                  pl.BlockSpec((1, S, hw), lambda b, h, i: (b, 0, 5 * nsec + h)),
                  pl.BlockSpec((1, hw), lambda b, h, i: (0, h))],
        out_specs=pl.BlockSpec((1, TQ, hw), lambda b, h, i: (b, i, h)),
        out_shape=jax.ShapeDtypeStruct((B, S, 512), BF16),
        scratch_shapes=(per_head((2 * TQ, LANES), BF16) + per_head((2 * TQ, LANES), F32)
                        + per_head((2 * TQ, 2 * LANES), F32) + per_head((2 * TQ, TQ), F32)
                        + per_head((2 * TQ, TQ), F32)),
        compiler_params=_cparams(("arbitrary", "arbitrary", "arbitrary")),
        name="attn",
    )(lq1, lk1, lq2, lk2, p_all, p_all, p_all, ng)


def _outproj_kernel(hm_ref, ha_ref, x_ref, mod_ref, g2_ref, wo_ref, wrh_ref, wrl_ref, br_ref, ltri_ref,
                    x1_ref, h2_ref, idx_ref, gate_ref, rank_ref, cnt_ref, cnt_sc):
    @pl.when((pl.program_id(0) == 0) & (pl.program_id(1) == 0))
    def _():
        cnt_sc[...] = jnp.zeros_like(cnt_sc)

    hcat = jnp.concatenate([hm_ref[0], ha_ref[0]], axis=1)
    mix = _dot(hcat, wo_ref[...])
    gate1 = mod_ref[0, 2:3, :]
    shift2 = mod_ref[0, 3:4, :]
    scale2 = mod_ref[0, 4:5, :]
    x1 = x_ref[0] + gate1 * mix
    x1_ref[0] = x1
    ms = jnp.mean(x1 * x1, axis=-1, keepdims=True)
    h2 = (x1 * lax.rsqrt(ms + EPS) * g2_ref[...]) * (1.0 + scale2) + shift2
    tm = h2.shape[0]
    for s in range(SUBLANES):
        h2_ref[pl.ds(s, tm, stride=SUBLANES), :] = h2[:, s * LANES:(s + 1) * LANES]
    hb = h2.astype(BF16)
    hl = (h2 - hb.astype(F32)).astype(BF16)
    logits = _dot(hb, wrh_ref[...]) + _dot(hl, wrh_ref[...]) + _dot(hb, wrl_ref[...]) + br_ref[...]
    lane = lax.broadcasted_iota(jnp.int32, logits.shape, 1).astype(F32)
    vals, idxs = [], []
    work = logits
    for _ in range(TOP_K):
        mx = jnp.max(work, axis=-1, keepdims=True)
        ix = jnp.min(jnp.where(work == mx, lane, float(LANES)), axis=-1, keepdims=True)
        vals.append(mx)
        idxs.append(ix)
        work = jnp.where(lane == ix, -jnp.inf, work)
    es = [jnp.exp(v - vals[0]) for v in vals]
    tot = es[0] + es[1] + es[2] + es[3]
    gsel = jnp.zeros_like(logits)
    isel = jnp.zeros_like(logits)
    for k in range(TOP_K):
        gsel = jnp.where(lane == float(k), es[k] / tot, gsel)
        isel = jnp.where(lane == float(k), idxs[k], isel)
    gate_ref[0] = gsel[:, :TOP_K]
    idx_ref[0] = isel[:, :TOP_K].astype(jnp.int32)
    chosen = [lane == ix for ix in idxs]
    multi = jnp.zeros_like(logits)
    for ch in chosen:
        multi = jnp.where(ch, 1.0, multi)
    before = _dot(ltri_ref[...], multi.astype(BF16)) + cnt_sc[...]
    rsel = jnp.zeros_like(logits)
    for k in range(TOP_K):
        rk = jnp.sum(jnp.where(chosen[k], before, 0.0), axis=-1, keepdims=True)
        rsel = jnp.where(lane == float(k), rk, rsel)
    rank_ref[0] = rsel[:, :TOP_K].astype(jnp.int32)
    cnt_sc[...] = cnt_sc[...] + jnp.sum(multi, axis=0, keepdims=True)
    cnt_ref[...] = cnt_sc[...]


def _outproj(hm, ha, x, mod3, g2, wo, wr_hi, wr_lo, br, ltri):
    B, S, D = x.shape
    nt = S // TM
    return pl.pallas_call(
        _outproj_kernel,
        grid=(B, nt),
        in_specs=[pl.BlockSpec((1, TM, 512), lambda b, s: (b, s, 0)),
                  pl.BlockSpec((1, TM, 512), lambda b, s: (b, s, 0)),
                  pl.BlockSpec((1, TM, D), lambda b, s: (b, s, 0)),
                  pl.BlockSpec((1, 6, D), lambda b, s: (b, 0, 0)),
                  pl.BlockSpec((1, D), lambda b, s: (0, 0)),
                  pl.BlockSpec((D, D), lambda b, s: (0, 0)),
                  pl.BlockSpec((D, LANES), lambda b, s: (0, 0)),
                  pl.BlockSpec((D, LANES), lambda b, s: (0, 0)),
                  pl.BlockSpec((1, LANES), lambda b, s: (0, 0)),
                  pl.BlockSpec((TM, TM), lambda b, s: (0, 0))],
        out_specs=[pl.BlockSpec((1, TM, D), lambda b, s: (b, s, 0)),
                   pl.BlockSpec((TM * SUBLANES, LANES), lambda b, s: (b * nt + s, 0)),
                   pl.BlockSpec((1, TM, TOP_K), lambda b, s: (b, s, 0)),
                   pl.BlockSpec((1, TM, TOP_K), lambda b, s: (b, s, 0)),
                   pl.BlockSpec((1, TM, TOP_K), lambda b, s: (b, s, 0)),
                   pl.BlockSpec((1, LANES), lambda b, s: (0, 0))],
        out_shape=[jax.ShapeDtypeStruct((B, S, D), F32),
                   jax.ShapeDtypeStruct((B * S * SUBLANES, LANES), F32),
                   jax.ShapeDtypeStruct((B, S, TOP_K), jnp.int32),
                   jax.ShapeDtypeStruct((B, S, TOP_K), F32),
                   jax.ShapeDtypeStruct((B, S, TOP_K), jnp.int32),
                   jax.ShapeDtypeStruct((1, LANES), F32)],
        scratch_shapes=[pltpu.VMEM((1, LANES), F32)],
        compiler_params=_cparams(("arbitrary", "arbitrary")),
        name="outproj",
    )(hm, ha, x, mod3, g2, wo, wr_hi, wr_lo, br, ltri)


DMA_UNROLL = 8


def _row_copy(src, dst, src_row, dst_row, sem):
    def tile_start(row):
        start = row * SUBLANES
        return start if isinstance(start, int) else pl.multiple_of(start, SUBLANES)

    return pltpu.make_async_copy(
        src.at[pl.ds(tile_start(src_row), SUBLANES), :],
        dst.at[pl.ds(tile_start(dst_row), SUBLANES), :],
        sem)


def _wait_rows(hbm, n_rows, sem):
    span = hbm.at[pl.ds(0, n_rows * SUBLANES), :]
    pltpu.make_async_copy(span, span, sem).wait()


def _dispatch_kernel(pad_ref, pos_ref, h2_ref, wgu_ref, wdn_ref, psel_ref, xs_hbm, wgu_out, wdn_out,
                     zero_sc, sem, sem_pad, *, n_pad, steps_per_expert):
    i = pl.program_id(0)
    n = TOP_K * TM

    @pl.when(i == 0)
    def _():
        zero_sc[...] = jnp.zeros_like(zero_sc)

        def issue_pad(g, carry):
            for u in range(DMA_UNROLL):
                _row_copy(zero_sc, xs_hbm, 0, pad_ref[g * DMA_UNROLL + u], sem_pad).start(priority=u % 2)
            return carry

        lax.fori_loop(0, n_pad // DMA_UNROLL, issue_pad, 0)

    def issue(g, carry):
        for u in range(DMA_UNROLL):
            j = g * DMA_UNROLL + u
            tok = g * (DMA_UNROLL // TOP_K) + u // TOP_K
            _row_copy(h2_ref, xs_hbm, tok, pos_ref[0, 0, j], sem).start(priority=u % 2)
        return carry

    lax.fori_loop(0, n // DMA_UNROLL, issue, 0)

    @pl.when(i % steps_per_expert == 0)
    def _():
        half = wgu_out.shape[2] // 2
        for cblk in range(wgu_out.shape[2] // 256):
            blk = _dot(wgu_ref[0, :, cblk * 256:(cblk + 1) * 256].astype(BF16), psel_ref[...])
            wgu_out[0, :, cblk * LANES:(cblk + 1) * LANES] = blk[:, :LANES].astype(BF16)
            wgu_out[0, :, half + cblk * LANES:half + (cblk + 1) * LANES] = blk[:, LANES:].astype(BF16)
        wdn_out[0] = wdn_ref[0].astype(BF16)

    _wait_rows(xs_hbm, n, sem)

    @pl.when(i == 0)
    def _():
        _wait_rows(xs_hbm, n_pad, sem_pad)


def _dispatch(pad_rows, pos_tiles, h2_rows, w_gu, w_down, psel, n_rows):
    n_tiles = pos_tiles.shape[0]
    n_pad = pad_rows.shape[0]
    E, D, F2 = w_gu.shape
    assert n_tiles % E == 0, "dispatch steps must be a multiple of the expert count"
    spe = n_tiles // E
    wmap = lambda i, p: (i // spe, 0, 0)
    return pl.pallas_call(
        functools.partial(_dispatch_kernel, n_pad=n_pad, steps_per_expert=spe),
        grid_spec=pltpu.PrefetchScalarGridSpec(
            num_scalar_prefetch=1,
            grid=(n_tiles,),
            in_specs=[pl.BlockSpec((1, 1, TOP_K * TM), lambda i, p: (i, 0, 0), memory_space=pltpu.SMEM),
                      pl.BlockSpec((TM * SUBLANES, LANES), lambda i, p: (i, 0)),
                      pl.BlockSpec((1, D, F2), wmap),
                      pl.BlockSpec((1, F2 // 2, D), wmap),
                      pl.BlockSpec((256, 256), lambda i, p: (0, 0))],
            out_specs=[pl.BlockSpec(memory_space=pl.ANY),
                       pl.BlockSpec((1, D, F2), wmap),
                       pl.BlockSpec((1, F2 // 2, D), wmap)],
            scratch_shapes=[pltpu.VMEM((SUBLANES, LANES), F32),
                            pltpu.SemaphoreType.DMA(()),
                            pltpu.SemaphoreType.DMA(())]),
        out_shape=[jax.ShapeDtypeStruct((n_rows * SUBLANES, LANES), F32),
                   jax.ShapeDtypeStruct((E, D, F2), BF16),
                   jax.ShapeDtypeStruct((E, F2 // 2, D), BF16)],
        compiler_params=_cparams(("arbitrary",)),
        name="dispatch",
    )(pad_rows, pos_tiles, h2_rows, w_gu, w_down, psel)


def _expert_kernel(be_ref, nused_ref, xs_ref, wgu_ref, bg_ref, bu_ref, wd_ref, bd_ref, ys_ref):
    i = pl.program_id(0)

    @pl.when(i < nused_ref[0])
    def _():
        x = jnp.concatenate([xs_ref[pl.ds(s, RB, stride=SUBLANES), :] for s in range(SUBLANES)], axis=1).astype(BF16)
        gu = _dot(x, wgu_ref[0])
        half = gu.shape[1] // 2
        gate = jnp.minimum(gu[:, :half] + bg_ref[0], SWIGLU_LIMIT)
        up = jnp.clip(gu[:, half:] + bu_ref[0], -SWIGLU_LIMIT, SWIGLU_LIMIT)
        act = (up + 1.0) * (gate * jax.nn.sigmoid(SWIGLU_ALPHA * gate))
        out = _dot(act.astype(BF16), wd_ref[0]) + bd_ref[0]
        for s in range(SUBLANES):
            ys_ref[pl.ds(s, RB, stride=SUBLANES), :] = out[:, s * LANES:(s + 1) * LANES]

    @pl.when(i >= nused_ref[0])
    def _():
        ys_ref[...] = jnp.zeros_like(ys_ref)


def _experts(block_expert, nused, xs, wgu, bg, bu, wd, bd, n_blocks):
    D = wgu.shape[1]
    F = wgu.shape[2] // 2
    wmap = lambda i, be, n: (be[i], 0, 0)
    return pl.pallas_call(
        _expert_kernel,
        grid_spec=pltpu.PrefetchScalarGridSpec(
            num_scalar_prefetch=2,
            grid=(n_blocks,),
            in_specs=[pl.BlockSpec((RB * SUBLANES, LANES), lambda i, be, n: (i, 0)),
                      pl.BlockSpec((1, D, 2 * F), wmap),
                      pl.BlockSpec((1, 1, F), wmap),
                      pl.BlockSpec((1, 1, F), wmap),
                      pl.BlockSpec((1, F, D), wmap),
                      pl.BlockSpec((1, 1, D), wmap)],
            out_specs=pl.BlockSpec((RB * SUBLANES, LANES), lambda i, be, n: (i, 0))),
        out_shape=jax.ShapeDtypeStruct((n_blocks * RB * SUBLANES, LANES), F32),
        compiler_params=_cparams(("arbitrary",)),
        name="experts",
    )(block_expert, nused, xs, wgu, bg, bu, wd, bd)


CB_TRIPS = 32


def _combine_kernel(pos_ref, posn_ref, ys_hbm, x1_ref, mod_ref, gate_ref, o_ref, buf_a, buf_b, sem_a, sem_b):
    s = pl.program_id(0)
    last = pl.num_programs(0) - 1
    n = TOP_K * TC
    per_trip = n // CB_TRIPS
    ts = TC // CB_TRIPS
    gate2 = mod_ref[0, 5:6, :]

    def issue_group(p_ref, buf, sem, g):
        for u in range(per_trip):
            j = g * per_trip + u
            _row_copy(ys_hbm, buf, p_ref[0, 0, j], j, sem).start(priority=u % 2)

    def sum_strip(buf, g):
        t0 = pl.multiple_of(g * ts, ts)
        gates = gate_ref[0, pl.ds(t0, ts), :]
        y = None
        for k in range(TOP_K):
            base = (k * TC + t0) * SUBLANES
            rows = jnp.concatenate(
                [buf[pl.ds(base + sl, ts, stride=SUBLANES), :] for sl in range(SUBLANES)], axis=1)
            term = gates[:, k:k + 1] * rows
            y = term if y is None else y + term
        o_ref[0, pl.ds(t0, ts), :] = x1_ref[0, pl.ds(t0, ts), :] + gate2 * y

    def step(cur_buf, cur_sem, nxt_buf, nxt_sem):
        @pl.when(s == 0)
        def _():
            def first(g, carry):
                issue_group(pos_ref, cur_buf, cur_sem, g)
                return carry

            lax.fori_loop(0, CB_TRIPS, first, 0)

        _wait_rows(ys_hbm, n, cur_sem)

        @pl.when(s < last)
        def _():
            def fused(g, carry):
                issue_group(posn_ref, nxt_buf, nxt_sem, g)
                sum_strip(cur_buf, g)
                return carry

            lax.fori_loop(0, CB_TRIPS, fused, 0)

        @pl.when(s == last)
        def _():
            def tail(g, carry):
                sum_strip(cur_buf, g)
                return carry

            lax.fori_loop(0, CB_TRIPS, tail, 0)

    @pl.when(s % 2 == 0)
    def _():
        step(buf_a, sem_a, buf_b, sem_b)

    @pl.when(s % 2 == 1)
    def _():
        step(buf_b, sem_b, buf_a, sem_a)


def _combine(pos_tiles, ys, x1, mod3, gates):
    B, S, D = x1.shape
    nt = S // TC
    n_steps = B * nt
    pos_spec = lambda off: pl.BlockSpec((1, 1, TOP_K * TC), lambda s: (jnp.minimum(s + off, n_steps - 1), 0, 0),
                                        memory_space=pltpu.SMEM)
    return pl.pallas_call(
        _combine_kernel,
        grid=(n_steps,),
        in_specs=[pos_spec(0), pos_spec(1),
                  pl.BlockSpec(memory_space=pl.ANY),
                  pl.BlockSpec((1, TC, D), lambda s: (s // nt, s % nt, 0)),
                  pl.BlockSpec((1, 6, D), lambda s: (s // nt, 0, 0)),
                  pl.BlockSpec((1, TC, TOP_K), lambda s: (s // nt, s % nt, 0))],
        out_specs=pl.BlockSpec((1, TC, D), lambda s: (s // nt, s % nt, 0)),
        out_shape=jax.ShapeDtypeStruct((B, S, D), F32),
        scratch_shapes=[pltpu.VMEM((TOP_K * TC * SUBLANES, LANES), F32),
                        pltpu.VMEM((TOP_K * TC * SUBLANES, LANES), F32),
                        pltpu.SemaphoreType.DMA(()),
                        pltpu.SemaphoreType.DMA(())],
        compiler_params=_cparams(("arbitrary",)),
        name="combine",
    )(pos_tiles, pos_tiles, ys, x1, mod3, gates)


def _lookup(table, idx):
    flat = idx.reshape(-1, LANES)
    out = jnp.zeros_like(flat)
    for e in range(table.shape[0]):
        out = jnp.where(flat == e, table[e], out)
    return out.reshape(idx.shape).astype(jnp.int32)


def _routing_tables(top_idx, rank, counts, n_tokens):
    n_slots = n_tokens * TOP_K
    n_rows = n_slots + N_EXPERTS * RB
    n_blocks = n_rows // RB
    sizes = counts[0, :N_EXPERTS].astype(jnp.int32)
    padded = ((sizes + RB - 1) // RB) * RB
    padded_end = jnp.cumsum(padded).astype(jnp.int32)
    padded_start = padded_end - padded
    pos = _lookup(padded_start, top_idx) + rank
    n_used = padded_end[-1] // RB
    blk = jnp.arange(n_blocks, dtype=jnp.int32)
    block_expert = jnp.minimum(jnp.sum(padded_end[None, :] <= (blk * RB)[:, None], axis=1), N_EXPERTS - 1)
    last_e = jnp.minimum(jnp.sum(padded_end <= (n_used - 1) * RB), N_EXPERTS - 1)
    block_expert = jnp.where(blk < n_used, block_expert, last_e).astype(jnp.int32)
    pad_cnt = padded - sizes
    pad_end = jnp.cumsum(pad_cnt).astype(jnp.int32)
    j = jnp.arange(N_EXPERTS * RB, dtype=jnp.int32)
    e_j = jnp.sum(pad_end[None, :] <= j[:, None], axis=1).astype(jnp.int32)
    in_group = e_j < N_EXPERTS
    e_c = jnp.minimum(e_j, N_EXPERTS - 1)
    group_row = _lookup(padded_start + sizes - (pad_end - pad_cnt), e_c) + j
    tail_row = padded_end[-1] + (j - pad_end[-1])
    pad_rows = jnp.where(in_group, group_row, tail_row).astype(jnp.int32)
    return pos, pad_rows, block_expert, n_used.reshape(1).astype(jnp.int32), n_rows, n_blocks


def kernel(x, c, w_ada, b_ada, norm1_g, w_in, conv_w, conv_b, b_igate, b_fgate, mlstm_norm_g, q_norm_g, k_norm_g,
           lambda_q1, lambda_k1, lambda_q2, lambda_k2, diff_norm_g, w_out, norm2_g, w_router, b_router, w_gu, b_gu,
           w_down, b_down):
    B, S, D = x.shape
    N = B * S
    l = 0
    o_mq, o_mv, o_mo = 0, 512, 1024
    o_mi, o_mf = 1536, 1540
    o_aq, o_ak, o_av = 1544, 2056, 2568

    wi = w_in[l]
    w_main = jnp.concatenate([wi[:, o_mq:o_mq + 512], wi[:, o_mv:o_mv + 512], wi[:, o_mo:o_mo + 512],
                              wi[:, o_aq:o_aq + 512], wi[:, o_ak:o_ak + 512], wi[:, o_av:o_av + 512]],
                             axis=1).astype(BF16)
    def hi_lo_lanes(w, width):
        hi = w.astype(BF16)
        lo = (w - hi.astype(F32)).astype(BF16)
        return jnp.pad(jnp.concatenate([hi, lo], axis=1), ((0, 0), (0, width - 2 * w.shape[1])))

    wg2 = hi_lo_lanes(wi[:, o_mi:o_mi + 2 * M_HEADS], GW)
    pos_ids = jnp.arange(S, dtype=F32)
    inv_freq = ROPE_THETA ** (-jnp.arange(0, A_HEAD_DIM, 2, dtype=F32) / A_HEAD_DIM)
    ang = pos_ids[:, None] * inv_freq[None, :]
    cos_h, sin_h = jnp.cos(ang), jnp.sin(ang)
    cos_t = jnp.concatenate([cos_h, cos_h, cos_h, cos_h], axis=1)
    sin_t = jnp.concatenate([-sin_h, sin_h, -sin_h, sin_h], axis=1)
    qg = jnp.tile(q_norm_g[l], 512 // A_HEAD_DIM).reshape(1, 512)
    kg = jnp.tile(k_norm_g[l], 512 // A_HEAD_DIM).reshape(1, 512)
    gid = jnp.arange(256) // A_HEAD_DIM
    bd = (gid[:, None] == gid[None, :]).astype(BF16)
    bias_c = jnp.pad(jnp.concatenate([b_igate[l], b_fgate[l]]), (0, GW - 2 * M_HEADS)).reshape(1, GW)
    bias_r = jnp.concatenate([b_igate[l], b_fgate[l]]).reshape(2 * M_HEADS, 1)
    wr = jnp.pad(w_router[l], ((0, 0), (0, LANES - N_EXPERTS)))
    wr_hi = wr.astype(BF16)
    wr_lo = (wr - wr_hi.astype(F32)).astype(BF16)
    br = jnp.pad(b_router[l], (0, LANES - N_EXPERTS), constant_values=-jnp.inf).reshape(1, LANES)
    col = jnp.arange(256)
    psel = (jnp.where(col % 2 == 0, col // 2, LANES + col // 2)[:, None] == col[None, :]).astype(BF16)
    ltri = (jnp.arange(TM)[None, :] < jnp.arange(TM)[:, None]).astype(BF16)
    bgate = b_gu[l][:, None, 0::2]
    bup = b_gu[l][:, None, 1::2]
    bdn = b_down[l][:, None, :]

    mod3 = _ada(c, w_ada[l], b_ada[l]).reshape(B, 6, D)
    p_all, gates_c = _inproj(x, mod3, norm1_g[l].reshape(1, D), w_main, wg2, cos_t, sin_t, qg, kg, bd)
    gates_r = gates_c[:, :, :2 * M_HEADS].reshape(B, S // CHUNK, CHUNK, 2 * M_HEADS).transpose(0, 1, 3, 2)
    hm = _mlstm(p_all, gates_c, gates_r, bias_c, bias_r, conv_w[l], conv_b[l].reshape(1, 512),
                mlstm_norm_g[l].reshape(1, 512))
    ha = _attn(p_all, lambda_q1[l].reshape(1, -1), lambda_k1[l].reshape(1, -1), lambda_q2[l].reshape(1, -1),
               lambda_k2[l].reshape(1, -1), diff_norm_g[l].reshape(1, 512))

    x1, h2_rows, top_idx, gates, rank, counts = _outproj(hm, ha, x, mod3, norm2_g[l].reshape(1, D),
                                                         w_out[l].astype(BF16), wr_hi, wr_lo, br, ltri)

    pos, pad_rows, block_expert, n_used, n_rows, n_blocks = _routing_tables(top_idx, rank, counts, N)
    xs, wgu, wdn = _dispatch(pad_rows, pos.reshape(N // TM, 1, TM * TOP_K), h2_rows, w_gu[l], w_down[l], psel, n_rows)
    ys = _experts(block_expert, n_used, xs, wgu, bgate, bup, wdn, bdn, n_blocks)
    pos_tiles = pos.reshape(N // TC, TC, TOP_K).transpose(0, 2, 1).reshape(N // TC, 1, TOP_K * TC)
    return _combine(pos_tiles, ys, x1, mod3, gates)
```

```python
import functools
import math

import jax
import jax.numpy as jnp
from jax import lax
from jax.experimental import pallas as pl
from jax.experimental.pallas import tpu as pltpu

F32 = jnp.float32
BF16 = jnp.bfloat16
HIGHEST = lax.Precision.HIGHEST

LANES = 128
SUBLANES = 8
VMEM_LIMIT = 48 * 1024 * 1024

CHUNK = 64
M_HEADS = 4
M_QK_DIM = 64
M_V_DIM = 128
CONV_WIDTH = 4
A_HEADS = 4
A_HEAD_DIM = 64
A_V_DIM = 128
ROPE_THETA = 10000.0
N_EXPERTS = 32
TOP_K = 4
SWIGLU_ALPHA = 1.702
SWIGLU_LIMIT = 7.0
EPS = 1e-6
LAMBDA_INIT = 0.8 - 0.6 * math.exp(-0.3 * 0)

TM = 512
GCH = 8
CPT = 4
TQ = 512
RB = 512
TC = 512
GW = 128


def _dot(a, b):
    return jnp.dot(a, b, preferred_element_type=F32)


def _cparams(sem):
    return pltpu.CompilerParams(dimension_semantics=sem, vmem_limit_bytes=VMEM_LIMIT)


def _ada_kernel(c_ref, w_ref, b_ref, o_ref):
    c = c_ref[...]
    cond = c * jax.nn.sigmoid(c)
    o_ref[...] = jnp.dot(cond, w_ref[...], preferred_element_type=F32, precision=HIGHEST) + b_ref[...]


def _ada(c, w, b):
    B, D = c.shape
    n = w.shape[1]
    tn = 1024
    return pl.pallas_call(
        _ada_kernel,
        grid=(n // tn,),
        in_specs=[pl.BlockSpec((B, D), lambda j: (0, 0)),
                  pl.BlockSpec((D, tn), lambda j: (0, j)),
                  pl.BlockSpec((1, tn), lambda j: (0, j))],
        out_specs=pl.BlockSpec((B, tn), lambda j: (0, j)),
        out_shape=jax.ShapeDtypeStruct((B, n), F32),
        compiler_params=_cparams(("arbitrary",)),
        name="ada",
    )(c, w, b.reshape(1, n))


def _split_dot(hi, lo, w2, n_cols):
    both = _dot(hi, w2) + _dot(lo, w2)
    return both + pltpu.roll(both, both.shape[1] - n_cols, 1)


def _inproj_kernel(x_ref, mod_ref, g1_ref, w_ref, wg2_ref, cos_ref, sin_ref, qg_ref, kg_ref,
                   bd_ref, p_ref, gates_ref):
    x = x_ref[0]
    ms = jnp.mean(x * x, axis=-1, keepdims=True)
    shift = mod_ref[0, 0:1, :]
    scale = mod_ref[0, 1:2, :]
    h = (x * lax.rsqrt(ms + EPS) * g1_ref[...]) * (1.0 + scale) + shift
    hb = h.astype(BF16)
    hl = (h - hb.astype(F32)).astype(BF16)
    gates_ref[0] = _split_dot(hb, hl, wg2_ref[...], 2 * M_HEADS)

    tm = x.shape[0]
    lane = lax.broadcasted_iota(jnp.int32, (tm, 512), 1)
    first_half = (lane & 63) < 32
    cos = jnp.concatenate([cos_ref[...]] * 4, axis=1)
    sin = jnp.concatenate([sin_ref[...]] * 4, axis=1)
    for sec in range(6):
        acc = _dot(hb, w_ref[:, sec * 512:(sec + 1) * 512])
        if sec in (3, 4):
            g = qg_ref[...] if sec == 3 else kg_ref[...]
            sq = (acc * acc).astype(BF16)
            ssq = jnp.concatenate([_dot(sq[:, c * 256:(c + 1) * 256], bd_ref[...]) for c in range(2)], axis=1)
            y = acc * lax.rsqrt(ssq * (1.0 / A_HEAD_DIM) + EPS) * g
            swapped = jnp.where(first_half, pltpu.roll(y, 512 - 32, 1), pltpu.roll(y, 32, 1))
            acc = y * cos + swapped * sin
            if sec == 3:
                acc = acc * (A_HEAD_DIM ** -0.5)
        p_ref[0, :, sec * 512:(sec + 1) * 512] = acc.astype(BF16)


def _inproj(x, mod3, g1, w_main, wg2, cos_t, sin_t, qg, kg, bd):
    B, S, D = x.shape
    nw = w_main.shape[1]
    return pl.pallas_call(
        _inproj_kernel,
        grid=(B, S // TM),
        in_specs=[pl.BlockSpec((1, TM, D), lambda b, s: (b, s, 0)),
                  pl.BlockSpec((1, 6, D), lambda b, s: (b, 0, 0)),
                  pl.BlockSpec((1, D), lambda b, s: (0, 0)),
                  pl.BlockSpec((D, nw), lambda b, s: (0, 0)),
                  pl.BlockSpec((D, GW), lambda b, s: (0, 0)),
                  pl.BlockSpec((TM, LANES), lambda b, s: (s, 0)),
                  pl.BlockSpec((TM, LANES), lambda b, s: (s, 0)),
                  pl.BlockSpec((1, 512), lambda b, s: (0, 0)),
                  pl.BlockSpec((1, 512), lambda b, s: (0, 0)),
                  pl.BlockSpec((256, 256), lambda b, s: (0, 0))],
        out_specs=[pl.BlockSpec((1, TM, nw), lambda b, s: (b, s, 0)),
                   pl.BlockSpec((1, TM, GW), lambda b, s: (b, s, 0))],
        out_shape=[jax.ShapeDtypeStruct((B, S, nw), BF16),
                   jax.ShapeDtypeStruct((B, S, GW), F32)],
        compiler_params=_cparams(("arbitrary", "arbitrary")),
        name="inproj",
    )(x, mod3, g1, w_main, wg2, cos_t, sin_t, qg, kg, bd)


def _log_sigmoid(z):
    return jnp.minimum(z, 0.0) - jnp.log1p(jnp.exp(-jnp.abs(z)))


def _mlstm_kernel(qk_ref, v_ref, o_ref, gc_ref, gr_ref, bc_ref, br_ref, cw_ref, cb_ref, ng_ref,
                  out_ref, ubuf, q_sc, k_sc, gcs, grs, cst, msc):
    g = pl.program_id(1)
    T = GCH * CHUNK
    HW = M_HEADS * M_QK_DIM

    @pl.when(g == 0)
    def _():
        cst[...] = jnp.zeros_like(cst)
        msc[...] = jnp.zeros_like(msc)
        ubuf[0:SUBLANES, :] = jnp.zeros((SUBLANES, 2 * HW), F32)

    @pl.when(g > 0)
    def _():
        ubuf[0:SUBLANES, :] = ubuf[T:T + SUBLANES, :]

    ubuf[SUBLANES:SUBLANES + T, :] = qk_ref[0].astype(F32)

    def conv_strip(c, carry):
        r0 = pl.multiple_of(c * CHUNK, CHUNK)
        strip = ubuf[pl.ds(r0, CHUNK + SUBLANES), :]
        y = cb_ref[...]
        for j in range(CONV_WIDTH):
            off = SUBLANES - (CONV_WIDTH - 1) + j
            y = y + cw_ref[j:j + 1, :] * strip[off:off + CHUNK, :]
        qk = y * jax.nn.sigmoid(y)
        q_sc[pl.ds(r0, CHUNK), :] = qk[:, :HW].astype(BF16)
        k_sc[pl.ds(r0, CHUNK), :] = (qk[:, HW:] * (M_QK_DIM ** -0.5)).astype(BF16)
        return carry

    lax.fori_loop(0, GCH, conv_strip, 0)

    gc = gc_ref[0] + bc_ref[...]
    lane = lax.broadcasted_iota(jnp.int32, gc.shape, 1)
    gcs[...] = jnp.where(lane < M_HEADS, gc, _log_sigmoid(gc))
    gr = gr_ref[0] + br_ref[...]
    row = lax.broadcasted_iota(jnp.int32, gr.shape, 1)
    grs[...] = jnp.where(row < M_HEADS, gr, _log_sigmoid(gr))

    ti = lax.broadcasted_iota(jnp.int32, (CHUNK, CHUNK), 0)
    si = lax.broadcasted_iota(jnp.int32, (CHUNK, CHUNK), 1)
    causal = si <= ti
    tri = causal.astype(F32)
    tri_t = (ti <= si).astype(F32)
    lane256 = lax.broadcasted_iota(jnp.int32, (CHUNK, HW), 1)
    ones_blk = jnp.ones((CHUNK, LANES), BF16)
    ones_sq = jnp.ones((LANES, LANES), BF16)

    def twice(a):
        return jnp.concatenate([a, a], axis=1)

    def rep(col):
        return jnp.broadcast_to(col, (CHUNK, LANES))

    def stage1(c):
        r0 = pl.multiple_of(c * CHUNK, CHUNK)
        qc = q_sc[pl.ds(r0, CHUNK), :]
        kc = k_sc[pl.ds(r0, CHUNK), :]
        vc = v_ref[0, pl.ds(r0, CHUNK), :]
        gcc = gcs[pl.ds(r0, CHUNK), :]
        grr = grs[c]
        b_c = jnp.dot(tri, gcc, preferred_element_type=F32, precision=HIGHEST)
        b_r = jnp.dot(grr, tri_t, preferred_element_type=F32, precision=HIGHEST)
        qstack = jnp.concatenate(
            [jnp.where((lane256 // M_QK_DIM) == h, qc, jnp.zeros_like(qc)) for h in range(M_HEADS)], axis=0)
        kstack = jnp.concatenate(
            [jnp.where((lane256 // M_QK_DIM) == h, kc, jnp.zeros_like(kc)) for h in range(M_HEADS)], axis=0)
        s_all = lax.dot_general(qstack, kc, (((1,), (1,)), ((), ())), preferred_element_type=F32)
        heads, vws = [], []
        for h in range(M_HEADS):
            bcol = rep(b_c[:, M_HEADS + h:M_HEADS + h + 1])
            icol = rep(gcc[:, h:h + 1])
            brow = b_r[M_HEADS + h:M_HEADS + h + 1, :]
            irow = grr[h:h + 1, :]
            b_last = bcol[CHUNK - 1:CHUNK, :]
            dlog = jnp.where(causal, bcol[:, :CHUNK] - brow + irow, -jnp.inf)
            m_intra = jnp.max(dlog, axis=-1, keepdims=True)
            vaug = jnp.concatenate([vc[:, h * M_V_DIM:(h + 1) * M_V_DIM], ones_blk], axis=1)
            a_col = b_last - bcol + icol
            m_loc = jnp.max(a_col, axis=0, keepdims=True)
            w_col = jnp.exp(a_col - m_loc)
            vws.append((vaug.astype(F32) * twice(w_col)).astype(BF16))
            heads.append((bcol, b_last, dlog, m_intra, vaug, m_loc))
        c_loc = lax.dot_general(kstack, jnp.concatenate(vws, axis=0), (((0,), (0,)), ((), ())),
                                preferred_element_type=F32)
        return r0, qstack, s_all, heads, c_loc

    def stage2(sts):
        items = []
        m_run = [msc[h:h + 1, :] for h in range(M_HEADS)]
        for r0, qstack, s_all, heads, c_loc in sts:
            i_all = _dot(qstack, cst[...].astype(BF16))
            for h in range(M_HEADS):
                bcol, b_last, dlog, m_intra, vaug, m_loc = heads[h]
                ks = slice(h * M_QK_DIM, (h + 1) * M_QK_DIM)
                m_prev = m_run[h]
                m_new = jnp.maximum(b_last + m_prev, m_loc)
                decay = jnp.exp(b_last + m_prev - m_new)
                fresh = jnp.exp(m_loc - m_new)
                cst[ks, :] = twice(decay) * cst[ks, :] + twice(fresh) * c_loc[ks, :]
                m_run[h] = m_new
                items.append((r0, h, bcol, dlog, m_intra, vaug, m_prev, s_all, i_all))
        for h in range(M_HEADS):
            msc[h:h + 1, :] = m_run[h]
        prods = []
        for r0, h, bcol, dlog, m_intra, vaug, m_prev, s_all, i_all in items:
            rs = slice(h * CHUNK, (h + 1) * CHUNK)
            m_inter = bcol + m_prev
            m_t = jnp.maximum(m_inter, m_intra)
            d_w = jnp.exp(dlog - m_t[:, :CHUNK])
            inter_w = jnp.exp(m_inter - m_t)
            p = (s_all[rs, :] * d_w).astype(BF16)
            prods.append((_dot(p, vaug) + twice(inter_w) * i_all[rs, :], m_t))
        normed = []
        for r, m_t in prods:
            hv = r[:, :M_V_DIM] / jnp.maximum(jnp.abs(r[:, M_V_DIM:]), jnp.exp(-m_t))
            hh = hv * hv
            hh_hi = hh.astype(BF16)
            hh_lo = (hh - hh_hi.astype(F32)).astype(BF16)
            normed.append((hv, _dot(hh_hi, ones_sq) + _dot(hh_lo, ones_sq)))
        for (r0, h, *_), (hv, ssq) in zip(items, normed):
            hs = slice(h * M_V_DIM, (h + 1) * M_V_DIM)
            hn = hv * lax.rsqrt(ssq * (1.0 / M_V_DIM) + EPS) * ng_ref[:, hs]
            og = o_ref[0, pl.ds(r0, CHUNK), hs].astype(F32)
            out_ref[0, pl.ds(r0, CHUNK), hs] = (hn * jax.nn.sigmoid(og)).astype(BF16)

    def chunk_group(cg, carry):
        stage2([stage1(CPT * cg + k) for k in range(CPT)])
        return carry

    lax.fori_loop(0, GCH // CPT, chunk_group, 0)


def _mlstm(p_all, gates_c, gates_r, bias_c, bias_r, conv_w, conv_b, ng):
    B, S, _ = p_all.shape
    T = GCH * CHUNK
    return pl.pallas_call(
        _mlstm_kernel,
        grid=(B, S // T),
        in_specs=[pl.BlockSpec((1, T, 512), lambda b, g: (b, g, 0)),
                  pl.BlockSpec((1, T, 512), lambda b, g: (b, g, 1)),
                  pl.BlockSpec((1, T, 512), lambda b, g: (b, g, 2)),
                  pl.BlockSpec((1, T, GW), lambda b, g: (b, g, 0)),
                  pl.BlockSpec((1, GCH, SUBLANES, CHUNK), lambda b, g: (b, g, 0, 0)),
                  pl.BlockSpec((1, GW), lambda b, g: (0, 0)),
                  pl.BlockSpec((SUBLANES, 1), lambda b, g: (0, 0)),
                  pl.BlockSpec((CONV_WIDTH, 512), lambda b, g: (0, 0)),
                  pl.BlockSpec((1, 512), lambda b, g: (0, 0)),
                  pl.BlockSpec((1, 512), lambda b, g: (0, 0))],
        out_specs=pl.BlockSpec((1, T, 512), lambda b, g: (b, g, 0)),
        out_shape=jax.ShapeDtypeStruct((B, S, 512), BF16),
        scratch_shapes=[pltpu.VMEM((T + SUBLANES, 512), F32),
                        pltpu.VMEM((T, 256), BF16),
                        pltpu.VMEM((T, 256), BF16),
                        pltpu.VMEM((T, GW), F32),
                        pltpu.VMEM((GCH, SUBLANES, CHUNK), F32),
                        pltpu.VMEM((M_HEADS * M_QK_DIM, 2 * M_V_DIM), F32),
                        pltpu.VMEM((SUBLANES, LANES), F32)],
        compiler_params=_cparams(("arbitrary", "arbitrary")),
        name="mlstm",
    )(p_all, p_all, p_all, gates_c, gates_r, bias_c, bias_r, conv_w, conv_b, ng)


HPS = 2


def _attn_kernel(lq1_ref, lk1_ref, lq2_ref, lk2_ref, q_ref, k_ref, v_ref, ng_ref, o_ref, *scratch):
    i = pl.program_id(2)
    qs_scs, m_scs, acc_scs, sa_scs, sb_scs = (scratch[n * HPS:(n + 1) * HPS] for n in range(5))
    heads = range(HPS)
    lanes_of = [slice(hh * LANES, (hh + 1) * LANES) for hh in heads]
    ones_blk = jnp.ones((TQ, LANES), BF16)

    for hh in heads:
        q = q_ref[0, :, lanes_of[hh]]
        lane = lax.broadcasted_iota(jnp.int32, q.shape, 1)
        qs_scs[hh][0:TQ, :] = jnp.where(lane < A_HEAD_DIM, q, jnp.zeros_like(q))
        qs_scs[hh][TQ:2 * TQ, :] = jnp.where(lane >= A_HEAD_DIM, q, jnp.zeros_like(q))
        m_scs[hh][...] = jnp.full(m_scs[hh].shape, -jnp.inf, F32)
        acc_scs[hh][...] = jnp.zeros_like(acc_scs[hh])

    def scores(hh, j):
        k = k_ref[0, pl.ds(pl.multiple_of(j * TQ, TQ), TQ), lanes_of[hh]]
        return lax.dot_general(qs_scs[hh][...], k, (((1,), (1,)), ((), ())), preferred_element_type=F32)

    def diag_scores(hh):
        ri = lax.broadcasted_iota(jnp.int32, (2 * TQ, TQ), 0)
        ci = lax.broadcasted_iota(jnp.int32, (2 * TQ, TQ), 1)
        visible = (ci // CHUNK) <= ((ri & (TQ - 1)) // CHUNK)
        return jnp.where(visible, scores(hh, i), -jnp.inf)

    def accumulate(hh, s_ref, j):
        s = s_ref[...]
        v = v_ref[0, pl.ds(pl.multiple_of(j * TQ, TQ), TQ), lanes_of[hh]]
        m_old = m_scs[hh][...]
        m_new = jnp.maximum(m_old, jnp.max(s, axis=-1, keepdims=True))
        alpha = jnp.exp(m_old - m_new)
        p = jnp.exp(s - jnp.concatenate([m_new] * (TQ // LANES), axis=1))
        pv = _dot(p.astype(BF16), jnp.concatenate([v, ones_blk], axis=1))
        acc_scs[hh][...] = jnp.concatenate([alpha, alpha], axis=1) * acc_scs[hh][...] + pv
        m_scs[hh][...] = m_new

    def kv_of(t):
        return jnp.where(t == 0, i, t - 1)

    for hh in heads:
        sa_scs[hh][...] = diag_scores(hh)
    pairs = i // 2

    def body(u, carry):
        t = 2 * u
        for hh in heads:
            sb_scs[hh][...] = scores(hh, t)
            accumulate(hh, sa_scs[hh], kv_of(t))
        for hh in heads:
            sa_scs[hh][...] = scores(hh, t + 1)
            accumulate(hh, sb_scs[hh], t)
        return carry

    lax.fori_loop(0, pairs, body, 0)
    t0 = 2 * pairs

    @pl.when(i == t0)
    def _():
        for hh in heads:
            accumulate(hh, sa_scs[hh], kv_of(t0))

    @pl.when(i > t0)
    def _():
        for hh in heads:
            sb_scs[hh][...] = scores(hh, t0)
            accumulate(hh, sa_scs[hh], kv_of(t0))
        for hh in heads:
            accumulate(hh, sb_scs[hh], t0)

    lam = (jnp.exp(jnp.sum(lq1_ref[...] * lk1_ref[...], axis=-1, keepdims=True))
           - jnp.exp(jnp.sum(lq2_ref[...] * lk2_ref[...], axis=-1, keepdims=True)) + LAMBDA_INIT)
    for hh in heads:
        acc = acc_scs[hh][...]
        o = acc[:, :LANES] / acc[:, LANES:]
        a = o[0:TQ, :] - lam * o[TQ:2 * TQ, :]
        y = a * lax.rsqrt(jnp.mean(a * a, axis=-1, keepdims=True) + EPS) * ng_ref[:, lanes_of[hh]]
        o_ref[0, :, lanes_of[hh]] = (y * (1.0 - LAMBDA_INIT)).astype(BF16)


def _attn(p_all, lq1, lk1, lq2, lk2, ng):
    B, S, _ = p_all.shape
    hw = HPS * LANES
    nsec = 512 // hw
    lam_spec = pl.BlockSpec((1, A_HEAD_DIM), lambda b, h, i: (0, 0))
    per_head = lambda shape, dtype: [pltpu.VMEM(shape, dtype) for _ in range(HPS)]
    return pl.pallas_call(
        _attn_kernel,
        grid=(B, A_HEADS // HPS, S // TQ),
        in_specs=[lam_spec, lam_spec, lam_spec, lam_spec,
                  pl.BlockSpec((1, TQ, hw), lambda b, h, i: (b, i, 3 * nsec + h)),
                  pl.BlockSpec((1, S, hw), lambda b, h, i: (b, 0, 4 * nsec + h)),
                  pl.BlockSpec((1, S, hw), lambda b, h, i: (b, 0, 5 * nsec + h)),
                  pl.BlockSpec((1, hw), lambda b, h, i: (0, h))],
        out_specs=pl.BlockSpec((1, TQ, hw), lambda b, h, i: (b, i, h)),
        out_shape=jax.ShapeDtypeStruct((B, S, 512), BF16),
        scratch_shapes=(per_head((2 * TQ, LANES), BF16) + per_head((2 * TQ, LANES), F32)
                        + per_head((2 * TQ, 2 * LANES), F32) + per_head((2 * TQ, TQ), F32)
                        + per_head((2 * TQ, TQ), F32)),
        compiler_params=_cparams(("arbitrary", "arbitrary", "arbitrary")),
        name="attn",
    )(lq1, lk1, lq2, lk2, p_all, p_all, p_all, ng)


def _outproj_kernel(hm_ref, ha_ref, x_ref, mod_ref, g2_ref, wo_ref, wrh_ref, wrl_ref, br_ref, ltri_ref,
                    x1_ref, h2_ref, idx_ref, gate_ref, rank_ref, cnt_ref, cnt_sc):
    @pl.when((pl.program_id(0) == 0) & (pl.program_id(1) == 0))
    def _():
        cnt_sc[...] = jnp.zeros_like(cnt_sc)

    hcat = jnp.concatenate([hm_ref[0], ha_ref[0]], axis=1)
    mix = _dot(hcat, wo_ref[...])
    gate1 = mod_ref[0, 2:3, :]
    shift2 = mod_ref[0, 3:4, :]
    scale2 = mod_ref[0, 4:5, :]
    x1 = x_ref[0] + gate1 * mix
    x1_ref[0] = x1
    ms = jnp.mean(x1 * x1, axis=-1, keepdims=True)
    h2 = (x1 * lax.rsqrt(ms + EPS) * g2_ref[...]) * (1.0 + scale2) + shift2
    tm = h2.shape[0]
    for s in range(SUBLANES):
        h2_ref[pl.ds(s, tm, stride=SUBLANES), :] = h2[:, s * LANES:(s + 1) * LANES]
    hb = h2.astype(BF16)
    hl = (h2 - hb.astype(F32)).astype(BF16)
    logits = _dot(hb, wrh_ref[...]) + _dot(hl, wrh_ref[...]) + _dot(hb, wrl_ref[...]) + br_ref[...]
    lane = lax.broadcasted_iota(jnp.int32, logits.shape, 1).astype(F32)
    vals, idxs = [], []
    work = logits
    for _ in range(TOP_K):
        mx = jnp.max(work, axis=-1, keepdims=True)
        ix = jnp.min(jnp.where(work == mx, lane, float(LANES)), axis=-1, keepdims=True)
        vals.append(mx)
        idxs.append(ix)
        work = jnp.where(lane == ix, -jnp.inf, work)
    es = [jnp.exp(v - vals[0]) for v in vals]
    tot = es[0] + es[1] + es[2] + es[3]
    gsel = jnp.zeros_like(logits)
    isel = jnp.zeros_like(logits)
    for k in range(TOP_K):
        gsel = jnp.where(lane == float(k), es[k] / tot, gsel)
        isel = jnp.where(lane == float(k), idxs[k], isel)
    gate_ref[0] = gsel[:, :TOP_K]
    idx_ref[0] = isel[:, :TOP_K].astype(jnp.int32)
    chosen = [lane == ix for ix in idxs]
    multi = jnp.zeros_like(logits)
    for ch in chosen:
        multi = jnp.where(ch, 1.0, multi)
    before = _dot(ltri_ref[...], multi.astype(BF16)) + cnt_sc[...]
    rsel = jnp.zeros_like(logits)
    for k in range(TOP_K):
        rk = jnp.sum(jnp.where(chosen[k], before, 0.0), axis=-1, keepdims=True)
        rsel = jnp.where(lane == float(k), rk, rsel)
    rank_ref[0] = rsel[:, :TOP_K].astype(jnp.int32)
    cnt_sc[...] = cnt_sc[...] + jnp.sum(multi, axis=0, keepdims=True)
    cnt_ref[...] = cnt_sc[...]


def _outproj(hm, ha, x, mod3, g2, wo, wr_hi, wr_lo, br, ltri):
    B, S, D = x.shape
    nt = S // TM
    return pl.pallas_call(
        _outproj_kernel,
        grid=(B, nt),
        in_specs=[pl.BlockSpec((1, TM, 512), lambda b, s: (b, s, 0)),
                  pl.BlockSpec((1, TM, 512), lambda b, s: (b, s, 0)),
                  pl.BlockSpec((1, TM, D), lambda b, s: (b, s, 0)),
                  pl.BlockSpec((1, 6, D), lambda b, s: (b, 0, 0)),
                  pl.BlockSpec((1, D), lambda b, s: (0, 0)),
                  pl.BlockSpec((D, D), lambda b, s: (0, 0)),
                  pl.BlockSpec((D, LANES), lambda b, s: (0, 0)),
                  pl.BlockSpec((D, LANES), lambda b, s: (0, 0)),
                  pl.BlockSpec((1, LANES), lambda b, s: (0, 0)),
                  pl.BlockSpec((TM, TM), lambda b, s: (0, 0))],
        out_specs=[pl.BlockSpec((1, TM, D), lambda b, s: (b, s, 0)),
                   pl.BlockSpec((TM * SUBLANES, LANES), lambda b, s: (b * nt + s, 0)),
                   pl.BlockSpec((1, TM, TOP_K), lambda b, s: (b, s, 0)),
                   pl.BlockSpec((1, TM, TOP_K), lambda b, s: (b, s, 0)),
                   pl.BlockSpec((1, TM, TOP_K), lambda b, s: (b, s, 0)),
                   pl.BlockSpec((1, LANES), lambda b, s: (0, 0))],
        out_shape=[jax.ShapeDtypeStruct((B, S, D), F32),
                   jax.ShapeDtypeStruct((B * S * SUBLANES, LANES), F32),
                   jax.ShapeDtypeStruct((B, S, TOP_K), jnp.int32),
                   jax.ShapeDtypeStruct((B, S, TOP_K), F32),
                   jax.ShapeDtypeStruct((B, S, TOP_K), jnp.int32),
                   jax.ShapeDtypeStruct((1, LANES), F32)],
        scratch_shapes=[pltpu.VMEM((1, LANES), F32)],
        compiler_params=_cparams(("arbitrary", "arbitrary")),
        name="outproj",
    )(hm, ha, x, mod3, g2, wo, wr_hi, wr_lo, br, ltri)


DMA_UNROLL = 8


def _row_copy(src, dst, src_row, dst_row, sem):
    def tile_start(row):
        start = row * SUBLANES
        return start if isinstance(start, int) else pl.multiple_of(start, SUBLANES)

    return pltpu.make_async_copy(
        src.at[pl.ds(tile_start(src_row), SUBLANES), :],
        dst.at[pl.ds(tile_start(dst_row), SUBLANES), :],
        sem)


def _wait_rows(hbm, n_rows, sem):
    span = hbm.at[pl.ds(0, n_rows * SUBLANES), :]
    pltpu.make_async_copy(span, span, sem).wait()


def _dispatch_kernel(pad_ref, pos_ref, h2_ref, wgu_ref, wdn_ref, psel_ref, xs_hbm, wgu_out, wdn_out,
                     zero_sc, sem, sem_pad, *, n_pad, steps_per_expert):
    i = pl.program_id(0)
    n = TOP_K * TM

    @pl.when(i == 0)
    def _():
        zero_sc[...] = jnp.zeros_like(zero_sc)

        def issue_pad(g, carry):
            for u in range(DMA_UNROLL):
                _row_copy(zero_sc, xs_hbm, 0, pad_ref[g * DMA_UNROLL + u], sem_pad).start(priority=u % 2)
            return carry

        lax.fori_loop(0, n_pad // DMA_UNROLL, issue_pad, 0)

    def issue(g, carry):
        for u in range(DMA_UNROLL):
            j = g * DMA_UNROLL + u
            tok = g * (DMA_UNROLL // TOP_K) + u // TOP_K
            _row_copy(h2_ref, xs_hbm, tok, pos_ref[0, 0, j], sem).start(priority=u % 2)
        return carry

    lax.fori_loop(0, n // DMA_UNROLL, issue, 0)

    @pl.when(i % steps_per_expert == 0)
    def _():
        half = wgu_out.shape[2] // 2
        for cblk in range(wgu_out.shape[2] // 256):
            blk = _dot(wgu_ref[0, :, cblk * 256:(cblk + 1) * 256].astype(BF16), psel_ref[...])
            wgu_out[0, :, cblk * LANES:(cblk + 1) * LANES] = blk[:, :LANES].astype(BF16)
            wgu_out[0, :, half + cblk * LANES:half + (cblk + 1) * LANES] = blk[:, LANES:].astype(BF16)
        wdn_out[0] = wdn_ref[0].astype(BF16)

    _wait_rows(xs_hbm, n, sem)

    @pl.when(i == 0)
    def _():
        _wait_rows(xs_hbm, n_pad, sem_pad)


def _dispatch(pad_rows, pos_tiles, h2_rows, w_gu, w_down, psel, n_rows):
    n_tiles = pos_tiles.shape[0]
    n_pad = pad_rows.shape[0]
    E, D, F2 = w_gu.shape
    assert n_tiles % E == 0, "dispatch steps must be a multiple of the expert count"
    spe = n_tiles // E
    wmap = lambda i, p: (i // spe, 0, 0)
    return pl.pallas_call(
        functools.partial(_dispatch_kernel, n_pad=n_pad, steps_per_expert=spe),
        grid_spec=pltpu.PrefetchScalarGridSpec(
            num_scalar_prefetch=1,
            grid=(n_tiles,),
            in_specs=[pl.BlockSpec((1, 1, TOP_K * TM), lambda i, p: (i, 0, 0), memory_space=pltpu.SMEM),
                      pl.BlockSpec((TM * SUBLANES, LANES), lambda i, p: (i, 0)),
                      pl.BlockSpec((1, D, F2), wmap),
                      pl.BlockSpec((1, F2 // 2, D), wmap),
                      pl.BlockSpec((256, 256), lambda i, p: (0, 0))],
            out_specs=[pl.BlockSpec(memory_space=pl.ANY),
                       pl.BlockSpec((1, D, F2), wmap),
                       pl.BlockSpec((1, F2 // 2, D), wmap)],
            scratch_shapes=[pltpu.VMEM((SUBLANES, LANES), F32),
                            pltpu.SemaphoreType.DMA(()),
                            pltpu.SemaphoreType.DMA(())]),
        out_shape=[jax.ShapeDtypeStruct((n_rows * SUBLANES, LANES), F32),
                   jax.ShapeDtypeStruct((E, D, F2), BF16),
                   jax.ShapeDtypeStruct((E, F2 // 2, D), BF16)],
        compiler_params=_cparams(("arbitrary",)),
        name="dispatch",
    )(pad_rows, pos_tiles, h2_rows, w_gu, w_down, psel)


def _expert_kernel(be_ref, nused_ref, xs_ref, wgu_ref, bg_ref, bu_ref, wd_ref, bd_ref, ys_ref):
    i = pl.program_id(0)

    @pl.when(i < nused_ref[0])
    def _():
        x = jnp.concatenate([xs_ref[pl.ds(s, RB, stride=SUBLANES), :] for s in range(SUBLANES)], axis=1).astype(BF16)
        gu = _dot(x, wgu_ref[0])
        half = gu.shape[1] // 2
        gate = jnp.minimum(gu[:, :half] + bg_ref[0], SWIGLU_LIMIT)
        up = jnp.clip(gu[:, half:] + bu_ref[0], -SWIGLU_LIMIT, SWIGLU_LIMIT)
        act = (up + 1.0) * (gate * jax.nn.sigmoid(SWIGLU_ALPHA * gate))
        out = _dot(act.astype(BF16), wd_ref[0]) + bd_ref[0]
        for s in range(SUBLANES):
            ys_ref[pl.ds(s, RB, stride=SUBLANES), :] = out[:, s * LANES:(s + 1) * LANES]

    @pl.when(i >= nused_ref[0])
    def _():
        ys_ref[...] = jnp.zeros_like(ys_ref)


def _experts(block_expert, nused, xs, wgu, bg, bu, wd, bd, n_blocks):
    D = wgu.shape[1]
    F = wgu.shape[2] // 2
    wmap = lambda i, be, n: (be[i], 0, 0)
    return pl.pallas_call(
        _expert_kernel,
        grid_spec=pltpu.PrefetchScalarGridSpec(
            num_scalar_prefetch=2,
            grid=(n_blocks,),
            in_specs=[pl.BlockSpec((RB * SUBLANES, LANES), lambda i, be, n: (i, 0)),
                      pl.BlockSpec((1, D, 2 * F), wmap),
                      pl.BlockSpec((1, 1, F), wmap),
                      pl.BlockSpec((1, 1, F), wmap),
                      pl.BlockSpec((1, F, D), wmap),
                      pl.BlockSpec((1, 1, D), wmap)],
            out_specs=pl.BlockSpec((RB * SUBLANES, LANES), lambda i, be, n: (i, 0))),
        out_shape=jax.ShapeDtypeStruct((n_blocks * RB * SUBLANES, LANES), F32),
        compiler_params=_cparams(("arbitrary",)),
        name="experts",
    )(block_expert, nused, xs, wgu, bg, bu, wd, bd)


CB_TRIPS = 32


def _combine_kernel(pos_ref, posn_ref, ys_hbm, x1_ref, mod_ref, gate_ref, o_ref, buf_a, buf_b, sem_a, sem_b):
    s = pl.program_id(0)
    last = pl.num_programs(0) - 1
    n = TOP_K * TC
    per_trip = n // CB_TRIPS
    ts = TC // CB_TRIPS
    gate2 = mod_ref[0, 5:6, :]

    def issue_group(p_ref, buf, sem, g):
        for u in range(per_trip):
            j = g * per_trip + u
            _row_copy(ys_hbm, buf, p_ref[0, 0, j], j, sem).start(priority=u % 2)

    def sum_strip(buf, g):
        t0 = pl.multiple_of(g * ts, ts)
        gates = gate_ref[0, pl.ds(t0, ts), :]
        y = None
        for k in range(TOP_K):
            base = (k * TC + t0) * SUBLANES
            rows = jnp.concatenate(
                [buf[pl.ds(base + sl, ts, stride=SUBLANES), :] for sl in range(SUBLANES)], axis=1)
            term = gates[:, k:k + 1] * rows
            y = term if y is None else y + term
        o_ref[0, pl.ds(t0, ts), :] = x1_ref[0, pl.ds(t0, ts), :] + gate2 * y

    def step(cur_buf, cur_sem, nxt_buf, nxt_sem):
        @pl.when(s == 0)
        def _():
            def first(g, carry):
                issue_group(pos_ref, cur_buf, cur_sem, g)
                return carry

            lax.fori_loop(0, CB_TRIPS, first, 0)

        _wait_rows(ys_hbm, n, cur_sem)

        @pl.when(s < last)
        def _():
            def fused(g, carry):
                issue_group(posn_ref, nxt_buf, nxt_sem, g)
                sum_strip(cur_buf, g)
                return carry

            lax.fori_loop(0, CB_TRIPS, fused, 0)

        @pl.when(s == last)
        def _():
            def tail(g, carry):
                sum_strip(cur_buf, g)
                return carry

            lax.fori_loop(0, CB_TRIPS, tail, 0)

    @pl.when(s % 2 == 0)
    def _():
        step(buf_a, sem_a, buf_b, sem_b)

    @pl.when(s % 2 == 1)
    def _():
        step(buf_b, sem_b, buf_a, sem_a)


def _combine(pos_tiles, ys, x1, mod3, gates):
    B, S, D = x1.shape
    nt = S // TC
    n_steps = B * nt
    pos_spec = lambda off: pl.BlockSpec((1, 1, TOP_K * TC), lambda s: (jnp.minimum(s + off, n_steps - 1), 0, 0),
                                        memory_space=pltpu.SMEM)
    return pl.pallas_call(
        _combine_kernel,
        grid=(n_steps,),
        in_specs=[pos_spec(0), pos_spec(1),
                  pl.BlockSpec(memory_space=pl.ANY),
                  pl.BlockSpec((1, TC, D), lambda s: (s // nt, s % nt, 0)),
                  pl.BlockSpec((1, 6, D), lambda s: (s // nt, 0, 0)),
                  pl.BlockSpec((1, TC, TOP_K), lambda s: (s // nt, s % nt, 0))],
        out_specs=pl.BlockSpec((1, TC, D), lambda s: (s // nt, s % nt, 0)),
        out_shape=jax.ShapeDtypeStruct((B, S, D), F32),
        scratch_shapes=[pltpu.VMEM((TOP_K * TC * SUBLANES, LANES), F32),
                        pltpu.VMEM((TOP_K * TC * SUBLANES, LANES), F32),
                        pltpu.SemaphoreType.DMA(()),
                        pltpu.SemaphoreType.DMA(())],
        compiler_params=_cparams(("arbitrary",)),
        name="combine",
    )(pos_tiles, pos_tiles, ys, x1, mod3, gates)


def _lookup(table, idx):
    hit = idx[..., None] == jnp.arange(table.shape[0], dtype=jnp.int32)
    return jnp.sum(jnp.where(hit, table, 0), axis=-1).astype(jnp.int32)


def _routing_tables(top_idx, rank, counts, n_tokens):
    n_slots = n_tokens * TOP_K
    n_rows = n_slots + N_EXPERTS * RB
    n_blocks = n_rows // RB
    sizes = counts[0, :N_EXPERTS].astype(jnp.int32)
    padded = ((sizes + RB - 1) // RB) * RB
    padded_end = jnp.cumsum(padded).astype(jnp.int32)
    padded_start = padded_end - padded
    pos = _lookup(padded_start, top_idx) + rank
    n_used = padded_end[-1] // RB
    blk = jnp.arange(n_blocks, dtype=jnp.int32)
    block_expert = jnp.minimum(jnp.sum(padded_end[None, :] <= (blk * RB)[:, None], axis=1), N_EXPERTS - 1)
    last_e = jnp.minimum(jnp.sum(padded_end <= (n_used - 1) * RB), N_EXPERTS - 1)
    block_expert = jnp.where(blk < n_used, block_expert, last_e).astype(jnp.int32)
    pad_cnt = padded - sizes
    pad_end = jnp.cumsum(pad_cnt).astype(jnp.int32)
    j = jnp.arange(N_EXPERTS * RB, dtype=jnp.int32)
    e_j = jnp.sum(pad_end[None, :] <= j[:, None], axis=1).astype(jnp.int32)
    in_group = e_j < N_EXPERTS
    e_c = jnp.minimum(e_j, N_EXPERTS - 1)
    group_row = _lookup(padded_start + sizes - (pad_end - pad_cnt), e_c) + j
    tail_row = padded_end[-1] + (j - pad_end[-1])
    pad_rows = jnp.where(in_group, group_row, tail_row).astype(jnp.int32)
    return pos, pad_rows, block_expert, n_used.reshape(1).astype(jnp.int32), n_rows, n_blocks


def kernel(x, c, w_ada, b_ada, norm1_g, w_in, conv_w, conv_b, b_igate, b_fgate, mlstm_norm_g, q_norm_g, k_norm_g,
           lambda_q1, lambda_k1, lambda_q2, lambda_k2, diff_norm_g, w_out, norm2_g, w_router, b_router, w_gu, b_gu,
           w_down, b_down):
    B, S, D = x.shape
    N = B * S
    l = 0
    o_mq, o_mv, o_mo = 0, 512, 1024
    o_mi, o_mf = 1536, 1540
    o_aq, o_ak, o_av = 1544, 2056, 2568

    wi = w_in[l]
    w_main = jnp.concatenate([wi[:, o_mq:o_mq + 512], wi[:, o_mv:o_mv + 512], wi[:, o_mo:o_mo + 512],
                              wi[:, o_aq:o_aq + 512], wi[:, o_ak:o_ak + 512], wi[:, o_av:o_av + 512]],
                             axis=1).astype(BF16)
    def hi_lo_lanes(w, width):
        hi = w.astype(BF16)
        lo = (w - hi.astype(F32)).astype(BF16)
        return jnp.pad(jnp.concatenate([hi, lo], axis=1), ((0, 0), (0, width - 2 * w.shape[1])))

    wg2 = hi_lo_lanes(wi[:, o_mi:o_mi + 2 * M_HEADS], GW)
    pos_ids = jnp.arange(S, dtype=F32)
    inv_freq = ROPE_THETA ** (-jnp.arange(0, A_HEAD_DIM, 2, dtype=F32) / A_HEAD_DIM)
    ang = pos_ids[:, None] * inv_freq[None, :]
    cos_h, sin_h = jnp.cos(ang), jnp.sin(ang)
    cos_t = jnp.concatenate([cos_h, cos_h, cos_h, cos_h], axis=1)
    sin_t = jnp.concatenate([-sin_h, sin_h, -sin_h, sin_h], axis=1)
    qg = jnp.tile(q_norm_g[l], 512 // A_HEAD_DIM).reshape(1, 512)
    kg = jnp.tile(k_norm_g[l], 512 // A_HEAD_DIM).reshape(1, 512)
    gid = jnp.arange(256) // A_HEAD_DIM
    bd = (gid[:, None] == gid[None, :]).astype(BF16)
    bias_c = jnp.pad(jnp.concatenate([b_igate[l], b_fgate[l]]), (0, GW - 2 * M_HEADS)).reshape(1, GW)
    bias_r = jnp.concatenate([b_igate[l], b_fgate[l]]).reshape(2 * M_HEADS, 1)
    wr = jnp.pad(w_router[l], ((0, 0), (0, LANES - N_EXPERTS)))
    wr_hi = wr.astype(BF16)
    wr_lo = (wr - wr_hi.astype(F32)).astype(BF16)
    br = jnp.pad(b_router[l], (0, LANES - N_EXPERTS), constant_values=-jnp.inf).reshape(1, LANES)
    col = jnp.arange(256)
    psel = (jnp.where(col % 2 == 0, col // 2, LANES + col // 2)[:, None] == col[None, :]).astype(BF16)
    ltri = (jnp.arange(TM)[None, :] < jnp.arange(TM)[:, None]).astype(BF16)
    bgate = b_gu[l][:, None, 0::2]
    bup = b_gu[l][:, None, 1::2]
    bdn = b_down[l][:, None, :]

    mod3 = _ada(c, w_ada[l], b_ada[l]).reshape(B, 6, D)
    p_all, gates_c = _inproj(x, mod3, norm1_g[l].reshape(1, D), w_main, wg2, cos_t, sin_t, qg, kg, bd)
    gates_r = gates_c[:, :, :2 * M_HEADS].reshape(B, S // CHUNK, CHUNK, 2 * M_HEADS).transpose(0, 1, 3, 2)
    hm = _mlstm(p_all, gates_c, gates_r, bias_c, bias_r, conv_w[l], conv_b[l].reshape(1, 512),
                mlstm_norm_g[l].reshape(1, 512))
    ha = _attn(p_all, lambda_q1[l].reshape(1, -1), lambda_k1[l].reshape(1, -1), lambda_q2[l].reshape(1, -1),
               lambda_k2[l].reshape(1, -1), diff_norm_g[l].reshape(1, 512))

    x1, h2_rows, top_idx, gates, rank, counts = _outproj(hm, ha, x, mod3, norm2_g[l].reshape(1, D),
                                                         w_out[l].astype(BF16), wr_hi, wr_lo, br, ltri)

    pos, pad_rows, block_expert, n_used, n_rows, n_blocks = _routing_tables(top_idx, rank, counts, N)
    xs, wgu, wdn = _dispatch(pad_rows, pos.reshape(N // TM, 1, TM * TOP_K), h2_rows, w_gu[l], w_down[l], psel, n_rows)
    ys = _experts(block_expert, n_used, xs, wgu, bgate, bup, wdn, bdn, n_blocks)
    pos_tiles = pos.reshape(N // TC, TC, TOP_K).transpose(0, 2, 1).reshape(N // TC, 1, TOP_K * TC)
    return _combine(pos_tiles, ys, x1, mod3, gates)
```

```python
import functools
import math

import jax
import jax.numpy as jnp
from jax import lax
from jax.experimental import pallas as pl
from jax.experimental.pallas import tpu as pltpu

F32 = jnp.float32
BF16 = jnp.bfloat16
HIGHEST = lax.Precision.HIGHEST

LANES = 128
SUBLANES = 8
VMEM_LIMIT = 48 * 1024 * 1024

CHUNK = 64
M_HEADS = 4
M_QK_DIM = 64
M_V_DIM = 128
CONV_WIDTH = 4
A_HEADS = 4
A_HEAD_DIM = 64
A_V_DIM = 128
ROPE_THETA = 10000.0
N_EXPERTS = 32
TOP_K = 4
SWIGLU_ALPHA = 1.702
SWIGLU_LIMIT = 7.0
EPS = 1e-6
LAMBDA_INIT = 0.8 - 0.6 * math.exp(-0.3 * 0)

TM = 512
GCH = 8
CPT = 4
TQ = 512
RB = 512
TC = 512
GW = 128


def _dot(a, b):
    return jnp.dot(a, b, preferred_element_type=F32)


def _cparams(sem):
    return pltpu.CompilerParams(dimension_semantics=sem, vmem_limit_bytes=VMEM_LIMIT)


def _ada_kernel(c_ref, w_ref, b_ref, o_ref):
    c = c_ref[...]
    cond = c * jax.nn.sigmoid(c)
    o_ref[...] = jnp.dot(cond, w_ref[...], preferred_element_type=F32, precision=HIGHEST) + b_ref[...]


def _ada(c, w, b):
    B, D = c.shape
    n = w.shape[1]
    tn = 1024
    return pl.pallas_call(
        _ada_kernel,
        grid=(n // tn,),
        in_specs=[pl.BlockSpec((B, D), lambda j: (0, 0)),
                  pl.BlockSpec((D, tn), lambda j: (0, j)),
                  pl.BlockSpec((1, tn), lambda j: (0, j))],
        out_specs=pl.BlockSpec((B, tn), lambda j: (0, j)),
        out_shape=jax.ShapeDtypeStruct((B, n), F32),
        compiler_params=_cparams(("arbitrary",)),
        name="ada",
    )(c, w, b.reshape(1, n))


def _split_dot(hi, lo, w2, n_cols):
    both = _dot(hi, w2) + _dot(lo, w2)
    return both + pltpu.roll(both, both.shape[1] - n_cols, 1)


def _inproj_kernel(x_ref, mod_ref, g1_ref, w_ref, wg2_ref, cos_ref, sin_ref, qg_ref, kg_ref,
                   bd_ref, p_ref, gates_ref):
    x = x_ref[0]
    ms = jnp.mean(x * x, axis=-1, keepdims=True)
    shift = mod_ref[0, 0:1, :]
    scale = mod_ref[0, 1:2, :]
    h = (x * lax.rsqrt(ms + EPS) * g1_ref[...]) * (1.0 + scale) + shift
    hb = h.astype(BF16)
    hl = (h - hb.astype(F32)).astype(BF16)
    gates_ref[0] = _split_dot(hb, hl, wg2_ref[...], 2 * M_HEADS)

    tm = x.shape[0]
    lane = lax.broadcasted_iota(jnp.int32, (tm, 512), 1)
    first_half = (lane & 63) < 32
    cos = jnp.concatenate([cos_ref[...]] * 4, axis=1)
    sin = jnp.concatenate([sin_ref[...]] * 4, axis=1)
    for sec in range(6):
        acc = _dot(hb, w_ref[:, sec * 512:(sec + 1) * 512])
        if sec in (3, 4):
            g = qg_ref[...] if sec == 3 else kg_ref[...]
            sq = (acc * acc).astype(BF16)
            ssq = jnp.concatenate([_dot(sq[:, c * 256:(c + 1) * 256], bd_ref[...]) for c in range(2)], axis=1)
            y = acc * lax.rsqrt(ssq * (1.0 / A_HEAD_DIM) + EPS) * g
            swapped = jnp.where(first_half, pltpu.roll(y, 512 - 32, 1), pltpu.roll(y, 32, 1))
            acc = y * cos + swapped * sin
            if sec == 3:
                acc = acc * (A_HEAD_DIM ** -0.5)
        p_ref[0, :, sec * 512:(sec + 1) * 512] = acc.astype(BF16)


def _inproj(x, mod3, g1, w_main, wg2, cos_t, sin_t, qg, kg, bd):
    B, S, D = x.shape
    nw = w_main.shape[1]
    return pl.pallas_call(
        _inproj_kernel,
        grid=(B, S // TM),
        in_specs=[pl.BlockSpec((1, TM, D), lambda b, s: (b, s, 0)),
                  pl.BlockSpec((1, 6, D), lambda b, s: (b, 0, 0)),
                  pl.BlockSpec((1, D), lambda b, s: (0, 0)),
                  pl.BlockSpec((D, nw), lambda b, s: (0, 0)),
                  pl.BlockSpec((D, GW), lambda b, s: (0, 0)),
                  pl.BlockSpec((TM, LANES), lambda b, s: (s, 0)),
                  pl.BlockSpec((TM, LANES), lambda b, s: (s, 0)),
                  pl.BlockSpec((1, 512), lambda b, s: (0, 0)),
                  pl.BlockSpec((1, 512), lambda b, s: (0, 0)),
                  pl.BlockSpec((256, 256), lambda b, s: (0, 0))],
        out_specs=[pl.BlockSpec((1, TM, nw), lambda b, s: (b, s, 0)),
                   pl.BlockSpec((1, TM, GW), lambda b, s: (b, s, 0))],
        out_shape=[jax.ShapeDtypeStruct((B, S, nw), BF16),
                   jax.ShapeDtypeStruct((B, S, GW), F32)],
        compiler_params=_cparams(("arbitrary", "arbitrary")),
        name="inproj",
    )(x, mod3, g1, w_main, wg2, cos_t, sin_t, qg, kg, bd)


def _log_sigmoid(z):
    return jnp.minimum(z, 0.0) - jnp.log1p(jnp.exp(-jnp.abs(z)))


def _mlstm_kernel(qk_ref, v_ref, o_ref, gc_ref, gr_ref, bc_ref, br_ref, cw_ref, cb_ref, ng_ref,
                  out_ref, ubuf, q_sc, k_sc, gcs, grs, cst, msc):
    g = pl.program_id(1)
    T = GCH * CHUNK
    HW = M_HEADS * M_QK_DIM

    @pl.when(g == 0)
    def _():
        cst[...] = jnp.zeros_like(cst)
        msc[...] = jnp.zeros_like(msc)
        ubuf[0:SUBLANES, :] = jnp.zeros((SUBLANES, 2 * HW), F32)

    @pl.when(g > 0)
    def _():
        ubuf[0:SUBLANES, :] = ubuf[T:T + SUBLANES, :]

    ubuf[SUBLANES:SUBLANES + T, :] = qk_ref[0].astype(F32)

    def conv_strip(c, carry):
        r0 = pl.multiple_of(c * CHUNK, CHUNK)
        strip = ubuf[pl.ds(r0, CHUNK + SUBLANES), :]
        y = cb_ref[...]
        for j in range(CONV_WIDTH):
            off = SUBLANES - (CONV_WIDTH - 1) + j
            y = y + cw_ref[j:j + 1, :] * strip[off:off + CHUNK, :]
        qk = y * jax.nn.sigmoid(y)
        q_sc[pl.ds(r0, CHUNK), :] = qk[:, :HW].astype(BF16)
        k_sc[pl.ds(r0, CHUNK), :] = (qk[:, HW:] * (M_QK_DIM ** -0.5)).astype(BF16)
        return carry

    lax.fori_loop(0, GCH, conv_strip, 0)

    gc = gc_ref[0] + bc_ref[...]
    lane = lax.broadcasted_iota(jnp.int32, gc.shape, 1)
    gcs[...] = jnp.where(lane < M_HEADS, gc, _log_sigmoid(gc))
    gr = gr_ref[0] + br_ref[...]
    row = lax.broadcasted_iota(jnp.int32, gr.shape, 1)
    grs[...] = jnp.where(row < M_HEADS, gr, _log_sigmoid(gr))

    ti = lax.broadcasted_iota(jnp.int32, (CHUNK, CHUNK), 0)
    si = lax.broadcasted_iota(jnp.int32, (CHUNK, CHUNK), 1)
    causal = si <= ti
    tri = causal.astype(F32)
    tri_t = (ti <= si).astype(F32)
    lane256 = lax.broadcasted_iota(jnp.int32, (CHUNK, HW), 1)
    ones_blk = jnp.ones((CHUNK, LANES), BF16)
    ones_sq = jnp.ones((LANES, LANES), BF16)

    def twice(a):
        return jnp.concatenate([a, a], axis=1)

    def rep(col):
        return jnp.broadcast_to(col, (CHUNK, LANES))

    def stage1(c):
        r0 = pl.multiple_of(c * CHUNK, CHUNK)
        qc = q_sc[pl.ds(r0, CHUNK), :]
        kc = k_sc[pl.ds(r0, CHUNK), :]
        vc = v_ref[0, pl.ds(r0, CHUNK), :]
        gcc = gcs[pl.ds(r0, CHUNK), :]
        grr = grs[c]
        b_c = jnp.dot(tri, gcc, preferred_element_type=F32, precision=HIGHEST)
        b_r = jnp.dot(grr, tri_t, preferred_element_type=F32, precision=HIGHEST)
        qstack = jnp.concatenate(
            [jnp.where((lane256 // M_QK_DIM) == h, qc, jnp.zeros_like(qc)) for h in range(M_HEADS)], axis=0)
        kstack = jnp.concatenate(
            [jnp.where((lane256 // M_QK_DIM) == h, kc, jnp.zeros_like(kc)) for h in range(M_HEADS)], axis=0)
        s_all = lax.dot_general(qstack, kc, (((1,), (1,)), ((), ())), preferred_element_type=F32)
        heads, vws = [], []
        for h in range(M_HEADS):
            bcol = rep(b_c[:, M_HEADS + h:M_HEADS + h + 1])
            icol = rep(gcc[:, h:h + 1])
            brow = b_r[M_HEADS + h:M_HEADS + h + 1, :]
            irow = grr[h:h + 1, :]
            b_last = bcol[CHUNK - 1:CHUNK, :]
            dlog = jnp.where(causal, bcol[:, :CHUNK] - brow + irow, -jnp.inf)
            m_intra = jnp.max(dlog, axis=-1, keepdims=True)
            vaug = jnp.concatenate([vc[:, h * M_V_DIM:(h + 1) * M_V_DIM], ones_blk], axis=1)
            a_col = b_last - bcol + icol
            m_loc = jnp.max(a_col, axis=0, keepdims=True)
            w_col = jnp.exp(a_col - m_loc)
            vws.append((vaug.astype(F32) * twice(w_col)).astype(BF16))
            heads.append((bcol, b_last, dlog, m_intra, vaug, m_loc))
        c_loc = lax.dot_general(kstack, jnp.concatenate(vws, axis=0), (((0,), (0,)), ((), ())),
                                preferred_element_type=F32)
        return r0, qstack, s_all, heads, c_loc

    def stage2(sts):
        items = []
        m_run = [msc[h:h + 1, :] for h in range(M_HEADS)]
        for r0, qstack, s_all, heads, c_loc in sts:
            i_all = _dot(qstack, cst[...].astype(BF16))
            for h in range(M_HEADS):
                bcol, b_last, dlog, m_intra, vaug, m_loc = heads[h]
                ks = slice(h * M_QK_DIM, (h + 1) * M_QK_DIM)
                m_prev = m_run[h]
                m_new = jnp.maximum(b_last + m_prev, m_loc)
                decay = jnp.exp(b_last + m_prev - m_new)
                fresh = jnp.exp(m_loc - m_new)
                cst[ks, :] = twice(decay) * cst[ks, :] + twice(fresh) * c_loc[ks, :]
                m_run[h] = m_new
                items.append((r0, h, bcol, dlog, m_intra, vaug, m_prev, s_all, i_all))
        for h in range(M_HEADS):
            msc[h:h + 1, :] = m_run[h]
        prods = []
        for r0, h, bcol, dlog, m_intra, vaug, m_prev, s_all, i_all in items:
            rs = slice(h * CHUNK, (h + 1) * CHUNK)
            m_inter = bcol + m_prev
            m_t = jnp.maximum(m_inter, m_intra)
            d_w = jnp.exp(dlog - m_t[:, :CHUNK])
            inter_w = jnp.exp(m_inter - m_t)
            p = (s_all[rs, :] * d_w).astype(BF16)
            prods.append((_dot(p, vaug) + twice(inter_w) * i_all[rs, :], m_t))
        normed = []
        for r, m_t in prods:
            hv = r[:, :M_V_DIM] / jnp.maximum(jnp.abs(r[:, M_V_DIM:]), jnp.exp(-m_t))
            hh = hv * hv
            hh_hi = hh.astype(BF16)
            hh_lo = (hh - hh_hi.astype(F32)).astype(BF16)
            normed.append((hv, _dot(hh_hi, ones_sq) + _dot(hh_lo, ones_sq)))
        for (r0, h, *_), (hv, ssq) in zip(items, normed):
            hs = slice(h * M_V_DIM, (h + 1) * M_V_DIM)
            hn = hv * lax.rsqrt(ssq * (1.0 / M_V_DIM) + EPS) * ng_ref[:, hs]
            og = o_ref[0, pl.ds(r0, CHUNK), hs].astype(F32)
            out_ref[0, pl.ds(r0, CHUNK), hs] = (hn * jax.nn.sigmoid(og)).astype(BF16)

    def chunk_group(cg, carry):
        stage2([stage1(CPT * cg + k) for k in range(CPT)])
        return carry

    lax.fori_loop(0, GCH // CPT, chunk_group, 0)


def _mlstm(p_all, gates_c, gates_r, bias_c, bias_r, conv_w, conv_b, ng):
    B, S, _ = p_all.shape
    T = GCH * CHUNK
    return pl.pallas_call(
        _mlstm_kernel,
        grid=(B, S // T),
        in_specs=[pl.BlockSpec((1, T, 512), lambda b, g: (b, g, 0)),
                  pl.BlockSpec((1, T, 512), lambda b, g: (b, g, 1)),
                  pl.BlockSpec((1, T, 512), lambda b, g: (b, g, 2)),
                  pl.BlockSpec((1, T, GW), lambda b, g: (b, g, 0)),
                  pl.BlockSpec((1, GCH, SUBLANES, CHUNK), lambda b, g: (b, g, 0, 0)),
                  pl.BlockSpec((1, GW), lambda b, g: (0, 0)),
                  pl.BlockSpec((SUBLANES, 1), lambda b, g: (0, 0)),
                  pl.BlockSpec((CONV_WIDTH, 512), lambda b, g: (0, 0)),
                  pl.BlockSpec((1, 512), lambda b, g: (0, 0)),
                  pl.BlockSpec((1, 512), lambda b, g: (0, 0))],
        out_specs=pl.BlockSpec((1, T, 512), lambda b, g: (b, g, 0)),
        out_shape=jax.ShapeDtypeStruct((B, S, 512), BF16),
        scratch_shapes=[pltpu.VMEM((T + SUBLANES, 512), F32),
                        pltpu.VMEM((T, 256), BF16),
                        pltpu.VMEM((T, 256), BF16),
                        pltpu.VMEM((T, GW), F32),
                        pltpu.VMEM((GCH, SUBLANES, CHUNK), F32),
                        pltpu.VMEM((M_HEADS * M_QK_DIM, 2 * M_V_DIM), F32),
                        pltpu.VMEM((SUBLANES, LANES), F32)],
        compiler_params=_cparams(("arbitrary", "arbitrary")),
        name="mlstm",
    )(p_all, p_all, p_all, gates_c, gates_r, bias_c, bias_r, conv_w, conv_b, ng)


HPS = 4


def _attn_kernel(lq1_ref, lk1_ref, lq2_ref, lk2_ref, q_ref, k_ref, v_ref, ng_ref, o_ref, *scratch):
    i = pl.program_id(2)
    qs_scs, m_scs, acc_scs, sa_scs, sb_scs = (scratch[n * HPS:(n + 1) * HPS] for n in range(5))
    heads = range(HPS)
    lanes_of = [slice(hh * LANES, (hh + 1) * LANES) for hh in heads]
    ones_blk = jnp.ones((TQ, LANES), BF16)

    for hh in heads:
        q = q_ref[0, :, lanes_of[hh]]
        lane = lax.broadcasted_iota(jnp.int32, q.shape, 1)
        qs_scs[hh][0:TQ, :] = jnp.where(lane < A_HEAD_DIM, q, jnp.zeros_like(q))
        qs_scs[hh][TQ:2 * TQ, :] = jnp.where(lane >= A_HEAD_DIM, q, jnp.zeros_like(q))
        m_scs[hh][...] = jnp.full(m_scs[hh].shape, -jnp.inf, F32)
        acc_scs[hh][...] = jnp.zeros_like(acc_scs[hh])

    def scores(hh, j):
        k = k_ref[0, pl.ds(pl.multiple_of(j * TQ, TQ), TQ), lanes_of[hh]]
        return lax.dot_general(qs_scs[hh][...], k, (((1,), (1,)), ((), ())), preferred_element_type=F32)

    def diag_scores(hh):
        ri = lax.broadcasted_iota(jnp.int32, (2 * TQ, TQ), 0)
        ci = lax.broadcasted_iota(jnp.int32, (2 * TQ, TQ), 1)
        visible = (ci // CHUNK) <= ((ri & (TQ - 1)) // CHUNK)
        return jnp.where(visible, scores(hh, i), -jnp.inf)

    def accumulate(hh, s_ref, j):
        s = s_ref[...]
        v = v_ref[0, pl.ds(pl.multiple_of(j * TQ, TQ), TQ), lanes_of[hh]]
        m_old = m_scs[hh][...]
        m_new = jnp.maximum(m_old, jnp.max(s, axis=-1, keepdims=True))
        alpha = jnp.exp(m_old - m_new)
        p = jnp.exp(s - jnp.concatenate([m_new] * (TQ // LANES), axis=1))
        pv = _dot(p.astype(BF16), jnp.concatenate([v, ones_blk], axis=1))
        acc_scs[hh][...] = jnp.concatenate([alpha, alpha], axis=1) * acc_scs[hh][...] + pv
        m_scs[hh][...] = m_new

    def kv_of(t):
        return jnp.where(t == 0, i, t - 1)

    for hh in heads:
        sa_scs[hh][...] = diag_scores(hh)
    pairs = i // 2

    def body(u, carry):
        t = 2 * u
        for hh in heads:
            sb_scs[hh][...] = scores(hh, t)
            accumulate(hh, sa_scs[hh], kv_of(t))
        for hh in heads:
            sa_scs[hh][...] = scores(hh, t + 1)
            accumulate(hh, sb_scs[hh], t)
        return carry

    lax.fori_loop(0, pairs, body, 0)
    t0 = 2 * pairs

    @pl.when(i == t0)
    def _():
        for hh in heads:
            accumulate(hh, sa_scs[hh], kv_of(t0))

    @pl.when(i > t0)
    def _():
        for hh in heads:
            sb_scs[hh][...] = scores(hh, t0)
            accumulate(hh, sa_scs[hh], kv_of(t0))
        for hh in heads:
            accumulate(hh, sb_scs[hh], t0)

    lam = (jnp.exp(jnp.sum(lq1_ref[...] * lk1_ref[...], axis=-1, keepdims=True))
           - jnp.exp(jnp.sum(lq2_ref[...] * lk2_ref[...], axis=-1, keepdims=True)) + LAMBDA_INIT)
    for hh in heads:
        acc = acc_scs[hh][...]
        o = acc[:, :LANES] / acc[:, LANES:]
        a = o[0:TQ, :] - lam * o[TQ:2 * TQ, :]
        y = a * lax.rsqrt(jnp.mean(a * a, axis=-1, keepdims=True) + EPS) * ng_ref[:, lanes_of[hh]]
        o_ref[0, :, lanes_of[hh]] = (y * (1.0 - LAMBDA_INIT)).astype(BF16)


def _attn(p_all, lq1, lk1, lq2, lk2, ng):
    B, S, _ = p_all.shape
    hw = HPS * LANES
    nsec = 512 // hw
    lam_spec = pl.BlockSpec((1, A_HEAD_DIM), lambda b, h, i: (0, 0))
    per_head = lambda shape, dtype: [pltpu.VMEM(shape, dtype) for _ in range(HPS)]
    return pl.pallas_call(
        _attn_kernel,
        grid=(B, A_HEADS // HPS, S // TQ),
        in_specs=[lam_spec, lam_spec, lam_spec, lam_spec,
                  pl.BlockSpec((1, TQ, hw), lambda b, h, i: (b, i, 3 * nsec + h)),
                  pl.BlockSpec((1, S, hw), lambda b, h, i: (b, 0, 4 * nsec + h)),
                  pl.BlockSpec((1, S, hw), lambda b, h, i: (b, 0, 5 * nsec + h)),
                  pl.BlockSpec((1, hw), lambda b, h, i: (0, h))],
        out_specs=pl.BlockSpec((1, TQ, hw), lambda b, h, i: (b, i, h)),
        out_shape=jax.ShapeDtypeStruct((B, S, 512), BF16),
        scratch_shapes=(per_head((2 * TQ, LANES), BF16) + per_head((2 * TQ, LANES), F32)
                        + per_head((2 * TQ, 2 * LANES), F32) + per_head((2 * TQ, TQ), F32)
                        + per_head((2 * TQ, TQ), F32)),
        compiler_params=_cparams(("arbitrary", "arbitrary", "arbitrary")),
        name="attn",
    )(lq1, lk1, lq2, lk2, p_all, p_all, p_all, ng)


def _outproj_kernel(hm_ref, ha_ref, x_ref, mod_ref, g2_ref, wo_ref, wrh_ref, wrl_ref, br_ref, ltri_ref,
                    x1_ref, h2_ref, idx_ref, gate_ref, rank_ref, cnt_ref, cnt_sc):
    @pl.when((pl.program_id(0) == 0) & (pl.program_id(1) == 0))
    def _():
        cnt_sc[...] = jnp.zeros_like(cnt_sc)

    hcat = jnp.concatenate([hm_ref[0], ha_ref[0]], axis=1)
    mix = _dot(hcat, wo_ref[...])
    gate1 = mod_ref[0, 2:3, :]
    shift2 = mod_ref[0, 3:4, :]
    scale2 = mod_ref[0, 4:5, :]
    x1 = x_ref[0] + gate1 * mix
    x1_ref[0] = x1
    ms = jnp.mean(x1 * x1, axis=-1, keepdims=True)
    h2 = (x1 * lax.rsqrt(ms + EPS) * g2_ref[...]) * (1.0 + scale2) + shift2
    tm = h2.shape[0]
    for s in range(SUBLANES):
        h2_ref[pl.ds(s, tm, stride=SUBLANES), :] = h2[:, s * LANES:(s + 1) * LANES]
    hb = h2.astype(BF16)
    hl = (h2 - hb.astype(F32)).astype(BF16)
    logits = _dot(hb, wrh_ref[...]) + _dot(hl, wrh_ref[...]) + _dot(hb, wrl_ref[...]) + br_ref[...]
    lane = lax.broadcasted_iota(jnp.int32, logits.shape, 1).astype(F32)
    vals, idxs = [], []
    work = logits
    for _ in range(TOP_K):
        mx = jnp.max(work, axis=-1, keepdims=True)
        ix = jnp.min(jnp.where(work == mx, lane, float(LANES)), axis=-1, keepdims=True)
        vals.append(mx)
        idxs.append(ix)
        work = jnp.where(lane == ix, -jnp.inf, work)
    es = [jnp.exp(v - vals[0]) for v in vals]
    tot = es[0] + es[1] + es[2] + es[3]
    gsel = jnp.zeros_like(logits)
    isel = jnp.zeros_like(logits)
    for k in range(TOP_K):
        gsel = jnp.where(lane == float(k), es[k] / tot, gsel)
        isel = jnp.where(lane == float(k), idxs[k], isel)
    gate_ref[0] = gsel[:, :TOP_K]
    idx_ref[0] = isel[:, :TOP_K].astype(jnp.int32)
    chosen = [lane == ix for ix in idxs]
    multi = jnp.zeros_like(logits)
    for ch in chosen:
        multi = jnp.where(ch, 1.0, multi)
    before = _dot(ltri_ref[...], multi.astype(BF16)) + cnt_sc[...]
    rsel = jnp.zeros_like(logits)
    for k in range(TOP_K):
        rk = jnp.sum(jnp.where(chosen[k], before, 0.0), axis=-1, keepdims=True)
        rsel = jnp.where(lane == float(k), rk, rsel)
    rank_ref[0] = rsel[:, :TOP_K].astype(jnp.int32)
    cnt_sc[...] = cnt_sc[...] + jnp.sum(multi, axis=0, keepdims=True)
    cnt_ref[...] = cnt_sc[...]


def _outproj(hm, ha, x, mod3, g2, wo, wr_hi, wr_lo, br, ltri):
    B, S, D = x.shape
    nt = S // TM
    return pl.pallas_call(
        _outproj_kernel,
        grid=(B, nt),
        in_specs=[pl.BlockSpec((1, TM, 512), lambda b, s: (b, s, 0)),
                  pl.BlockSpec((1, TM, 512), lambda b, s: (b, s, 0)),
                  pl.BlockSpec((1, TM, D), lambda b, s: (b, s, 0)),
                  pl.BlockSpec((1, 6, D), lambda b, s: (b, 0, 0)),
                  pl.BlockSpec((1, D), lambda b, s: (0, 0)),
                  pl.BlockSpec((D, D), lambda b, s: (0, 0)),
                  pl.BlockSpec((D, LANES), lambda b, s: (0, 0)),
                  pl.BlockSpec((D, LANES), lambda b, s: (0, 0)),
                  pl.BlockSpec((1, LANES), lambda b, s: (0, 0)),
                  pl.BlockSpec((TM, TM), lambda b, s: (0, 0))],
        out_specs=[pl.BlockSpec((1, TM, D), lambda b, s: (b, s, 0)),
                   pl.BlockSpec((TM * SUBLANES, LANES), lambda b, s: (b * nt + s, 0)),
                   pl.BlockSpec((1, TM, TOP_K), lambda b, s: (b, s, 0)),
                   pl.BlockSpec((1, TM, TOP_K), lambda b, s: (b, s, 0)),
                   pl.BlockSpec((1, TM, TOP_K), lambda b, s: (b, s, 0)),
                   pl.BlockSpec((1, LANES), lambda b, s: (0, 0))],
        out_shape=[jax.ShapeDtypeStruct((B, S, D), F32),
                   jax.ShapeDtypeStruct((B * S * SUBLANES, LANES), F32),
                   jax.ShapeDtypeStruct((B, S, TOP_K), jnp.int32),
                   jax.ShapeDtypeStruct((B, S, TOP_K), F32),
                   jax.ShapeDtypeStruct((B, S, TOP_K), jnp.int32),
                   jax.ShapeDtypeStruct((1, LANES), F32)],
        scratch_shapes=[pltpu.VMEM((1, LANES), F32)],
        compiler_params=_cparams(("arbitrary", "arbitrary")),
        name="outproj",
    )(hm, ha, x, mod3, g2, wo, wr_hi, wr_lo, br, ltri)


DMA_UNROLL = 8


def _row_copy(src, dst, src_row, dst_row, sem):
    def tile_start(row):
        start = row * SUBLANES
        return start if isinstance(start, int) else pl.multiple_of(start, SUBLANES)

    return pltpu.make_async_copy(
        src.at[pl.ds(tile_start(src_row), SUBLANES), :],
        dst.at[pl.ds(tile_start(dst_row), SUBLANES), :],
        sem)


def _wait_rows(hbm, n_rows, sem):
    span = hbm.at[pl.ds(0, n_rows * SUBLANES), :]
    pltpu.make_async_copy(span, span, sem).wait()


def _dispatch_kernel(pad_ref, pos_ref, h2_ref, wgu_ref, wdn_ref, psel_ref, xs_hbm, wgu_out, wdn_out,
                     zero_sc, sem, sem_pad, *, n_pad, steps_per_expert):
    i = pl.program_id(0)
    n = TOP_K * TM

    @pl.when(i == 0)
    def _():
        zero_sc[...] = jnp.zeros_like(zero_sc)

        def issue_pad(g, carry):
            for u in range(DMA_UNROLL):
                _row_copy(zero_sc, xs_hbm, 0, pad_ref[g * DMA_UNROLL + u], sem_pad).start(priority=u % 2)
            return carry

        lax.fori_loop(0, n_pad // DMA_UNROLL, issue_pad, 0)

    def issue(g, carry):
        for u in range(DMA_UNROLL):
            j = g * DMA_UNROLL + u
            tok = g * (DMA_UNROLL // TOP_K) + u // TOP_K
            _row_copy(h2_ref, xs_hbm, tok, pos_ref[0, 0, j], sem).start(priority=u % 2)
        return carry

    lax.fori_loop(0, n // DMA_UNROLL, issue, 0)

    @pl.when(i % steps_per_expert == 0)
    def _():
        half = wgu_out.shape[2] // 2
        for cblk in range(wgu_out.shape[2] // 256):
            blk = _dot(wgu_ref[0, :, cblk * 256:(cblk + 1) * 256].astype(BF16), psel_ref[...])
            wgu_out[0, :, cblk * LANES:(cblk + 1) * LANES] = blk[:, :LANES].astype(BF16)
            wgu_out[0, :, half + cblk * LANES:half + (cblk + 1) * LANES] = blk[:, LANES:].astype(BF16)
        wdn_out[0] = wdn_ref[0].astype(BF16)

    _wait_rows(xs_hbm, n, sem)

    @pl.when(i == 0)
    def _():
        _wait_rows(xs_hbm, n_pad, sem_pad)


def _dispatch(pad_rows, pos_tiles, h2_rows, w_gu, w_down, psel, n_rows):
    n_tiles = pos_tiles.shape[0]
    n_pad = pad_rows.shape[0]
    E, D, F2 = w_gu.shape
    assert n_tiles % E == 0, "dispatch steps must be a multiple of the expert count"
    spe = n_tiles // E
    wmap = lambda i, p: (i // spe, 0, 0)
    return pl.pallas_call(
        functools.partial(_dispatch_kernel, n_pad=n_pad, steps_per_expert=spe),
        grid_spec=pltpu.PrefetchScalarGridSpec(
            num_scalar_prefetch=1,
            grid=(n_tiles,),
            in_specs=[pl.BlockSpec((1, 1, TOP_K * TM), lambda i, p: (i, 0, 0), memory_space=pltpu.SMEM),
                      pl.BlockSpec((TM * SUBLANES, LANES), lambda i, p: (i, 0)),
                      pl.BlockSpec((1, D, F2), wmap),
                      pl.BlockSpec((1, F2 // 2, D), wmap),
                      pl.BlockSpec((256, 256), lambda i, p: (0, 0))],
            out_specs=[pl.BlockSpec(memory_space=pl.ANY),
                       pl.BlockSpec((1, D, F2), wmap),
                       pl.BlockSpec((1, F2 // 2, D), wmap)],
            scratch_shapes=[pltpu.VMEM((SUBLANES, LANES), F32),
                            pltpu.SemaphoreType.DMA(()),
                            pltpu.SemaphoreType.DMA(())]),
        out_shape=[jax.ShapeDtypeStruct((n_rows * SUBLANES, LANES), F32),
                   jax.ShapeDtypeStruct((E, D, F2), BF16),
                   jax.ShapeDtypeStruct((E, F2 // 2, D), BF16)],
        compiler_params=_cparams(("arbitrary",)),
        name="dispatch",
    )(pad_rows, pos_tiles, h2_rows, w_gu, w_down, psel)


def _expert_kernel(be_ref, nused_ref, xs_ref, wgu_ref, bg_ref, bu_ref, wd_ref, bd_ref, ys_ref):
    i = pl.program_id(0)

    @pl.when(i < nused_ref[0])
    def _():
        x = jnp.concatenate([xs_ref[pl.ds(s, RB, stride=SUBLANES), :] for s in range(SUBLANES)], axis=1).astype(BF16)
        gu = _dot(x, wgu_ref[0])
        half = gu.shape[1] // 2
        gate = jnp.minimum(gu[:, :half] + bg_ref[0], SWIGLU_LIMIT)
        up = jnp.clip(gu[:, half:] + bu_ref[0], -SWIGLU_LIMIT, SWIGLU_LIMIT)
        act = (up + 1.0) * (gate * jax.nn.sigmoid(SWIGLU_ALPHA * gate))
        out = _dot(act.astype(BF16), wd_ref[0]) + bd_ref[0]
        for s in range(SUBLANES):
            ys_ref[pl.ds(s, RB, stride=SUBLANES), :] = out[:, s * LANES:(s + 1) * LANES]

    @pl.when(i >= nused_ref[0])
    def _():
        ys_ref[...] = jnp.zeros_like(ys_ref)


def _experts(block_expert, nused, xs, wgu, bg, bu, wd, bd, n_blocks):
    D = wgu.shape[1]
    F = wgu.shape[2] // 2
    wmap = lambda i, be, n: (be[i], 0, 0)
    return pl.pallas_call(
        _expert_kernel,
        grid_spec=pltpu.PrefetchScalarGridSpec(
            num_scalar_prefetch=2,
            grid=(n_blocks,),
            in_specs=[pl.BlockSpec((RB * SUBLANES, LANES), lambda i, be, n: (i, 0)),
                      pl.BlockSpec((1, D, 2 * F), wmap),
                      pl.BlockSpec((1, 1, F), wmap),
                      pl.BlockSpec((1, 1, F), wmap),
                      pl.BlockSpec((1, F, D), wmap),
                      pl.BlockSpec((1, 1, D), wmap)],
            out_specs=pl.BlockSpec((RB * SUBLANES, LANES), lambda i, be, n: (i, 0))),
        out_shape=jax.ShapeDtypeStruct((n_blocks * RB * SUBLANES, LANES), F32),
        compiler_params=_cparams(("arbitrary",)),
        name="experts",
    )(block_expert, nused, xs, wgu, bg, bu, wd, bd)


CB_TRIPS = 32


def _combine_kernel(pos_ref, posn_ref, ys_hbm, x1_ref, mod_ref, gate_ref, o_ref, buf_a, buf_b, sem_a, sem_b):
    s = pl.program_id(0)
    last = pl.num_programs(0) - 1
    n = TOP_K * TC
    per_trip = n // CB_TRIPS
    ts = TC // CB_TRIPS
    gate2 = mod_ref[0, 5:6, :]

    def issue_group(p_ref, buf, sem, g):
        for u in range(per_trip):
            j = g * per_trip + u
            _row_copy(ys_hbm, buf, p_ref[0, 0, j], j, sem).start(priority=u % 2)

    def sum_strip(buf, g):
        t0 = pl.multiple_of(g * ts, ts)
        gates = gate_ref[0, pl.ds(t0, ts), :]
        y = None
        for k in range(TOP_K):
            base = (k * TC + t0) * SUBLANES
            rows = jnp.concatenate(
                [buf[pl.ds(base + sl, ts, stride=SUBLANES), :] for sl in range(SUBLANES)], axis=1)
            term = gates[:, k:k + 1] * rows
            y = term if y is None else y + term
        o_ref[0, pl.ds(t0, ts), :] = x1_ref[0, pl.ds(t0, ts), :] + gate2 * y

    def step(cur_buf, cur_sem, nxt_buf, nxt_sem):
        @pl.when(s == 0)
        def _():
            def first(g, carry):
                issue_group(pos_ref, cur_buf, cur_sem, g)
                return carry

            lax.fori_loop(0, CB_TRIPS, first, 0)

        _wait_rows(ys_hbm, n, cur_sem)

        @pl.when(s < last)
        def _():
            def fused(g, carry):
                issue_group(posn_ref, nxt_buf, nxt_sem, g)
                sum_strip(cur_buf, g)
                return carry

            lax.fori_loop(0, CB_TRIPS, fused, 0)

        @pl.when(s == last)
        def _():
            def tail(g, carry):
                sum_strip(cur_buf, g)
                return carry

            lax.fori_loop(0, CB_TRIPS, tail, 0)

    @pl.when(s % 2 == 0)
    def _():
        step(buf_a, sem_a, buf_b, sem_b)

    @pl.when(s % 2 == 1)
    def _():
        step(buf_b, sem_b, buf_a, sem_a)


def _combine(pos_tiles, ys, x1, mod3, gates):
    B, S, D = x1.shape
    nt = S // TC
    n_steps = B * nt
    pos_spec = lambda off: pl.BlockSpec((1, 1, TOP_K * TC), lambda s: (jnp.minimum(s + off, n_steps - 1), 0, 0),
                                        memory_space=pltpu.SMEM)
    return pl.pallas_call(
        _combine_kernel,
        grid=(n_steps,),
        in_specs=[pos_spec(0), pos_spec(1),
                  pl.BlockSpec(memory_space=pl.ANY),
                  pl.BlockSpec((1, TC, D), lambda s: (s // nt, s % nt, 0)),
                  pl.BlockSpec((1, 6, D), lambda s: (s // nt, 0, 0)),
                  pl.BlockSpec((1, TC, TOP_K), lambda s: (s // nt, s % nt, 0))],
        out_specs=pl.BlockSpec((1, TC, D), lambda s: (s // nt, s % nt, 0)),
        out_shape=jax.ShapeDtypeStruct((B, S, D), F32),
        scratch_shapes=[pltpu.VMEM((TOP_K * TC * SUBLANES, LANES), F32),
                        pltpu.VMEM((TOP_K * TC * SUBLANES, LANES), F32),
                        pltpu.SemaphoreType.DMA(()),
                        pltpu.SemaphoreType.DMA(())],
        compiler_params=_cparams(("arbitrary",)),
        name="combine",
    )(pos_tiles, pos_tiles, ys, x1, mod3, gates)


def _lookup(table, idx):
    hit = idx[..., None] == jnp.arange(table.shape[0], dtype=jnp.int32)
    return jnp.sum(jnp.where(hit, table, 0), axis=-1).astype(jnp.int32)


def _routing_tables(top_idx, rank, counts, n_tokens):
    n_slots = n_tokens * TOP_K
    n_rows = n_slots + N_EXPERTS * RB
    n_blocks = n_rows // RB
    sizes = counts[0, :N_EXPERTS].astype(jnp.int32)
    padded = ((sizes + RB - 1) // RB) * RB
    padded_end = jnp.cumsum(padded).astype(jnp.int32)
    padded_start = padded_end - padded
    pos = _lookup(padded_start, top_idx) + rank
    n_used = padded_end[-1] // RB
    blk = jnp.arange(n_blocks, dtype=jnp.int32)
    block_expert = jnp.minimum(jnp.sum(padded_end[None, :] <= (blk * RB)[:, None], axis=1), N_EXPERTS - 1)
    last_e = jnp.minimum(jnp.sum(padded_end <= (n_used - 1) * RB), N_EXPERTS - 1)
    block_expert = jnp.where(blk < n_used, block_expert, last_e).astype(jnp.int32)
    pad_cnt = padded - sizes
    pad_end = jnp.cumsum(pad_cnt).astype(jnp.int32)
    j = jnp.arange(N_EXPERTS * RB, dtype=jnp.int32)
    e_j = jnp.sum(pad_end[None, :] <= j[:, None], axis=1).astype(jnp.int32)
    in_group = e_j < N_EXPERTS
    e_c = jnp.minimum(e_j, N_EXPERTS - 1)
    group_row = _lookup(padded_start + sizes - (pad_end - pad_cnt), e_c) + j
    tail_row = padded_end[-1] + (j - pad_end[-1])
    pad_rows = jnp.where(in_group, group_row, tail_row).astype(jnp.int32)
    return pos, pad_rows, block_expert, n_used.reshape(1).astype(jnp.int32), n_rows, n_blocks


def kernel(x, c, w_ada, b_ada, norm1_g, w_in, conv_w, conv_b, b_igate, b_fgate, mlstm_norm_g, q_norm_g, k_norm_g,
           lambda_q1, lambda_k1, lambda_q2, lambda_k2, diff_norm_g, w_out, norm2_g, w_router, b_router, w_gu, b_gu,
           w_down, b_down):
    B, S, D = x.shape
    N = B * S
    l = 0
    o_mq, o_mv, o_mo = 0, 512, 1024
    o_mi, o_mf = 1536, 1540
    o_aq, o_ak, o_av = 1544, 2056, 2568

    wi = w_in[l]
    w_main = jnp.concatenate([wi[:, o_mq:o_mq + 512], wi[:, o_mv:o_mv + 512], wi[:, o_mo:o_mo + 512],
                              wi[:, o_aq:o_aq + 512], wi[:, o_ak:o_ak + 512], wi[:, o_av:o_av + 512]],
                             axis=1).astype(BF16)
    def hi_lo_lanes(w, width):
        hi = w.astype(BF16)
        lo = (w - hi.astype(F32)).astype(BF16)
        return jnp.pad(jnp.concatenate([hi, lo], axis=1), ((0, 0), (0, width - 2 * w.shape[1])))

    wg2 = hi_lo_lanes(wi[:, o_mi:o_mi + 2 * M_HEADS], GW)
    pos_ids = jnp.arange(S, dtype=F32)
    inv_freq = ROPE_THETA ** (-jnp.arange(0, A_HEAD_DIM, 2, dtype=F32) / A_HEAD_DIM)
    ang = pos_ids[:, None] * inv_freq[None, :]
    cos_h, sin_h = jnp.cos(ang), jnp.sin(ang)
    cos_t = jnp.concatenate([cos_h, cos_h, cos_h, cos_h], axis=1)
    sin_t = jnp.concatenate([-sin_h, sin_h, -sin_h, sin_h], axis=1)
    qg = jnp.tile(q_norm_g[l], 512 // A_HEAD_DIM).reshape(1, 512)
    kg = jnp.tile(k_norm_g[l], 512 // A_HEAD_DIM).reshape(1, 512)
    gid = jnp.arange(256) // A_HEAD_DIM
    bd = (gid[:, None] == gid[None, :]).astype(BF16)
    bias_c = jnp.pad(jnp.concatenate([b_igate[l], b_fgate[l]]), (0, GW - 2 * M_HEADS)).reshape(1, GW)
    bias_r = jnp.concatenate([b_igate[l], b_fgate[l]]).reshape(2 * M_HEADS, 1)
    wr = jnp.pad(w_router[l], ((0, 0), (0, LANES - N_EXPERTS)))
    wr_hi = wr.astype(BF16)
    wr_lo = (wr - wr_hi.astype(F32)).astype(BF16)
    br = jnp.pad(b_router[l], (0, LANES - N_EXPERTS), constant_values=-jnp.inf).reshape(1, LANES)
    col = jnp.arange(256)
    psel = (jnp.where(col % 2 == 0, col // 2, LANES + col // 2)[:, None] == col[None, :]).astype(BF16)
    ltri = (jnp.arange(TM)[None, :] < jnp.arange(TM)[:, None]).astype(BF16)
    bgate = b_gu[l][:, None, 0::2]
    bup = b_gu[l][:, None, 1::2]
    bdn = b_down[l][:, None, :]

    mod3 = _ada(c, w_ada[l], b_ada[l]).reshape(B, 6, D)
    p_all, gates_c = _inproj(x, mod3, norm1_g[l].reshape(1, D), w_main, wg2, cos_t, sin_t, qg, kg, bd)
    gates_r = gates_c[:, :, :2 * M_HEADS].reshape(B, S // CHUNK, CHUNK, 2 * M_HEADS).transpose(0, 1, 3, 2)
    hm = _mlstm(p_all, gates_c, gates_r, bias_c, bias_r, conv_w[l], conv_b[l].reshape(1, 512),
                mlstm_norm_g[l].reshape(1, 512))
    ha = _attn(p_all, lambda_q1[l].reshape(1, -1), lambda_k1[l].reshape(1, -1), lambda_q2[l].reshape(1, -1),
               lambda_k2[l].reshape(1, -1), diff_norm_g[l].reshape(1, 512))

    x1, h2_rows, top_idx, gates, rank, counts = _outproj(hm, ha, x, mod3, norm2_g[l].reshape(1, D),
                                                         w_out[l].astype(BF16), wr_hi, wr_lo, br, ltri)

    pos, pad_rows, block_expert, n_used, n_rows, n_blocks = _routing_tables(top_idx, rank, counts, N)
    xs, wgu, wdn = _dispatch(pad_rows, pos.reshape(N // TM, 1, TM * TOP_K), h2_rows, w_gu[l], w_down[l], psel, n_rows)
    ys = _experts(block_expert, n_used, xs, wgu, bgate, bup, wdn, bdn, n_blocks)
    pos_tiles = pos.reshape(N // TC, TC, TOP_K).transpose(0, 2, 1).reshape(N // TC, 1, TOP_K * TC)
    return _combine(pos_tiles, ys, x1, mod3, gates)
```
